```python
import jax
import jax.numpy as jnp
from jax import lax
import numpy as np

D_MODEL = 1024
BATCH = 8
SEQ = 4096
DEPTH = 2

CTX_LEN = 256
GRID_W = 64
HEAD_DIM = 64
NORM_EPS = 1e-6
ROPE_BASE = 10000.0
NEG_INF = -1e30
Q_BLOCK = 128

MLA_HEADS = 4
MLA_Q_RANK = 192
MLA_KV_RANK = 128
MLA_NOPE = 64
MLA_ROPE = 32
MLA_V = 64

SWA_Q_HEADS = 8
SWA_KV_HEADS = 2
SWA_WINDOW = 128
SWA_BLOCK = 128

RET_HEADS = 4
RET_DK = 64
RET_DV = 64
RET_CHUNK = 128

D_FF = 2816
N_EXPERTS = 8
TOP_K = 2
D_FF_EXPERT = 3584

IN_SIZES = (MLA_Q_RANK, MLA_KV_RANK, MLA_ROPE,
            SWA_Q_HEADS * HEAD_DIM, SWA_KV_HEADS * HEAD_DIM, SWA_KV_HEADS * HEAD_DIM,
            RET_HEADS * RET_DK, RET_HEADS * RET_DK, RET_HEADS * RET_DV, RET_HEADS * RET_DV)
IN_DIM = sum(IN_SIZES)
MIX_DIM = MLA_HEADS * MLA_V + SWA_Q_HEADS * HEAD_DIM + RET_HEADS * RET_DV
N_DENSE = (DEPTH + 1) // 2
N_MOE = DEPTH // 2

kernel_name = 'hybrid_mla_swa_retention_moe_dit'


def _rms(x):
    xf = x.astype(jnp.float32)
    return xf * lax.rsqrt(jnp.mean(xf * xf, axis=-1, keepdims=True) + NORM_EPS)


def rms_norm(x, g):
    return (_rms(x) * g.astype(jnp.float32)).astype(x.dtype)


def modulate(h, shift, scale):
    return h * (1.0 + scale) + shift


def rope_angles(pos, dim):
    inv = ROPE_BASE ** (-jnp.arange(0, dim, 2, dtype=jnp.float32) / dim)
    ang = pos.astype(jnp.float32)[:, None] * inv[None, :]
    ang = jnp.concatenate([ang, ang], axis=-1)
    return jnp.cos(ang), jnp.sin(ang)


def apply_rope(x, cos, sin):
    x1, x2 = jnp.split(x, 2, axis=-1)
    rot = jnp.concatenate([-x2, x1], axis=-1)
    return (x * cos[:, None, :] + rot * sin[:, None, :]).astype(x.dtype)


def axial_rope(x, rows, cols):
    half = x.shape[-1] // 2
    cr, sr = rope_angles(rows, half)
    cc, sc = rope_angles(cols, half)
    return jnp.concatenate([apply_rope(x[..., :half], cr, sr),
                            apply_rope(x[..., half:], cc, sc)], axis=-1)


def split_proj(p):
    cuts = [int(v) for v in np.cumsum(IN_SIZES)[:-1]]
    return jnp.split(p, cuts, axis=-1)


def flip_seq(t):
    return t[:, ::-1]


def mla_queries(cq, q_norm, w_uq, rows, cols):
    B, L, _ = cq.shape
    q = (rms_norm(cq, q_norm) @ w_uq).reshape(B, L, MLA_HEADS, MLA_NOPE + MLA_ROPE)
    q_pe = q[..., MLA_NOPE:]
    if rows is not None:
        q_pe = axial_rope(q_pe, rows, cols)
    return jnp.concatenate([q[..., :MLA_NOPE], q_pe], axis=-1)


def mla_keys_values(ckv, k_pe, kv_norm, w_ukv, rows, cols):
    B, L, _ = ckv.shape
    kv = (rms_norm(ckv, kv_norm) @ w_ukv).reshape(B, L, MLA_HEADS, MLA_NOPE + MLA_V)
    k_pe = k_pe[:, :, None, :]
    if rows is not None:
        k_pe = axial_rope(k_pe, rows, cols)
    k = jnp.concatenate([kv[..., :MLA_NOPE],
                         jnp.broadcast_to(k_pe, (B, L, MLA_HEADS, MLA_ROPE))], axis=-1)
    return k, kv[..., MLA_NOPE:]


def block_attention(q, k, v, scale):
    B, L, H, d = q.shape
    nb = L // Q_BLOCK
    q_blocks = q.reshape(B, nb, Q_BLOCK, H, d).transpose(1, 0, 2, 3, 4)

    def one_block(q_blk):
        s = jnp.einsum('bqhd,bkhd->bhqk', q_blk, k, preferred_element_type=jnp.float32) * scale
        p = jax.nn.softmax(s, axis=-1).astype(v.dtype)
        return jnp.einsum('bhqk,bkhd->bqhd', p, v)

    o = lax.map(one_block, q_blocks)
    return o.transpose(1, 0, 2, 3, 4).reshape(B, L, -1)


def sink_softmax(s, sink_g):
    sk = jnp.broadcast_to(sink_g.astype(jnp.float32)[None, :, :, None, None], s.shape[:-1] + (1,))
    return jax.nn.softmax(jnp.concatenate([s, sk], axis=-1), axis=-1)[..., :-1]


def banded_sink_attention(q, k, v, k_ctx, v_ctx, sink):
    B, L, Hq, d = q.shape
    Hkv = k.shape[2]
    G = Hq // Hkv
    W = SWA_BLOCK
    nb = L // W
    scale = d ** -0.5
    pad = ((0, 0), (W, W), (0, 0), (0, 0))
    k_pad = jnp.pad(k, pad)
    v_pad = jnp.pad(v, pad)
    q_blocks = q.reshape(B, nb, W, Hkv, G, d).transpose(1, 0, 2, 3, 4, 5)
    offs = jnp.arange(3 * W) - W
    rel = offs[None, :] - jnp.arange(W)[:, None]
    in_band = jnp.abs(rel) <= SWA_WINDOW
    sink_g = sink.reshape(Hkv, G)

    def one_block(args):
        i, q_blk = args
        start = i * W
        k_blk = lax.dynamic_slice_in_dim(k_pad, start, 3 * W, axis=1)
        v_blk = lax.dynamic_slice_in_dim(v_pad, start, 3 * W, axis=1)
        key_t = start + offs
        valid = in_band & ((key_t >= 0) & (key_t < L))[None, :]
        s_band = jnp.einsum('bqhgd,bkhd->bhgqk', q_blk, k_blk,
                            preferred_element_type=jnp.float32) * scale
        s_band = jnp.where(valid, s_band, NEG_INF)
        s_ctx = jnp.einsum('bqhgd,bkhd->bhgqk', q_blk, k_ctx,
                           preferred_element_type=jnp.float32) * scale
        p = sink_softmax(jnp.concatenate([s_band, s_ctx], axis=-1), sink_g)
        p_band = p[..., :3 * W].astype(v.dtype)
        p_ctx = p[..., 3 * W:].astype(v.dtype)
        return (jnp.einsum('bhgqk,bkhd->bqhgd', p_band, v_blk)
                + jnp.einsum('bhgqk,bkhd->bqhgd', p_ctx, v_ctx))

    o = lax.map(one_block, (jnp.arange(nb), q_blocks))
    return o.transpose(1, 0, 2, 3, 4, 5).reshape(B, L, Hq * d)


def full_sink_attention(q, k, v, sink):
    B, L, Hq, d = q.shape
    Hkv = k.shape[2]
    G = Hq // Hkv
    qg = q.reshape(B, L, Hkv, G, d)
    s = jnp.einsum('bqhgd,bkhd->bhgqk', qg, k, preferred_element_type=jnp.float32) * d ** -0.5
    p = sink_softmax(s, sink.reshape(Hkv, G)).astype(v.dtype)
    return jnp.einsum('bhgqk,bkhd->bqhgd', p, v).reshape(B, L, Hq * d)


def retention_scan(q, k, v, gamma, state0):
    B, L, H, _ = q.shape
    dv = v.shape[-1]
    C = RET_CHUNK
    n = L // C
    log_g = jnp.log(gamma)[:, None]
    idx = jnp.arange(C, dtype=jnp.float32)
    diff = idx[:, None] - idx[None, :]
    intra = jnp.where(diff >= 0, jnp.exp(log_g[:, :, None] * jnp.maximum(diff, 0.0)), 0.0)
    q_dec = jnp.exp(log_g * (idx + 1.0))[:, :, None]
    k_dec = jnp.exp(log_g * (C - 1.0 - idx))[:, :, None]
    chunk_dec = jnp.exp(log_g * C)[:, :, None]

    def to_chunks(t):
        return t.astype(jnp.float32).reshape(B, n, C, H, -1).transpose(1, 0, 3, 2, 4)

    def step(state, inp):
        qi, ki, vi = inp
        att = jnp.einsum('bhqd,bhkd->bhqk', qi, ki) * intra
        out = (jnp.einsum('bhqk,bhkv->bhqv', att, vi)
               + jnp.einsum('bhqd,bhdv->bhqv', qi * q_dec, state))
        state = chunk_dec * state + jnp.einsum('bhkd,bhkv->bhdv', ki * k_dec, vi)
        return state, out

    state, out = lax.scan(step, state0, (to_chunks(q), to_chunks(k), to_chunks(v)))
    return out.transpose(1, 0, 3, 2, 4).reshape(B, L, H, dv), state


def retention_output(o, g):
    B, L = g.shape[:2]
    y = _rms(o).reshape(B, L, -1)
    return (y * jax.nn.silu(g.astype(jnp.float32))).astype(g.dtype)


def hybrid_mixer(h, hc, w_in, q_norm, w_uq, kv_norm, w_ukv, sink, decay_f, decay_b, w_out,
                 need_ctx_out):
    B, L, _ = h.shape
    Lc = hc.shape[1]
    n_rows = L // GRID_W
    t = jnp.arange(n_rows * GRID_W)
    rows, cols = t // GRID_W, t % GRID_W
    cq, ckv, kpe, sq, sk, sv, rq, rk, rv, rg = split_proj(h @ w_in)
    cq_x, ckv_x, kpe_x, sq_x, sk_x, sv_x, rq_x, rk_x, rv_x, rg_x = split_proj(hc @ w_in)

    mla_scale = (MLA_NOPE + MLA_ROPE) ** -0.5
    k_a, v_a = mla_keys_values(ckv, kpe, kv_norm, w_ukv, rows, cols)
    k_a_x, v_a_x = mla_keys_values(ckv_x, kpe_x, kv_norm, w_ukv, None, None)
    q_a = mla_queries(cq, q_norm, w_uq, rows, cols)
    o_a = block_attention(q_a, jnp.concatenate([k_a, k_a_x], axis=1),
                          jnp.concatenate([v_a, v_a_x], axis=1), mla_scale)

    q_b = axial_rope(sq.reshape(B, L, SWA_Q_HEADS, HEAD_DIM), rows, cols)
    k_b = axial_rope(sk.reshape(B, L, SWA_KV_HEADS, HEAD_DIM), rows, cols)
    v_b = sv.reshape(B, L, SWA_KV_HEADS, HEAD_DIM)
    k_b_x = sk_x.reshape(B, Lc, SWA_KV_HEADS, HEAD_DIM)
    v_b_x = sv_x.reshape(B, Lc, SWA_KV_HEADS, HEAD_DIM)
    o_b = banded_sink_attention(q_b, k_b, v_b, k_b_x, v_b_x, sink)

    gamma_f = jax.nn.sigmoid(decay_f.astype(jnp.float32))
    gamma_b = jax.nn.sigmoid(decay_b.astype(jnp.float32))
    cos, sin = rope_angles(t, RET_DK)
    k_scale = RET_DK ** -0.5
    q_c = apply_rope(rq.reshape(B, L, RET_HEADS, RET_DK), cos, sin)
    k_c = apply_rope(rk.reshape(B, L, RET_HEADS, RET_DK), cos, sin) * k_scale
    v_c = rv.reshape(B, L, RET_HEADS, RET_DV)
    q_c_x = rq_x.reshape(B, Lc, RET_HEADS, RET_DK)
    k_c_x = rk_x.reshape(B, Lc, RET_HEADS, RET_DK) * k_scale
    v_c_x = rv_x.reshape(B, Lc, RET_HEADS, RET_DV)
    zero_state = jnp.zeros((B, RET_HEADS, RET_DK, RET_DV), jnp.float32)
    oc_f, s_f = retention_scan(q_c_x, k_c_x, v_c_x, gamma_f, zero_state)
    oc_b, s_b = retention_scan(flip_seq(q_c_x), flip_seq(k_c_x), flip_seq(v_c_x), gamma_b, zero_state)
    o_f, _ = retention_scan(q_c, k_c, v_c, gamma_f, s_f)
    o_bk, _ = retention_scan(flip_seq(q_c), flip_seq(k_c), flip_seq(v_c), gamma_b, s_b)
    o_c = retention_output(o_f + flip_seq(o_bk), rg)

    y = jnp.concatenate([o_a, o_b, o_c], axis=-1) @ w_out
    if not need_ctx_out:
        return y, None

    q_a_x = mla_queries(cq_x, q_norm, w_uq, None, None)
    oc_a = block_attention(q_a_x, k_a_x, v_a_x, mla_scale)
    oc_bw = full_sink_attention(sq_x.reshape(B, Lc, SWA_Q_HEADS, HEAD_DIM), k_b_x, v_b_x, sink)
    oc_c = retention_output(oc_f + flip_seq(oc_b), rg_x)
    yc = jnp.concatenate([oc_a, oc_bw, oc_c], axis=-1) @ w_out
    return y, yc


def swiglu(h, w_gate, w_up, w_down):
    return (jax.nn.silu(h @ w_gate) * (h @ w_up)) @ w_down


def moe_swiglu(h, router, w_gate, w_up, w_down):
    logits = (h @ router).astype(jnp.float32)
    top_v, top_i = lax.top_k(logits, TOP_K)
    top_w = jax.nn.softmax(top_v, axis=-1)
    gates = jnp.sum(jax.nn.one_hot(top_i, N_EXPERTS, dtype=jnp.float32) * top_w[..., None], axis=-2)
    out = jnp.zeros_like(h)
    for e in range(N_EXPERTS):
        out = out + gates[..., e:e + 1].astype(h.dtype) * swiglu(h, w_gate[e], w_up[e], w_down[e])
    return out


def channel_mixer(h, layer, ffn_w_gate, ffn_w_up, ffn_w_down, moe_router, moe_w_gate, moe_w_up,
                  moe_w_down):
    i = layer // 2
    if layer % 2 == 0:
        return swiglu(h, ffn_w_gate[i], ffn_w_up[i], ffn_w_down[i])
    return moe_swiglu(h, moe_router[i], moe_w_gate[i], moe_w_up[i], moe_w_down[i])


def setup_inputs(seed: int = 0) -> dict:
    key = jax.random.key(seed)
    keys = iter(jax.random.split(key, 32))
    f32 = jnp.float32

    def normal(shape, scale):
        return jax.random.normal(next(keys), shape, f32) * scale

    def gain(shape):
        return 1.0 + normal(shape, 0.02)

    gamma0 = 1.0 - 2.0 ** (-5.0 - jnp.arange(RET_HEADS, dtype=f32))
    decay_logit0 = jnp.log(gamma0) - jnp.log1p(-gamma0)
    return {
        'x': normal((BATCH, SEQ, D_MODEL), 1.0),
        'c': normal((BATCH, D_MODEL), 1.0),
        'ctx': normal((BATCH, CTX_LEN, D_MODEL), 1.0),
        'c_ctx': normal((D_MODEL,), 1.0),
        'w_mod': normal((DEPTH, D_MODEL, 6 * D_MODEL), 0.5 * D_MODEL ** -0.5),
        'b_mod': normal((DEPTH, 6 * D_MODEL), 0.01),
        'norm1_g': gain((DEPTH, D_MODEL)),
        'norm2_g': gain((DEPTH, D_MODEL)),
        'w_in': normal((DEPTH, D_MODEL, IN_DIM), D_MODEL ** -0.5),
        'mla_q_norm': gain((DEPTH, MLA_Q_RANK)),
        'mla_w_uq': normal((DEPTH, MLA_Q_RANK, MLA_HEADS * (MLA_NOPE + MLA_ROPE)), MLA_Q_RANK ** -0.5),
        'mla_kv_norm': gain((DEPTH, MLA_KV_RANK)),
        'mla_w_ukv': normal((DEPTH, MLA_KV_RANK, MLA_HEADS * (MLA_NOPE + MLA_V)), MLA_KV_RANK ** -0.5),
        'swa_sink': normal((DEPTH, SWA_Q_HEADS), 0.5),
        'ret_decay_fwd': decay_logit0 + normal((DEPTH, RET_HEADS), 0.05),
        'ret_decay_bwd': decay_logit0 + normal((DEPTH, RET_HEADS), 0.05),
        'w_out': normal((DEPTH, MIX_DIM, D_MODEL), MIX_DIM ** -0.5),
        'ffn_w_gate': normal((N_DENSE, D_MODEL, D_FF), D_MODEL ** -0.5),
        'ffn_w_up': normal((N_DENSE, D_MODEL, D_FF), D_MODEL ** -0.5),
        'ffn_w_down': normal((N_DENSE, D_FF, D_MODEL), D_FF ** -0.5),
        'moe_router': normal((N_MOE, D_MODEL, N_EXPERTS), D_MODEL ** -0.5),
        'moe_w_gate': normal((N_MOE, N_EXPERTS, D_MODEL, D_FF_EXPERT), D_MODEL ** -0.5),
        'moe_w_up': normal((N_MOE, N_EXPERTS, D_MODEL, D_FF_EXPERT), D_MODEL ** -0.5),
        'moe_w_down': normal((N_MOE, N_EXPERTS, D_FF_EXPERT, D_MODEL), D_FF_EXPERT ** -0.5),
        'final_norm_g': gain((D_MODEL,)),
    }


def reference(x, c, ctx, c_ctx, w_mod, b_mod, norm1_g, norm2_g, w_in, mla_q_norm, mla_w_uq,
              mla_kv_norm, mla_w_ukv, swa_sink, ret_decay_fwd, ret_decay_bwd, w_out,
              ffn_w_gate, ffn_w_up, ffn_w_down, moe_router, moe_w_gate, moe_w_up, moe_w_down,
              final_norm_g):
    xc = ctx
    cond_lat = jax.nn.silu(c)
    cond_ctx = jax.nn.silu(c_ctx)[None]
    for layer in range(DEPTH):
        last = layer == DEPTH - 1
        mod = (cond_lat @ w_mod[layer] + b_mod[layer])[:, None, :]
        mod_x = (cond_ctx @ w_mod[layer] + b_mod[layer])[:, None, :]
        sh1, sc1, g1, sh2, sc2, g2 = jnp.split(mod, 6, axis=-1)
        sh1x, sc1x, g1x, sh2x, sc2x, g2x = jnp.split(mod_x, 6, axis=-1)

        h = modulate(rms_norm(x, norm1_g[layer]), sh1, sc1)
        hc = modulate(rms_norm(xc, norm1_g[layer]), sh1x, sc1x)
        y, yc = hybrid_mixer(h, hc, w_in[layer], mla_q_norm[layer], mla_w_uq[layer],
                             mla_kv_norm[layer], mla_w_ukv[layer], swa_sink[layer],
                             ret_decay_fwd[layer], ret_decay_bwd[layer], w_out[layer],
                             not last)
        x = x + g1 * y
        h = modulate(rms_norm(x, norm2_g[layer]), sh2, sc2)
        x = x + g2 * channel_mixer(h, layer, ffn_w_gate, ffn_w_up, ffn_w_down, moe_router,
                                   moe_w_gate, moe_w_up, moe_w_down)
        if not last:
            xc = xc + g1x * yc
            hcf = modulate(rms_norm(xc, norm2_g[layer]), sh2x, sc2x)
            xc = xc + g2x * channel_mixer(hcf, layer, ffn_w_gate, ffn_w_up, ffn_w_down, moe_router,
                                          moe_w_gate, moe_w_up, moe_w_down)
    return rms_norm(x, final_norm_g)
```

```python
import functools
import math

import numpy as np
import jax
import jax.numpy as jnp
from jax import lax
from jax.experimental import pallas as pl
from jax.experimental.pallas import tpu as pltpu

F32 = jnp.float32
BF16 = jnp.bfloat16

D_MODEL = 1024
DEPTH = 2
GRID_W = 64
HEAD_DIM = 64
NORM_EPS = 1e-6
ROPE_BASE = 10000.0
NEG_INF = -1e30

MLA_HEADS = 4
MLA_Q_RANK = 192
MLA_KV_RANK = 128
MLA_NOPE = 64
MLA_ROPE = 32
MLA_V = 64

SWA_Q_HEADS = 8
SWA_KV_HEADS = 2
SWA_BLOCK = 128

RET_HEADS = 4
RET_DK = 64
RET_DV = 64
RET_CHUNK = 128

D_FF = 2816
N_EXPERTS = 8
D_FF_EXPERT = 3584

LANES = 128
VMEM_LIMIT = 56 * 1024 * 1024

C_SQ, C_SK, C_SV = 0, 512, 768
C_RQ, C_RK, C_RV, C_RG = 1024, 1280, 1536, 1792
C_CKV, C_EXT = 2048, 2176
IN_COLS = 2432


def _cparams(sem, vmem=VMEM_LIMIT):
    return pltpu.CompilerParams(dimension_semantics=sem, vmem_limit_bytes=vmem)


def _dot(a, b):
    return jnp.dot(a, b, preferred_element_type=F32)


def _dot_nt(a, b):
    return lax.dot_general(a, b, (((1,), (1,)), ((), ())), preferred_element_type=F32)


def _dot_tn(a, b):
    return lax.dot_general(a, b, (((0,), (0,)), ((), ())), preferred_element_type=F32)


def _lane_iota(shape):
    return lax.broadcasted_iota(jnp.int32, shape, len(shape) - 1)


def _mod_kernel(c_ref, w_ref, b_ref, o_ref):
    c = c_ref[...]
    c = c * jax.nn.sigmoid(c)
    o_ref[0] = jnp.dot(c, w_ref[0], preferred_element_type=F32,
                       precision=lax.Precision.HIGHEST) + b_ref[0]


def _modulation(cond, w_mod, b_mod):
    depth, d, n = w_mod.shape
    rows = cond.shape[0]
    tn = 1024
    return pl.pallas_call(
        _mod_kernel,
        out_shape=jax.ShapeDtypeStruct((depth, rows, n), F32),
        grid=(depth, n // tn),
        in_specs=[pl.BlockSpec((rows, d), lambda l, j: (0, 0)),
                  pl.BlockSpec((1, d, tn), lambda l, j: (l, 0, j)),
                  pl.BlockSpec((1, 1, tn), lambda l, j: (l, 0, j))],
        out_specs=pl.BlockSpec((1, rows, tn), lambda l, j: (l, 0, j)),
        compiler_params=_cparams(("arbitrary", "arbitrary")),
        name="modulation",
    )(cond, w_mod, b_mod.reshape(depth, 1, n))


def _angles(pos, dim):
    inv = (ROPE_BASE ** (-np.arange(0, dim, 2, dtype=np.float32) / dim)).astype(np.float32)
    ang = pos.astype(np.float32)[:, None] * inv[None, :]
    return np.concatenate([ang, ang], axis=-1).astype(np.float64)


def _rope_tables(length):
    t = np.arange(length)
    rows, cols = t // GRID_W, t % GRID_W
    ar, ac = _angles(rows, 32), _angles(cols, 32)
    sign32 = np.concatenate([-np.ones(16), np.ones(16)])
    cos_a = np.concatenate([np.cos(ar), np.cos(ac)], axis=-1)
    sin_a = np.concatenate([np.sin(ar) * sign32, np.sin(ac) * sign32], axis=-1)
    cos_a, sin_a = np.tile(cos_a, (1, 2)), np.tile(sin_a, (1, 2))
    at = _angles(t, 64)
    sign64 = np.concatenate([-np.ones(32), np.ones(32)])
    cos_r, sin_r = np.tile(np.cos(at), (1, 2)), np.tile(np.sin(at) * sign64, (1, 2))
    mr, mc = _angles(rows, 16), _angles(cols, 16)
    cos_m = np.ones((length, LANES))
    sin_m = np.zeros((length, LANES))
    cos_m[:, 64:96] = np.concatenate([np.cos(mr), np.cos(mc)], axis=-1)
    sin_m[:, 64:96] = np.concatenate([np.sin(mr), np.sin(mc)], axis=-1)
    return tuple(jnp.asarray(a, F32) for a in (cos_a, sin_a, cos_r, sin_r, cos_m, sin_m))


def _prep_in_weights(w_in, q_norm, w_uq, kv_norm, w_ukv):
    cuts = np.cumsum([MLA_Q_RANK, MLA_KV_RANK, MLA_ROPE, 512, 128, 128, 256, 256, 256, 256])[:-1]
    cq, ckv, kpe, sq, sk, sv, rq, rk, rv, rg = jnp.split(w_in, [int(v) for v in cuts], axis=1)
    dup = lambda w: jnp.concatenate([w[:, :64], w[:, :64], w[:, 64:], w[:, 64:]], axis=1)
    d = w_in.shape[0]
    w_main = jnp.concatenate(
        [sq * HEAD_DIM ** -0.5, dup(sk), dup(sv), rq, rk * RET_DK ** -0.5, rv, rg, ckv,
         cq, kpe, jnp.zeros((d, 32), F32)], axis=1).astype(BF16)

    scale = (MLA_NOPE + MLA_ROPE) ** -0.5
    wq = (w_uq * scale).reshape(MLA_Q_RANK, MLA_HEADS, MLA_NOPE + MLA_ROPE)
    wq = jnp.pad(wq, ((0, 64), (0, 0), (0, 32))).reshape(256, 512)
    place = np.zeros((256, 512), np.float32)
    for h in range(MLA_HEADS):
        for dd in range(MLA_ROPE):
            place[MLA_Q_RANK + dd, h * LANES + MLA_NOPE + dd] = 1.0
    wz = jnp.concatenate([wq, jnp.asarray(place)], axis=1)
    perm = np.zeros((1024, 1024), np.float32)
    for g in range(8):
        for dd in range(MLA_ROPE):
            e = dd % 16
            src = dd + 8 if e < 8 else dd - 8
            perm[g * LANES + MLA_NOPE + src, g * LANES + MLA_NOPE + dd] = -1.0 if e < 8 else 1.0
    wz_rot = wz @ jnp.asarray(perm)
    qn_ext = jnp.pad(q_norm, (0, 64)).reshape(1, 256)

    wkv = w_ukv.reshape(MLA_KV_RANK, MLA_HEADS, MLA_NOPE + MLA_V)
    kn = jnp.pad(wkv[:, :, :MLA_NOPE], ((0, 0), (0, 0), (0, 64))).reshape(MLA_KV_RANK, 512)
    vv = jnp.concatenate([wkv[:, :, MLA_NOPE:], wkv[:, :, MLA_NOPE:]], axis=2).reshape(MLA_KV_RANK, 512)
    w_kv = jnp.concatenate([kn, vv], axis=1)
    return (w_main, wz.astype(BF16), wz_rot.astype(BF16), qn_ext, w_kv.astype(BF16),
            kv_norm.reshape(1, MLA_KV_RANK))


def _rope_roll(x, cos, sin_signed, half):
    lane = _lane_iota(x.shape)
    rot = jnp.where((lane % (2 * half)) < half,
                    pltpu.roll(x, LANES - half, 1), pltpu.roll(x, half, 1))
    return x * cos + rot * sin_signed


def _in_proj_kernel(*refs, rope):
    if rope:
        (x_ref, a_ref, sh_ref, w_ref, wz_ref, wzr_ref, qn_ref, wkv_ref, kvn_ref,
         ca_ref, sa_ref, cr_ref, sr_ref, cm_ref, sm_ref, *outs) = refs
    else:
        (x_ref, a_ref, sh_ref, w_ref, wz_ref, wzr_ref, qn_ref, wkv_ref, kvn_ref, *outs) = refs
    qm_ref, km_ref, vm_ref, sq_ref, sk_ref, sv_ref, rq_ref, rk_ref, rv_ref, rg_ref = outs

    x = x_ref[0]
    h = x * lax.rsqrt(jnp.mean(x * x, axis=-1, keepdims=True) + NORM_EPS) * a_ref[0] + sh_ref[0]
    p = _dot(h.astype(BF16), w_ref[...])

    def put(ref, col, width, tables=None, half=None):
        for g in range(width // LANES):
            blk = p[:, col + g * LANES: col + (g + 1) * LANES]
            if tables is not None:
                blk = _rope_roll(blk, tables[0][...], tables[1][...], half)
            ref[0, :, g * LANES:(g + 1) * LANES] = blk.astype(ref.dtype)

    axial = (ca_ref, sa_ref) if rope else None
    flat = (cr_ref, sr_ref) if rope else None
    put(sq_ref, C_SQ, 512, axial, 16)
    put(sk_ref, C_SK, 256, axial, 16)
    put(sv_ref, C_SV, 256)
    put(rq_ref, C_RQ, 256, flat, 32)
    put(rk_ref, C_RK, 256, flat, 32)
    put(rv_ref, C_RV, 256)
    put(rg_ref, C_RG, 256)

    ext = p[:, C_EXT:C_EXT + 256]
    lane = _lane_iota(ext.shape)
    is_cq = lane < MLA_Q_RANK
    cq_sq = jnp.where(is_cq, ext * ext, 0.0)
    inv = lax.rsqrt(jnp.sum(cq_sq, axis=-1, keepdims=True) * (1.0 / MLA_Q_RANK) + NORM_EPS)
    z = jnp.where(is_cq, ext * inv * qn_ref[...], ext).astype(BF16)
    zw = _dot(z, wz_ref[...])
    ckv = p[:, C_CKV:C_CKV + MLA_KV_RANK]
    ckv = ckv * lax.rsqrt(jnp.mean(ckv * ckv, axis=-1, keepdims=True) + NORM_EPS) * kvn_ref[...]
    kv = _dot(ckv.astype(BF16), wkv_ref[...])
    if rope:
        zr = _dot(z, wzr_ref[...])
    for g in range(MLA_HEADS):
        sl = slice(g * LANES, (g + 1) * LANES)
        sk_ = slice(512 + g * LANES, 512 + (g + 1) * LANES)
        q_g, kpe_g = zw[:, sl], zw[:, sk_]
        if rope:
            q_g = q_g * cm_ref[...] + zr[:, sl] * sm_ref[...]
            kpe_g = kpe_g * cm_ref[...] + zr[:, sk_] * sm_ref[...]
        qm_ref[0, :, sl] = q_g.astype(BF16)
        km_ref[0, :, sl] = (kv[:, sl] + kpe_g).astype(BF16)
        vm_ref[0, :, sl] = kv[:, sk_].astype(BF16)


def _in_proj(x, a, sh, wts, tables, tm):
    B, L, D = x.shape
    w_main, wz, wzr, qn_ext, w_kv, kvn = wts
    rope = tables is not None
    tm = min(tm, L)
    bm = (lambda b: b) if a.shape[0] == B else (lambda b: 0)
    const = lambda i, b: (0, 0)
    in_specs = [pl.BlockSpec((1, tm, D), lambda i, b: (b, i, 0)),
                pl.BlockSpec((1, 1, D), lambda i, b: (bm(b), 0, 0)),
                pl.BlockSpec((1, 1, D), lambda i, b: (bm(b), 0, 0)),
                pl.BlockSpec(w_main.shape, const), pl.BlockSpec(wz.shape, const),
                pl.BlockSpec(wzr.shape, const), pl.BlockSpec(qn_ext.shape, const),
                pl.BlockSpec(w_kv.shape, const), pl.BlockSpec(kvn.shape, const)]
    args = [x, a, sh, w_main, wz, wzr, qn_ext, w_kv, kvn]
    if rope:
        in_specs += [pl.BlockSpec((tm, LANES), lambda i, b: (i, 0))] * 6
        args += list(tables)
    widths = (512, 512, 512, 512, 256, 256, 256, 256, 256, 256)
    return pl.pallas_call(
        functools.partial(_in_proj_kernel, rope=rope),
        out_shape=[jax.ShapeDtypeStruct((B, L, w), BF16) for w in widths],
        grid=(L // tm, B),
        in_specs=in_specs,
        out_specs=[pl.BlockSpec((1, tm, w), lambda i, b: (b, i, 0)) for w in widths],
        compiler_params=_cparams(("arbitrary", "arbitrary")),
        name="in_proj_rope" if rope else "in_proj_ctx",
    )(*args)


def _mla_kernel(q_ref, k_ref, v_ref, o_ref):
    lane = _lane_iota((q_ref.shape[1], LANES))
    outs = []
    for h in range(MLA_HEADS):
        sl = slice(h * LANES, (h + 1) * LANES)
        s = _dot_nt(q_ref[0, :, sl], k_ref[0, :, sl])
        m = jnp.max(s, axis=-1, keepdims=True)
        p = jnp.exp(s - m)
        l = jnp.sum(p, axis=-1, keepdims=True)
        o = _dot(p.astype(BF16), v_ref[0, :, sl])
        outs.append(o * (1.0 / l))
    for g in range(MLA_HEADS // 2):
        o_ref[0, :, g * LANES:(g + 1) * LANES] = jnp.where(
            lane < 64, outs[2 * g], outs[2 * g + 1]).astype(o_ref.dtype)


def _mla_attention(qm, km, vm, tq):
    B, L, _ = qm.shape
    Lk = km.shape[1]
    tq = min(tq, L)
    return pl.pallas_call(
        _mla_kernel,
        out_shape=jax.ShapeDtypeStruct((B, L, MLA_HEADS * MLA_V), BF16),
        grid=(B, L // tq),
        in_specs=[pl.BlockSpec((1, tq, 512), lambda b, i: (b, i, 0)),
                  pl.BlockSpec((1, Lk, 512), lambda b, i: (b, 0, 0)),
                  pl.BlockSpec((1, Lk, 512), lambda b, i: (b, 0, 0))],
        out_specs=pl.BlockSpec((1, tq, MLA_HEADS * MLA_V), lambda b, i: (b, i, 0)),
        compiler_params=_cparams(("arbitrary", "arbitrary")),
        name="mla_attention",
    )(qm, km, vm)


def _swa_kernel(sink_ref, q_ref, k_ref, v_ref, kc_ref, vc_ref, o_ref, *, banded):
    W = SWA_BLOCK
    i = pl.program_id(1)
    nb = pl.num_programs(1)
    if banded:
        prev = pl.multiple_of(jnp.maximum(i - 1, 0) * W, W)
        cur = pl.multiple_of(i * W, W)
        nxt = pl.multiple_of(jnp.minimum(i + 1, nb - 1) * W, W)
        k_all = jnp.concatenate([k_ref[0, pl.ds(prev, W), :], k_ref[0, pl.ds(cur, W), :],
                                 k_ref[0, pl.ds(nxt, W), :], kc_ref[0]], axis=0)
        v_all = jnp.concatenate([v_ref[0, pl.ds(prev, W), :], v_ref[0, pl.ds(cur, W), :],
                                 v_ref[0, pl.ds(nxt, W), :], vc_ref[0]], axis=0)
        nk = k_all.shape[0]
        qq = lax.broadcasted_iota(jnp.int32, (W, nk), 0)
        kk = lax.broadcasted_iota(jnp.int32, (W, nk), 1)
        key_t = i * W + kk - W
        in_band = (jnp.abs(kk - W - qq) <= W) & (key_t >= 0) & (key_t < nb * W)
        valid = (kk >= 3 * W) | in_band
    else:
        k_all, v_all = kc_ref[0], vc_ref[0]
        valid = None
    lane = _lane_iota((q_ref.shape[1], LANES))
    for j in range(SWA_Q_HEADS // 2):
        g = (2 * j) // (SWA_Q_HEADS // SWA_KV_HEADS)
        gs = slice(g * LANES, (g + 1) * LANES)
        q_pair = q_ref[0, :, j * LANES:(j + 1) * LANES]
        k_g, v_g = k_all[:, gs], v_all[:, gs]
        outs = []
        for half in range(2):
            zq = jnp.zeros_like(q_pair)
            q_h = jnp.where(lane < 64, q_pair, zq) if half == 0 else jnp.where(lane < 64, zq, q_pair)
            s = _dot_nt(q_h, k_g)
            if valid is not None:
                s = jnp.where(valid, s, NEG_INF)
            sink = sink_ref[2 * j + half]
            m = jnp.maximum(jnp.max(s, axis=-1, keepdims=True), sink)
            p = jnp.exp(s - m)
            l = jnp.sum(p, axis=-1, keepdims=True) + jnp.exp(sink - m)
            outs.append(_dot(p.astype(BF16), v_g) * (1.0 / l))
        o_ref[0, :, j * LANES:(j + 1) * LANES] = jnp.where(lane < 64, outs[0], outs[1]).astype(o_ref.dtype)


def _swa_attention(sink, q, k, v, kc, vc, banded):
    B, L, _ = q.shape
    Lc = kc.shape[1]
    tq = SWA_BLOCK if banded else L
    Lkv = k.shape[1]
    return pl.pallas_call(
        functools.partial(_swa_kernel, banded=banded),
        out_shape=jax.ShapeDtypeStruct((B, L, 512), BF16),
        grid=(B, L // tq),
        in_specs=[pl.BlockSpec(memory_space=pltpu.SMEM),
                  pl.BlockSpec((1, tq, 512), lambda b, i: (b, i, 0)),
                  pl.BlockSpec((1, Lkv, 256), lambda b, i: (b, 0, 0)),
                  pl.BlockSpec((1, Lkv, 256), lambda b, i: (b, 0, 0)),
                  pl.BlockSpec((1, Lc, 256), lambda b, i: (b, 0, 0)),
                  pl.BlockSpec((1, Lc, 256), lambda b, i: (b, 0, 0))],
        out_specs=pl.BlockSpec((1, tq, 512), lambda b, i: (b, i, 0)),
        compiler_params=_cparams(("arbitrary", "arbitrary")),
        name="swa_banded" if banded else "swa_context",
    )(sink, q, k, v, kc, vc)


def _ret_tables(decay_f, decay_b):
    C = RET_CHUNK
    lg_f = jnp.log(jax.nn.sigmoid(decay_f.astype(F32)))
    lg_b = jnp.log(jax.nn.sigmoid(decay_b.astype(F32)))
    idx = jnp.arange(C, dtype=F32)
    diff = idx[:, None] - idx[None, :]
    intra = (jnp.where(diff >= 0, jnp.exp(lg_f[:, None, None] * jnp.maximum(diff, 0.0)), 0.0)
             + jnp.where(diff <= 0, jnp.exp(lg_b[:, None, None] * jnp.maximum(-diff, 0.0)), 0.0))
    lanes = lambda t: jnp.repeat(t.T, RET_DK, axis=1)
    qdf = lanes(jnp.exp(lg_f[:, None] * (idx + 1.0)))
    qdb = lanes(jnp.exp(lg_b[:, None] * (C - idx)))
    kdf = lanes(jnp.exp(lg_f[:, None] * (C - 1.0 - idx)))
    kdb = lanes(jnp.exp(lg_b[:, None] * idx))
    cdf = jnp.repeat(jnp.exp(lg_f * C), RET_DV).reshape(1, -1)
    cdb = jnp.repeat(jnp.exp(lg_b * C), RET_DV).reshape(1, -1)
    return intra, qdf, qdb, kdf, kdb, cdf, cdb


def _ret_kernel(q_ref, k_ref, v_ref, g_ref, qx_ref, kx_ref, vx_ref, gx_ref,
                d_ref, qdf_ref, qdb_ref, kdf_ref, kdb_ref, cdf_ref, cdb_ref,
                *rest, ctx_out):
    if ctx_out:
        o_ref, ox_ref, sf, sb, sfx, sbx = rest
    else:
        o_ref, sf, sb, sfx, sbx = rest
        ox_ref = None
    C = RET_CHUNK
    nc = q_ref.shape[1] // C
    ncx = qx_ref.shape[1] // C
    NG = RET_HEADS // 2
    r = lax.broadcasted_iota(jnp.int32, (LANES, LANES), 0)
    cidx = lax.broadcasted_iota(jnp.int32, (LANES, LANES), 1)
    blockdiag = (r // 64) == (cidx // 64)
    lane = _lane_iota((C, LANES))
    lo = lane < 64

    def kv_sum(kr, vr, c0, kd_ref, j):
        gs = slice(j * LANES, (j + 1) * LANES)
        kd = (kr[0, pl.ds(c0, C), gs].astype(F32) * kd_ref[:, gs]).astype(BF16)
        return jnp.where(blockdiag, _dot_tn(kd, vr[0, pl.ds(c0, C), gs]), 0.0)

    def state_pass(kr, vr, st, n, kd_ref, cd_ref, init, reverse):
        def body(t, carry):
            c = (n - 1 - t) if reverse else t
            c0 = pl.multiple_of(c * C, C)
            new = []
            for j in range(NG):
                st[c, j] = carry[j]
                new.append(carry[j] * cd_ref[:, j * LANES:(j + 1) * LANES] + kv_sum(kr, vr, c0, kd_ref, j))
            return tuple(new)
        return lax.fori_loop(0, n, body, init)

    zero = tuple(jnp.zeros((LANES, LANES), F32) for _ in range(NG))
    s_f = state_pass(kx_ref, vx_ref, sfx, ncx, kdf_ref, cdf_ref, zero, False)
    state_pass(k_ref, v_ref, sf, nc, kdf_ref, cdf_ref, s_f, False)
    s_b = state_pass(kx_ref, vx_ref, sbx, ncx, kdb_ref, cdb_ref, zero, True)
    state_pass(k_ref, v_ref, sb, nc, kdb_ref, cdb_ref, s_b, True)

    def out_pass(qr, kr, vr, gr, orf, stf, stb, n):
        def body(c, _):
            c0 = pl.multiple_of(c * C, C)
            for j in range(NG):
                gs = slice(j * LANES, (j + 1) * LANES)
                qg, kg, vg = qr[0, pl.ds(c0, C), gs], kr[0, pl.ds(c0, C), gs], vr[0, pl.ds(c0, C), gs]
                halves = []
                for half in range(2):
                    zq = jnp.zeros_like(qg)
                    qh = jnp.where(lo, qg, zq) if half == 0 else jnp.where(lo, zq, qg)
                    att = _dot_nt(qh, kg) * d_ref[2 * j + half]
                    halves.append(_dot(att.astype(BF16), vg))
                o = jnp.where(lo, halves[0], halves[1])
                qf = qg.astype(F32)
                qd = jnp.concatenate([(qf * qdf_ref[:, gs]).astype(BF16),
                                      (qf * qdb_ref[:, gs]).astype(BF16)], axis=1)
                s_cat = jnp.concatenate([stf[c, j], stb[c, j]], axis=0).astype(BF16)
                o = o + _dot(qd, s_cat)
                o2 = o * o
                ms = jnp.where(lo, jnp.sum(jnp.where(lo, o2, 0.0), axis=-1, keepdims=True),
                               jnp.sum(jnp.where(lo, 0.0, o2), axis=-1, keepdims=True)) * (1.0 / RET_DV)
                gate = gr[0, pl.ds(c0, C), gs].astype(F32)
                y = o * lax.rsqrt(ms + NORM_EPS) * (gate * jax.nn.sigmoid(gate))
                orf[0, pl.ds(c0, C), gs] = y.astype(orf.dtype)
            return 0
        lax.fori_loop(0, n, body, 0)

    out_pass(q_ref, k_ref, v_ref, g_ref, o_ref, sf, sb, nc)
    if ctx_out:
        out_pass(qx_ref, kx_ref, vx_ref, gx_ref, ox_ref, sfx, sbx, ncx)


def _retention(q, k, v, g, qx, kx, vx, gx, tabs, ctx_out):
    B, L, W = q.shape
    Lc = qx.shape[1]
    nc, ncx = L // RET_CHUNK, Lc // RET_CHUNK
    lat = pl.BlockSpec((1, L, W), lambda b: (b, 0, 0))
    cx = pl.BlockSpec((1, Lc, W), lambda b: (b, 0, 0))
    full = lambda a: pl.BlockSpec(a.shape, lambda b: (0,) * a.ndim)
    out_shape = [jax.ShapeDtypeStruct((B, L, W), BF16)]
    out_specs = [lat]
    if ctx_out:
        out_shape.append(jax.ShapeDtypeStruct((B, Lc, W), BF16))
        out_specs.append(cx)
    res = pl.pallas_call(
        functools.partial(_ret_kernel, ctx_out=ctx_out),
        out_shape=out_shape,
        grid=(B,),
        in_specs=[lat, lat, lat, lat, cx, cx, cx, cx] + [full(t) for t in tabs],
        out_specs=out_specs,
        scratch_shapes=[pltpu.VMEM((nc, 2, LANES, LANES), F32), pltpu.VMEM((nc, 2, LANES, LANES), F32),
                        pltpu.VMEM((ncx, 2, LANES, LANES), F32), pltpu.VMEM((ncx, 2, LANES, LANES), F32)],
        compiler_params=_cparams(("arbitrary",)),
        name="retention",
    )(q, k, v, g, qx, kx, vx, gx, *tabs)
    return (res[0], res[1]) if ctx_out else (res[0], None)


def _split_hi_lo(a):
    hi = a.astype(BF16)
    return hi, (a - hi.astype(F32)).astype(BF16)


def _out_proj_kernel(oa_ref, ob_ref, oc_ref, x_ref, g1_ref, a2_ref, sh2_ref, wo_ref, *rest, route):
    if route:
        r_ref, x1_ref, h2_ref, rt_ref = rest
    else:
        x1_ref, h2_ref = rest
    y = (_dot(oa_ref[0], wo_ref[0:256, :]) + _dot(ob_ref[0], wo_ref[256:768, :])
         + _dot(oc_ref[0], wo_ref[768:1024, :]))
    x1 = x_ref[0] + g1_ref[0] * y
    x1_ref[0] = x1
    h2 = x1 * lax.rsqrt(jnp.mean(x1 * x1, axis=-1, keepdims=True) + NORM_EPS) * a2_ref[0] + sh2_ref[0]
    h2_ref[0] = h2.astype(h2_ref.dtype)
    if route:
        h_hi, h_lo = _split_hi_lo(h2)
        r_hi, r_lo = _split_hi_lo(r_ref[...])
        logits = _dot(h_hi, r_hi) + _dot(h_hi, r_lo) + _dot(h_lo, r_hi)
        lane = _lane_iota(logits.shape)
        logits = jnp.where(lane < N_EXPERTS, logits, -jnp.inf)
        v1 = jnp.max(logits, axis=-1, keepdims=True)
        i1 = jnp.min(jnp.where(logits == v1, lane, LANES), axis=-1, keepdims=True)
        rest_l = jnp.where(lane == i1, -jnp.inf, logits)
        v2 = jnp.max(rest_l, axis=-1, keepdims=True)
        i2 = jnp.min(jnp.where(rest_l == v2, lane, LANES), axis=-1, keepdims=True)
        e2 = jnp.exp(v2 - v1)
        w1 = 1.0 / (1.0 + e2)
        w2 = e2 * w1
        rt_ref[0] = jnp.where(lane == 0, i1.astype(F32),
                              jnp.where(lane == 1, i2.astype(F32),
                                        jnp.where(lane == 2, w1, jnp.where(lane == 3, w2, 0.0))))


def _out_proj(oa, ob, oc, x, g1, a2, sh2, wo, router, tm, h2_dtype):
    B, L, D = x.shape
    tm = min(tm, L)
    route = router is not None
    bm = (lambda b: b) if g1.shape[0] == B else (lambda b: 0)
    tok = lambda w: pl.BlockSpec((1, tm, w), lambda b, i: (b, i, 0))
    vec = pl.BlockSpec((1, 1, D), lambda b, i: (bm(b), 0, 0))
    in_specs = [tok(256), tok(512), tok(256), tok(D), vec, vec, vec,
                pl.BlockSpec(wo.shape, lambda b, i: (0, 0))]
    args = [oa, ob, oc, x, g1, a2, sh2, wo]
    out_shape = [jax.ShapeDtypeStruct((B, L, D), F32), jax.ShapeDtypeStruct((B, L, D), h2_dtype)]
    out_specs = [tok(D), tok(D)]
    if route:
        in_specs.append(pl.BlockSpec(router.shape, lambda b, i: (0, 0)))
        args.append(router)
        out_shape.append(jax.ShapeDtypeStruct((B, L, LANES), F32))
        out_specs.append(tok(LANES))
    return pl.pallas_call(
        functools.partial(_out_proj_kernel, route=route),
        out_shape=out_shape,
        grid=(B, L // tm),
        in_specs=in_specs,
        out_specs=out_specs,
        compiler_params=_cparams(("arbitrary", "arbitrary")),
        name="out_proj_route" if route else "out_proj",
    )(*args)


def _ffn_kernel(h_ref, x_ref, g2_ref, wg_ref, wu_ref, wd_ref, o_ref, *, n_chunks):
    h = h_ref[0]
    tf = wg_ref.shape[1] // n_chunks
    acc = None
    for f in range(n_chunks):
        fs = slice(f * tf, (f + 1) * tf)
        g = _dot(h, wg_ref[:, fs])
        u = _dot(h, wu_ref[:, fs])
        a = (g * jax.nn.sigmoid(g) * u).astype(BF16)
        part = _dot(a, wd_ref[fs, :])
        acc = part if acc is None else acc + part
    o_ref[0] = x_ref[0] + g2_ref[0] * acc


def _ffn(h2, x1, g2, wg, wu, wd, tm):
    B, L, D = x1.shape
    tm = min(tm, L)
    bm = (lambda b: b) if g2.shape[0] == B else (lambda b: 0)
    tok = pl.BlockSpec((1, tm, D), lambda b, i: (b, i, 0))
    wspec = lambda w: pl.BlockSpec(w.shape, lambda b, i: (0, 0), pipeline_mode=pl.Buffered(1))
    return pl.pallas_call(
        functools.partial(_ffn_kernel, n_chunks=2),
        out_shape=jax.ShapeDtypeStruct((B, L, D), F32),
        grid=(B, L // tm),
        in_specs=[tok, tok, pl.BlockSpec((1, 1, D), lambda b, i: (bm(b), 0, 0)),
                  wspec(wg), wspec(wu), wspec(wd)],
        out_specs=tok,
        compiler_params=_cparams(("arbitrary", "arbitrary")),
        name="dense_ffn",
    )(h2, x1, g2, wg, wu, wd)


def _gather_kernel(idx_ref, src_ref, o_ref, buf, sem):
    rows = o_ref.shape[0]

    def row_copy(r):
        return pltpu.make_async_copy(src_ref.at[pl.ds(idx_ref[0, 0, r], 1)], buf.at[pl.ds(r, 1)], sem)

    def start(r, _):
        row_copy(r).start()
        return 0

    def wait(r, _):
        row_copy(r).wait()
        return 0

    lax.fori_loop(0, rows, start, 0)
    lax.fori_loop(0, rows, wait, 0)
    o_ref[...] = buf[...].astype(o_ref.dtype)


def _gather_rows(src, idx, tg):
    P = idx.shape[0]
    D = src.shape[1]
    return pl.pallas_call(
        _gather_kernel,
        out_shape=jax.ShapeDtypeStruct((P, D), BF16),
        grid=(P // tg,),
        in_specs=[pl.BlockSpec((1, 1, tg), lambda i: (i, 0, 0), memory_space=pltpu.SMEM),
                  pl.BlockSpec(memory_space=pl.ANY)],
        out_specs=pl.BlockSpec((tg, D), lambda i: (i, 0)),
        scratch_shapes=[pltpu.VMEM((tg, D), F32), pltpu.SemaphoreType.DMA],
        compiler_params=_cparams(("arbitrary",)),
        name="moe_gather",
    )(idx.reshape(P // tg, 1, tg), src)


def _gmm_kernel(te_ref, nu_ref, x_ref, wg_ref, wu_ref, wd_ref, o_ref, acc_ref):
    i, f = pl.program_id(0), pl.program_id(1)

    @pl.when(i < nu_ref[0])
    def _():
        x = x_ref[...]
        g = _dot(x, wg_ref[0])
        u = _dot(x, wu_ref[0])
        a = (g * jax.nn.sigmoid(g) * u).astype(BF16)
        part = _dot(a, wd_ref[0])

        @pl.when(f == 0)
        def _():
            acc_ref[...] = part

        @pl.when(f > 0)
        def _():
            acc_ref[...] += part

        @pl.when(f == pl.num_programs(1) - 1)
        def _():
            o_ref[...] = acc_ref[...]

    @pl.when((i >= nu_ref[0]) & (f == pl.num_programs(1) - 1))
    def _():
        o_ref[...] = jnp.zeros_like(o_ref)


def _grouped_swiglu(xs, tile_expert, n_used, wg, wu, wd, tm, tf):
    P, D = xs.shape
    E, _, F = wg.shape
    nf = F // tf
    n_tiles = P // tm

    def fsel(i, f, nu):
        return jnp.where(i < nu[0], f, nf - 1)

    return pl.pallas_call(
        _gmm_kernel,
        out_shape=jax.ShapeDtypeStruct((P, D), F32),
        grid_spec=pltpu.PrefetchScalarGridSpec(
            num_scalar_prefetch=2,
            grid=(n_tiles, nf),
            in_specs=[pl.BlockSpec((tm, D), lambda i, f, te, nu: (i, 0)),
                      pl.BlockSpec((1, D, tf), lambda i, f, te, nu: (te[i], 0, fsel(i, f, nu))),
                      pl.BlockSpec((1, D, tf), lambda i, f, te, nu: (te[i], 0, fsel(i, f, nu))),
                      pl.BlockSpec((1, tf, D), lambda i, f, te, nu: (te[i], fsel(i, f, nu), 0))],
            out_specs=pl.BlockSpec((tm, D), lambda i, f, te, nu: (i, 0)),
            scratch_shapes=[pltpu.VMEM((tm, D), F32)]),
        compiler_params=_cparams(("arbitrary", "arbitrary")),
        name="moe_grouped_swiglu",
    )(tile_expert, n_used, xs, wg, wu, wd)


def _combine_kernel(d0_ref, d1_ref, y_ref, x_ref, g2_ref, rt_ref, fg_ref, o_ref, buf, sem):
    rows = o_ref.shape[1]

    def row_copy(k, r):
        idx_ref = d0_ref if k == 0 else d1_ref
        return pltpu.make_async_copy(y_ref.at[pl.ds(idx_ref[0, 0, r], 1)], buf.at[k, pl.ds(r, 1)], sem)

    def start(r, _):
        row_copy(0, r).start()
        row_copy(1, r).start()
        return 0

    def wait(r, _):
        row_copy(0, r).wait()
        row_copy(1, r).wait()
        return 0

    lax.fori_loop(0, rows, start, 0)
    lax.fori_loop(0, rows, wait, 0)
    rt = rt_ref[0]
    moe = rt[:, 2:3] * buf[0] + rt[:, 3:4] * buf[1]
    x2 = x_ref[0] + g2_ref[0] * moe
    o_ref[0] = x2 * lax.rsqrt(jnp.mean(x2 * x2, axis=-1, keepdims=True) + NORM_EPS) * fg_ref[...]


def _combine(y, d0, d1, x1, g2, rt, final_g, tc):
    B, L, D = x1.shape
    tc = min(tc, L)
    nt = L // tc
    idx_spec = pl.BlockSpec((1, 1, tc), lambda b, i: (b * nt + i, 0, 0), memory_space=pltpu.SMEM)
    tok = lambda w: pl.BlockSpec((1, tc, w), lambda b, i: (b, i, 0))
    return pl.pallas_call(
        _combine_kernel,
        out_shape=jax.ShapeDtypeStruct((B, L, D), F32),
        grid=(B, nt),
        in_specs=[idx_spec, idx_spec, pl.BlockSpec(memory_space=pl.ANY), tok(D),
                  pl.BlockSpec((1, 1, D), lambda b, i: (b, 0, 0)), tok(LANES),
                  pl.BlockSpec((1, D), lambda b, i: (0, 0))],
        out_specs=tok(D),
        scratch_shapes=[pltpu.VMEM((2, tc, D), F32), pltpu.SemaphoreType.DMA],
        compiler_params=_cparams(("arbitrary", "arbitrary")),
        name="moe_combine_norm",
    )(d0.reshape(B * nt, 1, tc), d1.reshape(B * nt, 1, tc), y, x1, g2, rt, final_g.reshape(1, D))


def _routing(rt, tm):
    n = rt.shape[0]
    e = rt[:, :2].astype(jnp.int32).reshape(-1)
    onehot = (e[:, None] == jnp.arange(N_EXPERTS)[None, :]).astype(jnp.int32)
    csum = jnp.cumsum(onehot, axis=0)
    rank = jnp.sum((csum - onehot) * onehot, axis=1)
    counts = csum[-1]
    padded = ((counts + tm - 1) // tm) * tm
    ends = jnp.cumsum(padded)
    offs = ends - padded
    dest = offs[e] + rank
    P = 2 * n + N_EXPERTS * tm
    slot_tok = jnp.zeros((P,), jnp.int32).at[dest].set(jnp.arange(2 * n, dtype=jnp.int32) // 2)
    tile_start = jnp.arange(P // tm, dtype=jnp.int32) * tm
    n_used = (ends[-1] // tm).astype(jnp.int32)
    te = jnp.sum((tile_start[:, None] >= ends[None, :]).astype(jnp.int32), axis=1)
    last = jnp.minimum(te[jnp.maximum(n_used - 1, 0)], N_EXPERTS - 1)
    te = jnp.where(tile_start < ends[-1], jnp.minimum(te, N_EXPERTS - 1), last).astype(jnp.int32)
    d = dest.reshape(n, 2)
    return slot_tok, te, n_used.reshape(1), d[:, 0], d[:, 1]


TM_IN = 256
TQ_MLA = 256
TM_OUT = 512
TM_FFN = 512
TM_MOE = 512
TF_MOE = 896
TG_GATHER = 256
TC_COMBINE = 256


def kernel(x, c, ctx, c_ctx, w_mod, b_mod, norm1_g, norm2_g, w_in, mla_q_norm, mla_w_uq, mla_kv_norm,
           mla_w_ukv, swa_sink, ret_decay_fwd, ret_decay_bwd, w_out, ffn_w_gate, ffn_w_up, ffn_w_down,
           moe_router, moe_w_gate, moe_w_up, moe_w_down, final_norm_g):
    B, L, D = x.shape
    Lc = ctx.shape[1]
    depth = w_mod.shape[0]
    xc = ctx

    cond = jnp.concatenate([c, c_ctx[None], jnp.zeros((16 - B - 1, D), F32)], axis=0)
    mod_all = _modulation(cond, w_mod, b_mod)
    tables = _rope_tables(L)

    for layer in range(depth):
        last = layer == depth - 1
        mod = mod_all[layer].reshape(16, 6, 1, D)
        sh1, sc1, g1, sh2, sc2, g2 = (mod[:B, j] for j in range(6))
        sh1x, sc1x, g1x, sh2x, sc2x, g2x = (mod[B:B + 1, j] for j in range(6))
        n1, n2 = norm1_g[layer], norm2_g[layer]

        wts = _prep_in_weights(w_in[layer], mla_q_norm[layer], mla_w_uq[layer],
                               mla_kv_norm[layer], mla_w_ukv[layer])
        lat = _in_proj(x, n1 * (1.0 + sc1), sh1, wts, tables, TM_IN)
        cx = _in_proj(xc, n1 * (1.0 + sc1x), sh1x, wts, None, TM_IN)
        qm, km, vm, sq, sk, sv, rq, rk, rv, rg = lat
        qmx, kmx, vmx, sqx, skx, svx, rqx, rkx, rvx, rgx = cx

        o_a = _mla_attention(qm, jnp.concatenate([km, kmx], axis=1),
                             jnp.concatenate([vm, vmx], axis=1), TQ_MLA)
        sink = swa_sink[layer].astype(F32)
        o_b = _swa_attention(sink, sq, sk, sv, skx, svx, True)
        rtabs = _ret_tables(ret_decay_fwd[layer], ret_decay_bwd[layer])
        o_c, oc_c = _retention(rq, rk, rv, rg, rqx, rkx, rvx, rgx, rtabs, not last)
        wo = w_out[layer].astype(BF16)

        if layer % 2 == 0:
            i = layer // 2
            wg, wu, wd = (ffn_w_gate[i].astype(BF16), ffn_w_up[i].astype(BF16),
                          ffn_w_down[i].astype(BF16))
            x1, h2 = _out_proj(o_a, o_b, o_c, x, g1, n2 * (1.0 + sc2), sh2, wo, None, TM_OUT, BF16)
            x_next = _ffn(h2, x1, g2, wg, wu, wd, TM_FFN)
        else:
            i = layer // 2
            router = jnp.pad(moe_router[i], ((0, 0), (0, LANES - N_EXPERTS)))
            x1, h2, rt = _out_proj(o_a, o_b, o_c, x, g1, n2 * (1.0 + sc2), sh2, wo, router, TM_OUT, F32)
            slot_tok, te, n_used, d0, d1 = _routing(rt.reshape(B * L, LANES), TM_MOE)
            xs = _gather_rows(h2.reshape(B * L, D), slot_tok, TG_GATHER)
            y = _grouped_swiglu(xs, te, n_used, moe_w_gate[i].astype(BF16), moe_w_up[i].astype(BF16),
                                moe_w_down[i].astype(BF16), TM_MOE, TF_MOE)
            if last:
                return _combine(y, d0, d1, x1, g2, rt, final_norm_g, TC_COMBINE)
            raise NotImplementedError("expert layer is only supported as the last layer")

        if not last:
            oc_a = _mla_attention(qmx, kmx, vmx, TQ_MLA)
            oc_b = _swa_attention(sink, sqx, skx, svx, skx, svx, False)
            xc1, hc2 = _out_proj(oc_a, oc_b, oc_c, xc, g1x, n2 * (1.0 + sc2x), sh2x, wo, None, TM_OUT, BF16)
            xc = _ffn(hc2, xc1, g2x, wg, wu, wd, TM_FFN)
        x = x_next
    raise NotImplementedError("trunk must end with the expert layer")
```

```python
import functools
import math

import numpy as np
import jax
import jax.numpy as jnp
from jax import lax
from jax.experimental import pallas as pl
from jax.experimental.pallas import tpu as pltpu

F32 = jnp.float32
BF16 = jnp.bfloat16

D_MODEL = 1024
DEPTH = 2
GRID_W = 64
HEAD_DIM = 64
NORM_EPS = 1e-6
ROPE_BASE = 10000.0
NEG_INF = -1e30

MLA_HEADS = 4
MLA_Q_RANK = 192
MLA_KV_RANK = 128
MLA_NOPE = 64
MLA_ROPE = 32
MLA_V = 64

SWA_Q_HEADS = 8
SWA_KV_HEADS = 2
SWA_BLOCK = 128

RET_HEADS = 4
RET_DK = 64
RET_DV = 64
RET_CHUNK = 128

D_FF = 2816
N_EXPERTS = 8
D_FF_EXPERT = 3584

LANES = 128
VMEM_LIMIT = 56 * 1024 * 1024

C_SQ, C_SK, C_SV = 0, 512, 768
C_RQ, C_RK, C_RV, C_RG = 1024, 1280, 1536, 1792
C_CKV, C_EXT = 2048, 2176
IN_COLS = 2432


def _cparams(sem, vmem=VMEM_LIMIT):
    return pltpu.CompilerParams(dimension_semantics=sem, vmem_limit_bytes=vmem)


def _dot(a, b):
    return jnp.dot(a, b, preferred_element_type=F32)


def _dot_nt(a, b):
    return lax.dot_general(a, b, (((1,), (1,)), ((), ())), preferred_element_type=F32)


def _dot_tn(a, b):
    return lax.dot_general(a, b, (((0,), (0,)), ((), ())), preferred_element_type=F32)


def _lane_iota(shape):
    return lax.broadcasted_iota(jnp.int32, shape, len(shape) - 1)


def _mod_kernel(c_ref, w_ref, b_ref, o_ref):
    c = c_ref[...]
    c = c * jax.nn.sigmoid(c)
    o_ref[0] = jnp.dot(c, w_ref[0], preferred_element_type=F32,
                       precision=lax.Precision.HIGHEST) + b_ref[0]


def _modulation(cond, w_mod, b_mod):
    depth, d, n = w_mod.shape
    rows = cond.shape[0]
    tn = 1024
    return pl.pallas_call(
        _mod_kernel,
        out_shape=jax.ShapeDtypeStruct((depth, rows, n), F32),
        grid=(depth, n // tn),
        in_specs=[pl.BlockSpec((rows, d), lambda l, j: (0, 0)),
                  pl.BlockSpec((1, d, tn), lambda l, j: (l, 0, j)),
                  pl.BlockSpec((1, 1, tn), lambda l, j: (l, 0, j))],
        out_specs=pl.BlockSpec((1, rows, tn), lambda l, j: (l, 0, j)),
        compiler_params=_cparams(("arbitrary", "arbitrary")),
        name="modulation",
    )(cond, w_mod, b_mod.reshape(depth, 1, n))


def _angles(pos, dim):
    inv = (ROPE_BASE ** (-np.arange(0, dim, 2, dtype=np.float32) / dim)).astype(np.float32)
    ang = pos.astype(np.float32)[:, None] * inv[None, :]
    return np.concatenate([ang, ang], axis=-1).astype(np.float64)


def _rope_tables(length):
    t = np.arange(length)
    rows, cols = t // GRID_W, t % GRID_W
    ar, ac = _angles(rows, 32), _angles(cols, 32)
    sign32 = np.concatenate([-np.ones(16), np.ones(16)])
    cos_a = np.concatenate([np.cos(ar), np.cos(ac)], axis=-1)
    sin_a = np.concatenate([np.sin(ar) * sign32, np.sin(ac) * sign32], axis=-1)
    cos_a, sin_a = np.tile(cos_a, (1, 2)), np.tile(sin_a, (1, 2))
    at = _angles(t, 64)
    sign64 = np.concatenate([-np.ones(32), np.ones(32)])
    cos_r, sin_r = np.tile(np.cos(at), (1, 2)), np.tile(np.sin(at) * sign64, (1, 2))
    mr, mc = _angles(rows, 16), _angles(cols, 16)
    cos_m = np.ones((length, LANES))
    sin_m = np.zeros((length, LANES))
    cos_m[:, 64:96] = np.concatenate([np.cos(mr), np.cos(mc)], axis=-1)
    sin_m[:, 64:96] = np.concatenate([np.sin(mr), np.sin(mc)], axis=-1)
    return tuple(jnp.asarray(a, F32) for a in (cos_a, sin_a, cos_r, sin_r, cos_m, sin_m))


def _prep_in_weights(w_in, q_norm, w_uq, kv_norm, w_ukv):
    cuts = np.cumsum([MLA_Q_RANK, MLA_KV_RANK, MLA_ROPE, 512, 128, 128, 256, 256, 256, 256])[:-1]
    cq, ckv, kpe, sq, sk, sv, rq, rk, rv, rg = jnp.split(w_in, [int(v) for v in cuts], axis=1)
    dup = lambda w: jnp.concatenate([w[:, :64], w[:, :64], w[:, 64:], w[:, 64:]], axis=1)
    d = w_in.shape[0]
    w_main = jnp.concatenate(
        [sq * HEAD_DIM ** -0.5, dup(sk), dup(sv), rq, rk * RET_DK ** -0.5, rv, rg, ckv,
         cq, kpe, jnp.zeros((d, 32), F32)], axis=1).astype(BF16)

    scale = (MLA_NOPE + MLA_ROPE) ** -0.5
    wq = (w_uq * scale).reshape(MLA_Q_RANK, MLA_HEADS, MLA_NOPE + MLA_ROPE)
    wq = jnp.pad(wq, ((0, 64), (0, 0), (0, 32))).reshape(256, 512)
    place = np.zeros((256, 512), np.float32)
    for h in range(MLA_HEADS):
        for dd in range(MLA_ROPE):
            place[MLA_Q_RANK + dd, h * LANES + MLA_NOPE + dd] = 1.0
    wz = jnp.concatenate([wq, jnp.asarray(place)], axis=1)
    perm = np.zeros((1024, 1024), np.float32)
    for g in range(8):
        for dd in range(MLA_ROPE):
            e = dd % 16
            src = dd + 8 if e < 8 else dd - 8
            perm[g * LANES + MLA_NOPE + src, g * LANES + MLA_NOPE + dd] = -1.0 if e < 8 else 1.0
    wz_rot = wz @ jnp.asarray(perm)
    qn_ext = jnp.pad(q_norm, (0, 64)).reshape(1, 256)

    wkv = w_ukv.reshape(MLA_KV_RANK, MLA_HEADS, MLA_NOPE + MLA_V)
    kn = jnp.pad(wkv[:, :, :MLA_NOPE], ((0, 0), (0, 0), (0, 64))).reshape(MLA_KV_RANK, 512)
    vv = jnp.concatenate([wkv[:, :, MLA_NOPE:], wkv[:, :, MLA_NOPE:]], axis=2).reshape(MLA_KV_RANK, 512)
    w_kv = jnp.concatenate([kn, vv], axis=1)
    return (w_main, wz.astype(BF16), wz_rot.astype(BF16), qn_ext, w_kv.astype(BF16),
            kv_norm.reshape(1, MLA_KV_RANK))


def _rope_roll(x, cos, sin_signed, half):
    lane = _lane_iota(x.shape)
    rot = jnp.where((lane % (2 * half)) < half,
                    pltpu.roll(x, LANES - half, 1), pltpu.roll(x, half, 1))
    return x * cos + rot * sin_signed


def _in_proj_kernel(*refs, rope):
    if rope:
        (x_ref, a_ref, sh_ref, w_ref, wz_ref, wzr_ref, qn_ref, wkv_ref, kvn_ref,
         ca_ref, sa_ref, cr_ref, sr_ref, cm_ref, sm_ref, *outs) = refs
    else:
        (x_ref, a_ref, sh_ref, w_ref, wz_ref, wzr_ref, qn_ref, wkv_ref, kvn_ref, *outs) = refs
    qm_ref, km_ref, vm_ref, sq_ref, sk_ref, sv_ref, rq_ref, rk_ref, rv_ref, rg_ref = outs

    x = x_ref[0]
    h = x * lax.rsqrt(jnp.mean(x * x, axis=-1, keepdims=True) + NORM_EPS) * a_ref[0] + sh_ref[0]
    p = _dot(h.astype(BF16), w_ref[...])

    def put(ref, col, width, tables=None, half=None):
        for g in range(width // LANES):
            blk = p[:, col + g * LANES: col + (g + 1) * LANES]
            if tables is not None:
                blk = _rope_roll(blk, tables[0][...], tables[1][...], half)
            ref[0, :, g * LANES:(g + 1) * LANES] = blk.astype(ref.dtype)

    axial = (ca_ref, sa_ref) if rope else None
    flat = (cr_ref, sr_ref) if rope else None
    put(sq_ref, C_SQ, 512, axial, 16)
    put(sk_ref, C_SK, 256, axial, 16)
    put(sv_ref, C_SV, 256)
    put(rq_ref, C_RQ, 256, flat, 32)
    put(rk_ref, C_RK, 256, flat, 32)
    put(rv_ref, C_RV, 256)
    put(rg_ref, C_RG, 256)

    ext = p[:, C_EXT:C_EXT + 256]
    lane = _lane_iota(ext.shape)
    is_cq = lane < MLA_Q_RANK
    cq_sq = jnp.where(is_cq, ext * ext, 0.0)
    inv = lax.rsqrt(jnp.sum(cq_sq, axis=-1, keepdims=True) * (1.0 / MLA_Q_RANK) + NORM_EPS)
    z = jnp.where(is_cq, ext * inv * qn_ref[...], ext).astype(BF16)
    zw = _dot(z, wz_ref[...])
    ckv = p[:, C_CKV:C_CKV + MLA_KV_RANK]
    ckv = ckv * lax.rsqrt(jnp.mean(ckv * ckv, axis=-1, keepdims=True) + NORM_EPS) * kvn_ref[...]
    kv = _dot(ckv.astype(BF16), wkv_ref[...])
    if rope:
        zr = _dot(z, wzr_ref[...])
    for g in range(MLA_HEADS):
        sl = slice(g * LANES, (g + 1) * LANES)
        sk_ = slice(512 + g * LANES, 512 + (g + 1) * LANES)
        q_g, kpe_g = zw[:, sl], zw[:, sk_]
        if rope:
            q_g = q_g * cm_ref[...] + zr[:, sl] * sm_ref[...]
            kpe_g = kpe_g * cm_ref[...] + zr[:, sk_] * sm_ref[...]
        qm_ref[0, :, sl] = q_g.astype(BF16)
        km_ref[0, :, sl] = (kv[:, sl] + kpe_g).astype(BF16)
        vm_ref[0, :, sl] = kv[:, sk_].astype(BF16)


def _in_proj(x, a, sh, wts, tables, tm):
    B, L, D = x.shape
    w_main, wz, wzr, qn_ext, w_kv, kvn = wts
    rope = tables is not None
    tm = min(tm, L)
    bm = (lambda b: b) if a.shape[0] == B else (lambda b: 0)
    const = lambda i, b: (0, 0)
    in_specs = [pl.BlockSpec((1, tm, D), lambda i, b: (b, i, 0)),
                pl.BlockSpec((1, 1, D), lambda i, b: (bm(b), 0, 0)),
                pl.BlockSpec((1, 1, D), lambda i, b: (bm(b), 0, 0)),
                pl.BlockSpec(w_main.shape, const), pl.BlockSpec(wz.shape, const),
                pl.BlockSpec(wzr.shape, const), pl.BlockSpec(qn_ext.shape, const),
                pl.BlockSpec(w_kv.shape, const), pl.BlockSpec(kvn.shape, const)]
    args = [x, a, sh, w_main, wz, wzr, qn_ext, w_kv, kvn]
    if rope:
        in_specs += [pl.BlockSpec((tm, LANES), lambda i, b: (i, 0))] * 6
        args += list(tables)
    widths = (512, 512, 512, 512, 256, 256, 256, 256, 256, 256)
    return pl.pallas_call(
        functools.partial(_in_proj_kernel, rope=rope),
        out_shape=[jax.ShapeDtypeStruct((B, L, w), BF16) for w in widths],
        grid=(L // tm, B),
        in_specs=in_specs,
        out_specs=[pl.BlockSpec((1, tm, w), lambda i, b: (b, i, 0)) for w in widths],
        compiler_params=_cparams(("arbitrary", "arbitrary")),
        name="in_proj_rope" if rope else "in_proj_ctx",
    )(*args)


def _mla_kernel(q_ref, k_ref, v_ref, o_ref):
    lane = _lane_iota((q_ref.shape[1], LANES))
    outs = []
    for h in range(MLA_HEADS):
        sl = slice(h * LANES, (h + 1) * LANES)
        s = _dot_nt(q_ref[0, :, sl], k_ref[0, :, sl])
        m = jnp.max(s, axis=-1, keepdims=True)
        p = jnp.exp(s - m)
        l = jnp.sum(p, axis=-1, keepdims=True)
        o = _dot(p.astype(BF16), v_ref[0, :, sl])
        outs.append(o * (1.0 / l))
    for g in range(MLA_HEADS // 2):
        o_ref[0, :, g * LANES:(g + 1) * LANES] = jnp.where(
            lane < 64, outs[2 * g], outs[2 * g + 1]).astype(o_ref.dtype)


def _mla_attention(qm, km, vm, tq):
    B, L, _ = qm.shape
    Lk = km.shape[1]
    tq = min(tq, L)
    return pl.pallas_call(
        _mla_kernel,
        out_shape=jax.ShapeDtypeStruct((B, L, MLA_HEADS * MLA_V), BF16),
        grid=(B, L // tq),
        in_specs=[pl.BlockSpec((1, tq, 512), lambda b, i: (b, i, 0)),
                  pl.BlockSpec((1, Lk, 512), lambda b, i: (b, 0, 0)),
                  pl.BlockSpec((1, Lk, 512), lambda b, i: (b, 0, 0))],
        out_specs=pl.BlockSpec((1, tq, MLA_HEADS * MLA_V), lambda b, i: (b, i, 0)),
        compiler_params=_cparams(("arbitrary", "arbitrary")),
        name="mla_attention",
    )(qm, km, vm)


def _swa_kernel(sink_ref, q_ref, k_ref, v_ref, kc_ref, vc_ref, o_ref, *, banded):
    W = SWA_BLOCK
    i = pl.program_id(1)
    nb = pl.num_programs(1)
    if banded:
        prev = pl.multiple_of(jnp.maximum(i - 1, 0) * W, W)
        cur = pl.multiple_of(i * W, W)
        nxt = pl.multiple_of(jnp.minimum(i + 1, nb - 1) * W, W)
        k_all = jnp.concatenate([k_ref[0, pl.ds(prev, W), :], k_ref[0, pl.ds(cur, W), :],
                                 k_ref[0, pl.ds(nxt, W), :], kc_ref[0]], axis=0)
        v_all = jnp.concatenate([v_ref[0, pl.ds(prev, W), :], v_ref[0, pl.ds(cur, W), :],
                                 v_ref[0, pl.ds(nxt, W), :], vc_ref[0]], axis=0)
    else:
        k_all, v_all = kc_ref[0], vc_ref[0]
    tq = q_ref.shape[1]
    nk = k_all.shape[0]
    G = SWA_Q_HEADS // SWA_KV_HEADS
    row = lax.broadcasted_iota(jnp.int32, (G * tq, 1), 0)
    if banded:
        qq = lax.broadcasted_iota(jnp.int32, (G * tq, nk), 0) % tq
        kk = lax.broadcasted_iota(jnp.int32, (G * tq, nk), 1)
        key_t = i * W + kk - W
        in_band = (jnp.abs(kk - W - qq) <= W) & (key_t >= 0) & (key_t < nb * W)
        valid = (kk >= 3 * W) | in_band
    lo = _lane_iota((tq, LANES)) < 64
    for g in range(SWA_KV_HEADS):
        gs = slice(g * LANES, (g + 1) * LANES)
        k_g, v_g = k_all[:, gs], v_all[:, gs]
        pieces = []
        sink = jnp.zeros((G * tq, 1), F32)
        for jj in range(G // 2):
            j = g * (G // 2) + jj
            q_pair = q_ref[0, :, j * LANES:(j + 1) * LANES]
            zq = jnp.zeros_like(q_pair)
            pieces += [jnp.where(lo, q_pair, zq), jnp.where(lo, zq, q_pair)]
        for hh in range(G):
            sink = jnp.where(row // tq == hh, sink_ref[g * G + hh], sink)
        s = _dot_nt(jnp.concatenate(pieces, axis=0), k_g)
        if banded:
            s = jnp.where(valid, s, NEG_INF)
        m = jnp.maximum(jnp.max(s, axis=-1, keepdims=True), sink)
        p = jnp.exp(s - m)
        l = jnp.sum(p, axis=-1, keepdims=True) + jnp.exp(sink - m)
        o = _dot(p.astype(BF16), v_g) * (1.0 / l)
        for jj in range(G // 2):
            j = g * (G // 2) + jj
            o_ref[0, :, j * LANES:(j + 1) * LANES] = jnp.where(
                lo, o[2 * jj * tq:(2 * jj + 1) * tq], o[(2 * jj + 1) * tq:(2 * jj + 2) * tq]).astype(o_ref.dtype)


def _swa_attention(sink, q, k, v, kc, vc, banded):
    B, L, _ = q.shape
    Lc = kc.shape[1]
    tq = SWA_BLOCK if banded else L
    Lkv = k.shape[1]
    return pl.pallas_call(
        functools.partial(_swa_kernel, banded=banded),
        out_shape=jax.ShapeDtypeStruct((B, L, 512), BF16),
        grid=(B, L // tq),
        in_specs=[pl.BlockSpec(memory_space=pltpu.SMEM),
                  pl.BlockSpec((1, tq, 512), lambda b, i: (b, i, 0)),
                  pl.BlockSpec((1, Lkv, 256), lambda b, i: (b, 0, 0)),
                  pl.BlockSpec((1, Lkv, 256), lambda b, i: (b, 0, 0)),
                  pl.BlockSpec((1, Lc, 256), lambda b, i: (b, 0, 0)),
                  pl.BlockSpec((1, Lc, 256), lambda b, i: (b, 0, 0))],
        out_specs=pl.BlockSpec((1, tq, 512), lambda b, i: (b, i, 0)),
        compiler_params=_cparams(("arbitrary", "arbitrary")),
        name="swa_banded" if banded else "swa_context",
    )(sink, q, k, v, kc, vc)


def _ret_tables(decay_f, decay_b):
    C = RET_CHUNK
    lg_f = jnp.log(jax.nn.sigmoid(decay_f.astype(F32)))
    lg_b = jnp.log(jax.nn.sigmoid(decay_b.astype(F32)))
    idx = jnp.arange(C, dtype=F32)
    diff = idx[:, None] - idx[None, :]
    intra = (jnp.where(diff >= 0, jnp.exp(lg_f[:, None, None] * jnp.maximum(diff, 0.0)), 0.0)
             + jnp.where(diff <= 0, jnp.exp(lg_b[:, None, None] * jnp.maximum(-diff, 0.0)), 0.0))
    lanes = lambda t: jnp.repeat(t.T, RET_DK, axis=1)
    qdf = lanes(jnp.exp(lg_f[:, None] * (idx + 1.0)))
    qdb = lanes(jnp.exp(lg_b[:, None] * (C - idx)))
    kdf = lanes(jnp.exp(lg_f[:, None] * (C - 1.0 - idx)))
    kdb = lanes(jnp.exp(lg_b[:, None] * idx))
    cdf = jnp.repeat(jnp.exp(lg_f * C), RET_DV).reshape(1, -1)
    cdb = jnp.repeat(jnp.exp(lg_b * C), RET_DV).reshape(1, -1)
    return intra, qdf, qdb, kdf, kdb, cdf, cdb


def _ret_kernel(q_ref, k_ref, v_ref, g_ref, qx_ref, kx_ref, vx_ref, gx_ref,
                d_ref, qdf_ref, qdb_ref, kdf_ref, kdb_ref, cdf_ref, cdb_ref,
                *rest, ctx_out):
    if ctx_out:
        o_ref, ox_ref, sf, sb, sfx, sbx = rest
    else:
        o_ref, sf, sb, sfx, sbx = rest
        ox_ref = None
    C = RET_CHUNK
    nc = q_ref.shape[1] // C
    ncx = qx_ref.shape[1] // C
    NG = RET_HEADS // 2
    r = lax.broadcasted_iota(jnp.int32, (LANES, LANES), 0)
    cidx = lax.broadcasted_iota(jnp.int32, (LANES, LANES), 1)
    blockdiag = (r // 64) == (cidx // 64)
    lane = _lane_iota((C, LANES))
    lo = lane < 64

    def kv_sum(kr, vr, c0, kd_ref, j):
        gs = slice(j * LANES, (j + 1) * LANES)
        kd = (kr[0, pl.ds(c0, C), gs].astype(F32) * kd_ref[:, gs]).astype(BF16)
        return jnp.where(blockdiag, _dot_tn(kd, vr[0, pl.ds(c0, C), gs]), 0.0)

    def state_pass(kr, vr, st, n, kd_ref, cd_ref, init, reverse):
        def body(t, carry):
            c = (n - 1 - t) if reverse else t
            c0 = pl.multiple_of(c * C, C)
            new = []
            for j in range(NG):
                st[c, j] = carry[j]
                new.append(carry[j] * cd_ref[:, j * LANES:(j + 1) * LANES] + kv_sum(kr, vr, c0, kd_ref, j))
            return tuple(new)
        return lax.fori_loop(0, n, body, init)

    zero = tuple(jnp.zeros((LANES, LANES), F32) for _ in range(NG))
    s_f = state_pass(kx_ref, vx_ref, sfx, ncx, kdf_ref, cdf_ref, zero, False)
    state_pass(k_ref, v_ref, sf, nc, kdf_ref, cdf_ref, s_f, False)
    s_b = state_pass(kx_ref, vx_ref, sbx, ncx, kdb_ref, cdb_ref, zero, True)
    state_pass(k_ref, v_ref, sb, nc, kdb_ref, cdb_ref, s_b, True)

    def out_pass(qr, kr, vr, gr, orf, stf, stb, n):
        def body(c, _):
            c0 = pl.multiple_of(c * C, C)
            for j in range(NG):
                gs = slice(j * LANES, (j + 1) * LANES)
                qg, kg, vg = qr[0, pl.ds(c0, C), gs], kr[0, pl.ds(c0, C), gs], vr[0, pl.ds(c0, C), gs]
                halves = []
                for half in range(2):
                    zq = jnp.zeros_like(qg)
                    qh = jnp.where(lo, qg, zq) if half == 0 else jnp.where(lo, zq, qg)
                    att = _dot_nt(qh, kg) * d_ref[2 * j + half]
                    halves.append(_dot(att.astype(BF16), vg))
                o = jnp.where(lo, halves[0], halves[1])
                qf = qg.astype(F32)
                qd = jnp.concatenate([(qf * qdf_ref[:, gs]).astype(BF16),
                                      (qf * qdb_ref[:, gs]).astype(BF16)], axis=1)
                s_cat = jnp.concatenate([stf[c, j], stb[c, j]], axis=0).astype(BF16)
                o = o + _dot(qd, s_cat)
                o2 = o * o
                ms = jnp.where(lo, jnp.sum(jnp.where(lo, o2, 0.0), axis=-1, keepdims=True),
                               jnp.sum(jnp.where(lo, 0.0, o2), axis=-1, keepdims=True)) * (1.0 / RET_DV)
                gate = gr[0, pl.ds(c0, C), gs].astype(F32)
                y = o * lax.rsqrt(ms + NORM_EPS) * (gate * jax.nn.sigmoid(gate))
                orf[0, pl.ds(c0, C), gs] = y.astype(orf.dtype)
            return 0
        lax.fori_loop(0, n, body, 0)

    out_pass(q_ref, k_ref, v_ref, g_ref, o_ref, sf, sb, nc)
    if ctx_out:
        out_pass(qx_ref, kx_ref, vx_ref, gx_ref, ox_ref, sfx, sbx, ncx)


def _retention(q, k, v, g, qx, kx, vx, gx, tabs, ctx_out):
    B, L, W = q.shape
    Lc = qx.shape[1]
    nc, ncx = L // RET_CHUNK, Lc // RET_CHUNK
    lat = pl.BlockSpec((1, L, W), lambda b: (b, 0, 0))
    cx = pl.BlockSpec((1, Lc, W), lambda b: (b, 0, 0))
    full = lambda a: pl.BlockSpec(a.shape, lambda b: (0,) * a.ndim)
    out_shape = [jax.ShapeDtypeStruct((B, L, W), BF16)]
    out_specs = [lat]
    if ctx_out:
        out_shape.append(jax.ShapeDtypeStruct((B, Lc, W), BF16))
        out_specs.append(cx)
    res = pl.pallas_call(
        functools.partial(_ret_kernel, ctx_out=ctx_out),
        out_shape=out_shape,
        grid=(B,),
        in_specs=[lat, lat, lat, lat, cx, cx, cx, cx] + [full(t) for t in tabs],
        out_specs=out_specs,
        scratch_shapes=[pltpu.VMEM((nc, 2, LANES, LANES), F32), pltpu.VMEM((nc, 2, LANES, LANES), F32),
                        pltpu.VMEM((ncx, 2, LANES, LANES), F32), pltpu.VMEM((ncx, 2, LANES, LANES), F32)],
        compiler_params=_cparams(("arbitrary",)),
        name="retention",
    )(q, k, v, g, qx, kx, vx, gx, *tabs)
    return (res[0], res[1]) if ctx_out else (res[0], None)


def _split_hi_lo(a):
    hi = a.astype(BF16)
    return hi, (a - hi.astype(F32)).astype(BF16)


def _pack_bf16_pairs(a):
    w = a.shape[1] // 2
    bits = pltpu.bitcast(a.astype(BF16).astype(F32), jnp.uint32)
    return bits[:, :w] | (bits[:, w:] >> 16)


def _unpack_bf16_pairs(u):
    hi = pltpu.bitcast(u & jnp.uint32(0xFFFF0000), F32)
    lo = pltpu.bitcast(u << 16, F32)
    return jnp.concatenate([hi, lo], axis=1)


def _out_proj_kernel(oa_ref, ob_ref, oc_ref, x_ref, g1_ref, a2_ref, sh2_ref, wo_ref, *rest, route):
    if route:
        r_ref, x1_ref, h2_ref, rt_ref = rest
    else:
        x1_ref, h2_ref = rest
    y = (_dot(oa_ref[0], wo_ref[0:256, :]) + _dot(ob_ref[0], wo_ref[256:768, :])
         + _dot(oc_ref[0], wo_ref[768:1024, :]))
    x1 = x_ref[0] + g1_ref[0] * y
    x1_ref[0] = x1
    h2 = x1 * lax.rsqrt(jnp.mean(x1 * x1, axis=-1, keepdims=True) + NORM_EPS) * a2_ref[0] + sh2_ref[0]
    if not route:
        h2_ref[0] = h2.astype(h2_ref.dtype)
    else:
        h2_ref[0] = _pack_bf16_pairs(h2)
        tm = h2.shape[0]
        h_hi, h_lo = _split_hi_lo(h2)
        prod = _dot(jnp.concatenate([h_hi, h_lo], axis=0), r_ref[...])
        logits = prod[:tm] + pltpu.roll(prod[:tm], LANES - N_EXPERTS, 1) + prod[tm:]
        lane = _lane_iota(logits.shape)
        logits = jnp.where(lane < N_EXPERTS, logits, -jnp.inf)
        v1 = jnp.max(logits, axis=-1, keepdims=True)
        i1 = jnp.min(jnp.where(logits == v1, lane, LANES), axis=-1, keepdims=True)
        rest_l = jnp.where(lane == i1, -jnp.inf, logits)
        v2 = jnp.max(rest_l, axis=-1, keepdims=True)
        i2 = jnp.min(jnp.where(rest_l == v2, lane, LANES), axis=-1, keepdims=True)
        e2 = jnp.exp(v2 - v1)
        w1 = 1.0 / (1.0 + e2)
        w2 = e2 * w1
        rt_ref[0] = jnp.where(lane == 0, i1.astype(F32),
                              jnp.where(lane == 1, i2.astype(F32),
                                        jnp.where(lane == 2, w1, jnp.where(lane == 3, w2, 0.0))))


def _out_proj(oa, ob, oc, x, g1, a2, sh2, wo, router, tm):
    B, L, D = x.shape
    tm = min(tm, L)
    route = router is not None
    h2_dtype, h2_w = (jnp.uint32, D // 2) if route else (BF16, D)
    bm = (lambda b: b) if g1.shape[0] == B else (lambda b: 0)
    tok = lambda w: pl.BlockSpec((1, tm, w), lambda b, i: (b, i, 0))
    vec = pl.BlockSpec((1, 1, D), lambda b, i: (bm(b), 0, 0))
    in_specs = [tok(256), tok(512), tok(256), tok(D), vec, vec, vec,
                pl.BlockSpec(wo.shape, lambda b, i: (0, 0))]
    args = [oa, ob, oc, x, g1, a2, sh2, wo]
    out_shape = [jax.ShapeDtypeStruct((B, L, D), F32), jax.ShapeDtypeStruct((B, L, h2_w), h2_dtype)]
    out_specs = [tok(D), tok(h2_w)]
    if route:
        in_specs.append(pl.BlockSpec(router.shape, lambda b, i: (0, 0)))
        args.append(router)
        out_shape.append(jax.ShapeDtypeStruct((B, L, LANES), F32))
        out_specs.append(tok(LANES))
    return pl.pallas_call(
        functools.partial(_out_proj_kernel, route=route),
        out_shape=out_shape,
        grid=(B, L // tm),
        in_specs=in_specs,
        out_specs=out_specs,
        compiler_params=_cparams(("arbitrary", "arbitrary")),
        name="out_proj_route" if route else "out_proj",
    )(*args)


def _ffn_kernel(h_ref, x_ref, g2_ref, wg_ref, wu_ref, wd_ref, o_ref, *, n_chunks):
    h = h_ref[0]
    tf = wg_ref.shape[1] // n_chunks
    acc = None
    for f in range(n_chunks):
        fs = slice(f * tf, (f + 1) * tf)
        g = _dot(h, wg_ref[:, fs])
        u = _dot(h, wu_ref[:, fs])
        a = (g * jax.nn.sigmoid(g) * u).astype(BF16)
        part = _dot(a, wd_ref[fs, :])
        acc = part if acc is None else acc + part
    o_ref[0] = x_ref[0] + g2_ref[0] * acc


def _ffn(h2, x1, g2, wg, wu, wd, tm):
    B, L, D = x1.shape
    tm = min(tm, L)
    bm = (lambda b: b) if g2.shape[0] == B else (lambda b: 0)
    tok = pl.BlockSpec((1, tm, D), lambda b, i: (b, i, 0))
    wspec = lambda w: pl.BlockSpec(w.shape, lambda b, i: (0, 0), pipeline_mode=pl.Buffered(1))
    return pl.pallas_call(
        functools.partial(_ffn_kernel, n_chunks=2),
        out_shape=jax.ShapeDtypeStruct((B, L, D), F32),
        grid=(B, L // tm),
        in_specs=[tok, tok, pl.BlockSpec((1, 1, D), lambda b, i: (bm(b), 0, 0)),
                  wspec(wg), wspec(wu), wspec(wd)],
        out_specs=tok,
        compiler_params=_cparams(("arbitrary", "arbitrary")),
        name="dense_ffn",
    )(h2, x1, g2, wg, wu, wd)


ROW_UNROLL = 8


def _dispatch_kernel(d0_ref, d1_ref, h_ref, init_ref, o_ref, sem):
    del init_ref
    rows = h_ref.shape[0]

    def start(r, _):
        pltpu.make_async_copy(h_ref.at[pl.ds(r, 1)], o_ref.at[pl.ds(d0_ref[0, 0, r], 1)], sem).start()
        pltpu.make_async_copy(h_ref.at[pl.ds(r, 1)], o_ref.at[pl.ds(d1_ref[0, 0, r], 1)], sem).start()
        return 0

    lax.fori_loop(0, rows, start, 0, unroll=ROW_UNROLL)
    for _ in range(2):
        pltpu.make_async_copy(h_ref, o_ref.at[pl.ds(0, rows)], sem).wait()


def _dispatch(hp, d0, d1, n_slots, td):
    N, W = hp.shape
    idx_spec = pl.BlockSpec((1, 1, td), lambda i: (i, 0, 0), memory_space=pltpu.SMEM)
    return pl.pallas_call(
        _dispatch_kernel,
        out_shape=jax.ShapeDtypeStruct((n_slots, W), hp.dtype),
        grid=(N // td,),
        in_specs=[idx_spec, idx_spec, pl.BlockSpec((td, W), lambda i: (i, 0)),
                  pl.BlockSpec(memory_space=pl.ANY)],
        out_specs=pl.BlockSpec(memory_space=pl.ANY),
        scratch_shapes=[pltpu.SemaphoreType.DMA],
        input_output_aliases={3: 0},
        compiler_params=_cparams(("arbitrary",)),
        name="moe_dispatch",
    )(d0.reshape(N // td, 1, td), d1.reshape(N // td, 1, td), hp, jnp.zeros((n_slots, W), hp.dtype))


def _gmm_kernel(te_ref, nu_ref, x_ref, wg_ref, wu_ref, wd_ref, o_ref, acc_ref, xb_ref):
    i, f = pl.program_id(0), pl.program_id(1)

    @pl.when((i < nu_ref[0]) & (f == 0))
    def _():
        xb_ref[...] = _unpack_bf16_pairs(x_ref[...]).astype(BF16)

    @pl.when(i < nu_ref[0])
    def _():
        x = xb_ref[...]
        g = _dot(x, wg_ref[0])
        u = _dot(x, wu_ref[0])
        a = (g * jax.nn.sigmoid(g) * u).astype(BF16)
        part = _dot(a, wd_ref[0])

        @pl.when(f == 0)
        def _():
            acc_ref[...] = part

        @pl.when(f > 0)
        def _():
            acc_ref[...] += part

        @pl.when(f == pl.num_programs(1) - 1)
        def _():
            o_ref[...] = _pack_bf16_pairs(acc_ref[...])

    @pl.when((i >= nu_ref[0]) & (f == pl.num_programs(1) - 1))
    def _():
        o_ref[...] = jnp.zeros_like(o_ref)


def _grouped_swiglu(xs, tile_expert, n_used, wg, wu, wd, tm, tf):
    P, W = xs.shape
    D = 2 * W
    E, _, F = wg.shape
    nf = F // tf
    n_tiles = P // tm

    def fsel(i, f, nu):
        return jnp.where(i < nu[0], f, nf - 1)

    return pl.pallas_call(
        _gmm_kernel,
        out_shape=jax.ShapeDtypeStruct((P, W), jnp.uint32),
        grid_spec=pltpu.PrefetchScalarGridSpec(
            num_scalar_prefetch=2,
            grid=(n_tiles, nf),
            in_specs=[pl.BlockSpec((tm, W), lambda i, f, te, nu: (i, 0)),
                      pl.BlockSpec((1, D, tf), lambda i, f, te, nu: (te[i], 0, fsel(i, f, nu))),
                      pl.BlockSpec((1, D, tf), lambda i, f, te, nu: (te[i], 0, fsel(i, f, nu))),
                      pl.BlockSpec((1, tf, D), lambda i, f, te, nu: (te[i], fsel(i, f, nu), 0))],
            out_specs=pl.BlockSpec((tm, W), lambda i, f, te, nu: (i, 0)),
            scratch_shapes=[pltpu.VMEM((tm, D), F32), pltpu.VMEM((tm, D), BF16)]),
        compiler_params=_cparams(("arbitrary", "arbitrary")),
        name="moe_grouped_swiglu",
    )(tile_expert, n_used, xs, wg, wu, wd)


def _combine_kernel(d0_ref, d1_ref, y_ref, x_ref, g2_ref, rt_ref, fg_ref, o_ref, buf, sem):
    rows = o_ref.shape[1]

    def start(r, _):
        pltpu.make_async_copy(y_ref.at[pl.ds(d0_ref[0, 0, r], 1)], buf.at[0, pl.ds(r, 1)], sem).start()
        pltpu.make_async_copy(y_ref.at[pl.ds(d1_ref[0, 0, r], 1)], buf.at[1, pl.ds(r, 1)], sem).start()
        return 0

    lax.fori_loop(0, rows, start, 0, unroll=ROW_UNROLL)
    for k in range(2):
        pltpu.make_async_copy(y_ref.at[pl.ds(0, rows)], buf.at[k], sem).wait()
    rt = rt_ref[0]
    moe = rt[:, 2:3] * _unpack_bf16_pairs(buf[0]) + rt[:, 3:4] * _unpack_bf16_pairs(buf[1])
    x2 = x_ref[0] + g2_ref[0] * moe
    o_ref[0] = x2 * lax.rsqrt(jnp.mean(x2 * x2, axis=-1, keepdims=True) + NORM_EPS) * fg_ref[...]


def _combine(y, d0, d1, x1, g2, rt, final_g, tc):
    B, L, D = x1.shape
    tc = min(tc, L)
    nt = L // tc
    idx_spec = pl.BlockSpec((1, 1, tc), lambda b, i: (b * nt + i, 0, 0), memory_space=pltpu.SMEM)
    tok = lambda w: pl.BlockSpec((1, tc, w), lambda b, i: (b, i, 0))
    return pl.pallas_call(
        _combine_kernel,
        out_shape=jax.ShapeDtypeStruct((B, L, D), F32),
        grid=(B, nt),
        in_specs=[idx_spec, idx_spec, pl.BlockSpec(memory_space=pl.ANY), tok(D),
                  pl.BlockSpec((1, 1, D), lambda b, i: (b, 0, 0)), tok(LANES),
                  pl.BlockSpec((1, D), lambda b, i: (0, 0))],
        out_specs=tok(D),
        scratch_shapes=[pltpu.VMEM((2, tc, D // 2), jnp.uint32), pltpu.SemaphoreType.DMA],
        compiler_params=_cparams(("arbitrary", "arbitrary")),
        name="moe_combine_norm",
    )(d0.reshape(B * nt, 1, tc), d1.reshape(B * nt, 1, tc), y, x1, g2, rt, final_g.reshape(1, D))


def _routing(rt, tm):
    n = rt.shape[0]
    e = rt[:, :2].astype(jnp.int32).reshape(-1)
    onehot = (e[:, None] == jnp.arange(N_EXPERTS)[None, :]).astype(jnp.int32)
    csum = jnp.cumsum(onehot, axis=0)
    rank = jnp.sum((csum - onehot) * onehot, axis=1)
    counts = csum[-1]
    padded = ((counts + tm - 1) // tm) * tm
    ends = jnp.cumsum(padded)
    offs = ends - padded
    dest = jnp.sum(onehot * offs[None, :], axis=1) + rank
    n_slots = 2 * n + N_EXPERTS * tm
    tile_start = jnp.arange(n_slots // tm, dtype=jnp.int32) * tm
    n_used = (ends[-1] // tm).astype(jnp.int32)
    te = jnp.sum((tile_start[:, None] >= ends[None, :]).astype(jnp.int32), axis=1)
    last = jnp.sum((ends[-1] - tm >= ends).astype(jnp.int32))
    te = jnp.where(tile_start < ends[-1], te, last).astype(jnp.int32)
    d = dest.reshape(n, 2)
    return n_slots, te, n_used.reshape(1), d[:, 0], d[:, 1]


TM_IN = 256
TQ_MLA = 256
TM_OUT = 512
TM_FFN = 512
TM_MOE = 512
TF_MOE = 896
TD_DISPATCH = 512
TC_COMBINE = 256


def kernel(x, c, ctx, c_ctx, w_mod, b_mod, norm1_g, norm2_g, w_in, mla_q_norm, mla_w_uq, mla_kv_norm,
           mla_w_ukv, swa_sink, ret_decay_fwd, ret_decay_bwd, w_out, ffn_w_gate, ffn_w_up, ffn_w_down,
           moe_router, moe_w_gate, moe_w_up, moe_w_down, final_norm_g):
    B, L, D = x.shape
    Lc = ctx.shape[1]
    depth = w_mod.shape[0]
    xc = ctx

    cond = jnp.concatenate([c, c_ctx[None], jnp.zeros((16 - B - 1, D), F32)], axis=0)
    mod_all = _modulation(cond, w_mod, b_mod)
    tables = _rope_tables(L)

    for layer in range(depth):
        last = layer == depth - 1
        mod = mod_all[layer].reshape(16, 6, 1, D)
        sh1, sc1, g1, sh2, sc2, g2 = (mod[:B, j] for j in range(6))
        sh1x, sc1x, g1x, sh2x, sc2x, g2x = (mod[B:B + 1, j] for j in range(6))
        n1, n2 = norm1_g[layer], norm2_g[layer]

        wts = _prep_in_weights(w_in[layer], mla_q_norm[layer], mla_w_uq[layer],
                               mla_kv_norm[layer], mla_w_ukv[layer])
        lat = _in_proj(x, n1 * (1.0 + sc1), sh1, wts, tables, TM_IN)
        cx = _in_proj(xc, n1 * (1.0 + sc1x), sh1x, wts, None, TM_IN)
        qm, km, vm, sq, sk, sv, rq, rk, rv, rg = lat
        qmx, kmx, vmx, sqx, skx, svx, rqx, rkx, rvx, rgx = cx

        o_a = _mla_attention(qm, jnp.concatenate([km, kmx], axis=1),
                             jnp.concatenate([vm, vmx], axis=1), TQ_MLA)
        sink = swa_sink[layer].astype(F32)
        o_b = _swa_attention(sink, sq, sk, sv, skx, svx, True)
        rtabs = _ret_tables(ret_decay_fwd[layer], ret_decay_bwd[layer])
        o_c, oc_c = _retention(rq, rk, rv, rg, rqx, rkx, rvx, rgx, rtabs, not last)
        wo = w_out[layer].astype(BF16)

        if layer % 2 == 0:
            i = layer // 2
            wg, wu, wd = (ffn_w_gate[i].astype(BF16), ffn_w_up[i].astype(BF16),
                          ffn_w_down[i].astype(BF16))
            x1, h2 = _out_proj(o_a, o_b, o_c, x, g1, n2 * (1.0 + sc2), sh2, wo, None, TM_OUT)
            x_next = _ffn(h2, x1, g2, wg, wu, wd, TM_FFN)
        else:
            i = layer // 2
            r_hi = moe_router[i].astype(BF16)
            r_lo = (moe_router[i] - r_hi.astype(F32)).astype(BF16)
            router = jnp.pad(jnp.concatenate([r_hi, r_lo], axis=1), ((0, 0), (0, LANES - 2 * N_EXPERTS)))
            x1, h2p, rt = _out_proj(o_a, o_b, o_c, x, g1, n2 * (1.0 + sc2), sh2, wo, router, TM_OUT)
            n_slots, te, n_used, d0, d1 = _routing(rt.reshape(B * L, LANES), TM_MOE)
            xs = _dispatch(h2p.reshape(B * L, D // 2), d0, d1, n_slots, TD_DISPATCH)
            y = _grouped_swiglu(xs, te, n_used, moe_w_gate[i].astype(BF16), moe_w_up[i].astype(BF16),
                                moe_w_down[i].astype(BF16), TM_MOE, TF_MOE)
            if last:
                return _combine(y, d0, d1, x1, g2, rt, final_norm_g, TC_COMBINE)
            raise NotImplementedError("expert layer is only supported as the last layer")

        if not last:
            oc_a = _mla_attention(qmx, kmx, vmx, TQ_MLA)
            oc_b = _swa_attention(sink, sqx, skx, svx, skx, svx, False)
            xc1, hc2 = _out_proj(oc_a, oc_b, oc_c, xc, g1x, n2 * (1.0 + sc2x), sh2x, wo, None, TM_OUT)
            xc = _ffn(hc2, xc1, g2x, wg, wu, wd, TM_FFN)
        x = x_next
    raise NotImplementedError("trunk must end with the expert layer")
```

```python
import functools
import math

import numpy as np
import jax
import jax.numpy as jnp
from jax import lax
from jax.experimental import pallas as pl
from jax.experimental.pallas import tpu as pltpu

F32 = jnp.float32
BF16 = jnp.bfloat16

D_MODEL = 1024
DEPTH = 2
GRID_W = 64
HEAD_DIM = 64
NORM_EPS = 1e-6
ROPE_BASE = 10000.0
NEG_INF = -1e30

MLA_HEADS = 4
MLA_Q_RANK = 192
MLA_KV_RANK = 128
MLA_NOPE = 64
MLA_ROPE = 32
MLA_V = 64

SWA_Q_HEADS = 8
SWA_KV_HEADS = 2
SWA_BLOCK = 128

RET_HEADS = 4
RET_DK = 64
RET_DV = 64
RET_CHUNK = 128

D_FF = 2816
N_EXPERTS = 8
D_FF_EXPERT = 3584

LOG2E = math.log2(math.e)
MLA_ONE_LANE = (64, 0)
LANES = 128
VMEM_LIMIT = 56 * 1024 * 1024

C_SQ, C_SK, C_SV = 0, 512, 768
C_RQ, C_RK, C_RV, C_RG = 1024, 1280, 1536, 1792
C_CKV, C_EXT = 2048, 2176
IN_COLS = 2432


def _cparams(sem, vmem=VMEM_LIMIT):
    return pltpu.CompilerParams(dimension_semantics=sem, vmem_limit_bytes=vmem)


def _dot(a, b):
    return jnp.dot(a, b, preferred_element_type=F32)


def _dot_nt(a, b):
    return lax.dot_general(a, b, (((1,), (1,)), ((), ())), preferred_element_type=F32)


def _dot_tn(a, b):
    return lax.dot_general(a, b, (((0,), (0,)), ((), ())), preferred_element_type=F32)


def _lane_iota(shape):
    return lax.broadcasted_iota(jnp.int32, shape, len(shape) - 1)


def _mod_kernel(c_ref, w_ref, b_ref, o_ref):
    c = c_ref[...]
    c = c * jax.nn.sigmoid(c)
    o_ref[0] = jnp.dot(c, w_ref[0], preferred_element_type=F32,
                       precision=lax.Precision.HIGHEST) + b_ref[0]


def _modulation(cond, w_mod, b_mod):
    depth, d, n = w_mod.shape
    rows = cond.shape[0]
    tn = 1024
    return pl.pallas_call(
        _mod_kernel,
        out_shape=jax.ShapeDtypeStruct((depth, rows, n), F32),
        grid=(depth, n // tn),
        in_specs=[pl.BlockSpec((rows, d), lambda l, j: (0, 0)),
                  pl.BlockSpec((1, d, tn), lambda l, j: (l, 0, j)),
                  pl.BlockSpec((1, 1, tn), lambda l, j: (l, 0, j))],
        out_specs=pl.BlockSpec((1, rows, tn), lambda l, j: (l, 0, j)),
        compiler_params=_cparams(("arbitrary", "arbitrary")),
        name="modulation",
    )(cond, w_mod, b_mod.reshape(depth, 1, n))


def _angles(pos, dim):
    inv = (ROPE_BASE ** (-np.arange(0, dim, 2, dtype=np.float32) / dim)).astype(np.float32)
    ang = pos.astype(np.float32)[:, None] * inv[None, :]
    return np.concatenate([ang, ang], axis=-1).astype(np.float64)


def _rope_tables(length):
    t = np.arange(length)
    rows, cols = t // GRID_W, t % GRID_W
    ar, ac = _angles(rows, 32), _angles(cols, 32)
    sign32 = np.concatenate([-np.ones(16), np.ones(16)])
    cos_a = np.concatenate([np.cos(ar), np.cos(ac)], axis=-1)
    sin_a = np.concatenate([np.sin(ar) * sign32, np.sin(ac) * sign32], axis=-1)
    cos_a, sin_a = np.tile(cos_a, (1, 2)), np.tile(sin_a, (1, 2))
    at = _angles(t, 64)
    sign64 = np.concatenate([-np.ones(32), np.ones(32)])
    cos_r, sin_r = np.tile(np.cos(at), (1, 2)), np.tile(np.sin(at) * sign64, (1, 2))
    mr, mc = _angles(rows, 16), _angles(cols, 16)
    cos_m = np.ones((length, LANES))
    sin_m = np.zeros((length, LANES))
    cos_m[:, 64:96] = np.concatenate([np.cos(mr), np.cos(mc)], axis=-1)
    sin_m[:, 64:96] = np.concatenate([np.sin(mr), np.sin(mc)], axis=-1)
    return tuple(jnp.asarray(a, F32) for a in (cos_a, sin_a, cos_r, sin_r, cos_m, sin_m))


def _prep_in_weights(w_in, q_norm, w_uq, kv_norm, w_ukv):
    cuts = np.cumsum([MLA_Q_RANK, MLA_KV_RANK, MLA_ROPE, 512, 128, 128, 256, 256, 256, 256])[:-1]
    cq, ckv, kpe, sq, sk, sv, rq, rk, rv, rg = jnp.split(w_in, [int(v) for v in cuts], axis=1)
    dup = lambda w: jnp.concatenate([w[:, :64], w[:, :64], w[:, 64:], w[:, 64:]], axis=1)
    d = w_in.shape[0]
    w_main = jnp.concatenate(
        [sq * (HEAD_DIM ** -0.5 * LOG2E), dup(sk), dup(sv), rq, rk * RET_DK ** -0.5, rv, rg, ckv,
         cq, kpe, jnp.zeros((d, 32), F32)], axis=1).astype(BF16)

    scale = (MLA_NOPE + MLA_ROPE) ** -0.5 * LOG2E
    wq = (w_uq * scale).reshape(MLA_Q_RANK, MLA_HEADS, MLA_NOPE + MLA_ROPE)
    wq = jnp.pad(wq, ((0, 64), (0, 0), (0, 32))).reshape(256, 512)
    place = np.zeros((256, 512), np.float32)
    for h in range(MLA_HEADS):
        for dd in range(MLA_ROPE):
            place[MLA_Q_RANK + dd, h * LANES + MLA_NOPE + dd] = 1.0
    wz = jnp.concatenate([wq, jnp.asarray(place)], axis=1)
    perm = np.zeros((1024, 1024), np.float32)
    for g in range(8):
        for dd in range(MLA_ROPE):
            e = dd % 16
            src = dd + 8 if e < 8 else dd - 8
            perm[g * LANES + MLA_NOPE + src, g * LANES + MLA_NOPE + dd] = -1.0 if e < 8 else 1.0
    wz_rot = wz @ jnp.asarray(perm)
    qn_ext = jnp.pad(q_norm, (0, 64)).reshape(1, 256)

    wkv = w_ukv.reshape(MLA_KV_RANK, MLA_HEADS, MLA_NOPE + MLA_V)
    kn = jnp.pad(wkv[:, :, :MLA_NOPE], ((0, 0), (0, 0), (0, 64))).reshape(MLA_KV_RANK, 512)
    vals = wkv[:, :, MLA_NOPE:]
    vv = jnp.stack([jnp.pad(vals[:, h], ((0, 0), (64, 0) if h % 2 else (0, 64))) for h in range(MLA_HEADS)],
                   axis=1).reshape(MLA_KV_RANK, 512)
    w_kv = jnp.concatenate([kn, vv], axis=1)
    return (w_main, wz.astype(BF16), wz_rot.astype(BF16), qn_ext, w_kv.astype(BF16),
            kv_norm.reshape(1, MLA_KV_RANK))


def _rope_roll(x, cos, sin_signed, half):
    lane = _lane_iota(x.shape)
    rot = jnp.where((lane % (2 * half)) < half,
                    pltpu.roll(x, LANES - half, 1), pltpu.roll(x, half, 1))
    return x * cos + rot * sin_signed


def _in_proj_kernel(*refs, rope):
    if rope:
        (x_ref, a_ref, sh_ref, w_ref, wz_ref, wzr_ref, qn_ref, wkv_ref, kvn_ref,
         ca_ref, sa_ref, cr_ref, sr_ref, cm_ref, sm_ref, *outs) = refs
    else:
        (x_ref, a_ref, sh_ref, w_ref, wz_ref, wzr_ref, qn_ref, wkv_ref, kvn_ref, *outs) = refs
    qm_ref, km_ref, vm_ref, sq_ref, sk_ref, sv_ref, rq_ref, rk_ref, rv_ref, rg_ref = outs

    x = x_ref[0]
    h = x * lax.rsqrt(jnp.mean(x * x, axis=-1, keepdims=True) + NORM_EPS) * a_ref[0] + sh_ref[0]
    p = _dot(h.astype(BF16), w_ref[...])

    def put(ref, col, width, tables=None, half=None):
        for g in range(width // LANES):
            blk = p[:, col + g * LANES: col + (g + 1) * LANES]
            if tables is not None:
                blk = _rope_roll(blk, tables[0][...], tables[1][...], half)
            ref[0, :, g * LANES:(g + 1) * LANES] = blk.astype(ref.dtype)

    axial = (ca_ref, sa_ref) if rope else None
    flat = (cr_ref, sr_ref) if rope else None
    put(sq_ref, C_SQ, 512, axial, 16)
    put(sk_ref, C_SK, 256, axial, 16)
    put(sv_ref, C_SV, 256)
    put(rq_ref, C_RQ, 256, flat, 32)
    put(rk_ref, C_RK, 256, flat, 32)
    put(rv_ref, C_RV, 256)
    put(rg_ref, C_RG, 256)

    ext = p[:, C_EXT:C_EXT + 256]
    lane = _lane_iota(ext.shape)
    is_cq = lane < MLA_Q_RANK
    cq_sq = jnp.where(is_cq, ext * ext, 0.0)
    inv = lax.rsqrt(jnp.sum(cq_sq, axis=-1, keepdims=True) * (1.0 / MLA_Q_RANK) + NORM_EPS)
    z = jnp.where(is_cq, ext * inv * qn_ref[...], ext).astype(BF16)
    zw = _dot(z, wz_ref[...])
    ckv = p[:, C_CKV:C_CKV + MLA_KV_RANK]
    ckv = ckv * lax.rsqrt(jnp.mean(ckv * ckv, axis=-1, keepdims=True) + NORM_EPS) * kvn_ref[...]
    kv = _dot(ckv.astype(BF16), wkv_ref[...])
    if rope:
        zr = _dot(z, wzr_ref[...])
    lane_g = _lane_iota((ext.shape[0], LANES))
    for g in range(MLA_HEADS):
        sl = slice(g * LANES, (g + 1) * LANES)
        sk_ = slice(512 + g * LANES, 512 + (g + 1) * LANES)
        q_g, kpe_g = zw[:, sl], zw[:, sk_]
        if rope:
            q_g = q_g * cm_ref[...] + zr[:, sl] * sm_ref[...]
            kpe_g = kpe_g * cm_ref[...] + zr[:, sk_] * sm_ref[...]
        qm_ref[0, :, sl] = q_g.astype(BF16)
        km_ref[0, :, sl] = (kv[:, sl] + kpe_g).astype(BF16)
        vm_ref[0, :, sl] = jnp.where(lane_g == MLA_ONE_LANE[g % 2], 1.0, kv[:, sk_]).astype(BF16)


def _in_proj(x, a, sh, wts, tables, tm):
    B, L, D = x.shape
    w_main, wz, wzr, qn_ext, w_kv, kvn = wts
    rope = tables is not None
    tm = min(tm, L)
    bm = (lambda b: b) if a.shape[0] == B else (lambda b: 0)
    const = lambda i, b: (0, 0)
    in_specs = [pl.BlockSpec((1, tm, D), lambda i, b: (b, i, 0)),
                pl.BlockSpec((1, 1, D), lambda i, b: (bm(b), 0, 0)),
                pl.BlockSpec((1, 1, D), lambda i, b: (bm(b), 0, 0)),
                pl.BlockSpec(w_main.shape, const), pl.BlockSpec(wz.shape, const),
                pl.BlockSpec(wzr.shape, const), pl.BlockSpec(qn_ext.shape, const),
                pl.BlockSpec(w_kv.shape, const), pl.BlockSpec(kvn.shape, const)]
    args = [x, a, sh, w_main, wz, wzr, qn_ext, w_kv, kvn]
    if rope:
        in_specs += [pl.BlockSpec((tm, LANES), lambda i, b: (i, 0))] * 6
        args += list(tables)
    widths = (512, 512, 512, 512, 256, 256, 256, 256, 256, 256)
    return pl.pallas_call(
        functools.partial(_in_proj_kernel, rope=rope),
        out_shape=[jax.ShapeDtypeStruct((B, L, w), BF16) for w in widths],
        grid=(L // tm, B),
        in_specs=in_specs,
        out_specs=[pl.BlockSpec((1, tm, w), lambda i, b: (b, i, 0)) for w in widths],
        compiler_params=_cparams(("arbitrary", "arbitrary")),
        name="in_proj_rope" if rope else "in_proj_ctx",
    )(*args)


def _mla_kernel(q_ref, k_ref, v_ref, o_ref):
    lane = _lane_iota((q_ref.shape[1], LANES))
    outs = []
    for h in range(MLA_HEADS):
        sl = slice(h * LANES, (h + 1) * LANES)
        s = _dot_nt(q_ref[0, :, sl], k_ref[0, :, sl])
        p = jnp.exp2(s - jnp.max(s, axis=-1, keepdims=True))
        o = _dot(p.astype(BF16), v_ref[0, :, sl])
        one = MLA_ONE_LANE[h % 2]
        outs.append(o * (1.0 / o[:, one:one + 1]))
    for g in range(MLA_HEADS // 2):
        o_ref[0, :, g * LANES:(g + 1) * LANES] = jnp.where(
            lane < 64, outs[2 * g], outs[2 * g + 1]).astype(o_ref.dtype)


def _mla_attention(qm, km, vm, tq):
    B, L, _ = qm.shape
    Lk = km.shape[1]
    tq = min(tq, L)
    return pl.pallas_call(
        _mla_kernel,
        out_shape=jax.ShapeDtypeStruct((B, L, MLA_HEADS * MLA_V), BF16),
        grid=(B, L // tq),
        in_specs=[pl.BlockSpec((1, tq, 512), lambda b, i: (b, i, 0)),
                  pl.BlockSpec((1, Lk, 512), lambda b, i: (b, 0, 0)),
                  pl.BlockSpec((1, Lk, 512), lambda b, i: (b, 0, 0))],
        out_specs=pl.BlockSpec((1, tq, MLA_HEADS * MLA_V), lambda b, i: (b, i, 0)),
        compiler_params=_cparams(("arbitrary", "arbitrary")),
        name="mla_attention",
    )(qm, km, vm)


def _swa_bias(n_ctx):
    W = SWA_BLOCK
    G = SWA_Q_HEADS // SWA_KV_HEADS
    qq = np.arange(G * W)[:, None] % W
    kk = np.arange(3 * W + n_ctx)[None, :]
    in_band = np.abs(kk - W - qq) <= W
    is_ctx = kk >= 3 * W
    first = is_ctx | (in_band & (kk >= W))
    last = is_ctx | (in_band & (kk < 2 * W))
    masks = np.stack([first, is_ctx | in_band, last])
    return jnp.asarray(np.where(masks, 0.0, NEG_INF), F32)


def _swa_kernel(sink_ref, q_ref, k_ref, v_ref, kc_ref, vc_ref, *rest, banded):
    W = SWA_BLOCK
    i = pl.program_id(1)
    nb = pl.num_programs(1)
    if banded:
        bias_ref, o_ref = rest
        prev = pl.multiple_of(jnp.maximum(i - 1, 0) * W, W)
        cur = pl.multiple_of(i * W, W)
        nxt = pl.multiple_of(jnp.minimum(i + 1, nb - 1) * W, W)
        k_all = jnp.concatenate([k_ref[0, pl.ds(prev, W), :], k_ref[0, pl.ds(cur, W), :],
                                 k_ref[0, pl.ds(nxt, W), :], kc_ref[0]], axis=0)
        v_all = jnp.concatenate([v_ref[0, pl.ds(prev, W), :], v_ref[0, pl.ds(cur, W), :],
                                 v_ref[0, pl.ds(nxt, W), :], vc_ref[0]], axis=0)
    else:
        (o_ref,) = rest
        k_all, v_all = kc_ref[0], vc_ref[0]
    tq = q_ref.shape[1]
    G = SWA_Q_HEADS // SWA_KV_HEADS
    row = lax.broadcasted_iota(jnp.int32, (G * tq, 1), 0)
    lo = _lane_iota((tq, LANES)) < 64
    for g in range(SWA_KV_HEADS):
        gs = slice(g * LANES, (g + 1) * LANES)
        k_g, v_g = k_all[:, gs], v_all[:, gs]
        pieces = []
        sink = jnp.zeros((G * tq, 1), F32)
        for jj in range(G // 2):
            j = g * (G // 2) + jj
            q_pair = q_ref[0, :, j * LANES:(j + 1) * LANES]
            zq = jnp.zeros_like(q_pair)
            pieces += [jnp.where(lo, q_pair, zq), jnp.where(lo, zq, q_pair)]
        for hh in range(G):
            sink = jnp.where(row // tq == hh, sink_ref[g * G + hh], sink)
        s = _dot_nt(jnp.concatenate(pieces, axis=0), k_g)
        if banded:
            s = s + bias_ref[0]
        m = jnp.maximum(jnp.max(s, axis=-1, keepdims=True), sink)
        p = jnp.exp2(s - m)
        l = jnp.sum(p, axis=-1, keepdims=True) + jnp.exp2(sink - m)
        o = _dot(p.astype(BF16), v_g) * (1.0 / l)
        for jj in range(G // 2):
            j = g * (G // 2) + jj
            o_ref[0, :, j * LANES:(j + 1) * LANES] = jnp.where(
                lo, o[2 * jj * tq:(2 * jj + 1) * tq], o[(2 * jj + 1) * tq:(2 * jj + 2) * tq]).astype(o_ref.dtype)


def _swa_attention(sink, q, k, v, kc, vc, banded):
    B, L, _ = q.shape
    Lc = kc.shape[1]
    tq = SWA_BLOCK if banded else L
    nb = L // tq
    Lkv = k.shape[1]
    in_specs = [pl.BlockSpec(memory_space=pltpu.SMEM),
                pl.BlockSpec((1, tq, 512), lambda b, i: (b, i, 0)),
                pl.BlockSpec((1, Lkv, 256), lambda b, i: (b, 0, 0)),
                pl.BlockSpec((1, Lkv, 256), lambda b, i: (b, 0, 0)),
                pl.BlockSpec((1, Lc, 256), lambda b, i: (b, 0, 0)),
                pl.BlockSpec((1, Lc, 256), lambda b, i: (b, 0, 0))]
    args = [sink, q, k, v, kc, vc]
    if banded:
        assert nb >= 2, "band masks assume distinct first and last query blocks"
        bias = _swa_bias(Lc)
        in_specs.append(pl.BlockSpec((1,) + bias.shape[1:],
                                     lambda b, i: (jnp.where(i == 0, 0, jnp.where(i == nb - 1, 2, 1)), 0, 0)))
        args.append(bias)
    return pl.pallas_call(
        functools.partial(_swa_kernel, banded=banded),
        out_shape=jax.ShapeDtypeStruct((B, L, 512), BF16),
        grid=(B, nb),
        in_specs=in_specs,
        out_specs=pl.BlockSpec((1, tq, 512), lambda b, i: (b, i, 0)),
        compiler_params=_cparams(("arbitrary", "arbitrary")),
        name="swa_banded" if banded else "swa_context",
    )(*args)


def _ret_tables(decay_f, decay_b):
    C = RET_CHUNK
    lg_f = jnp.log(jax.nn.sigmoid(decay_f.astype(F32)))
    lg_b = jnp.log(jax.nn.sigmoid(decay_b.astype(F32)))
    idx = jnp.arange(C, dtype=F32)
    diff = idx[:, None] - idx[None, :]
    intra = (jnp.where(diff >= 0, jnp.exp(lg_f[:, None, None] * jnp.maximum(diff, 0.0)), 0.0)
             + jnp.where(diff <= 0, jnp.exp(lg_b[:, None, None] * jnp.maximum(-diff, 0.0)), 0.0))
    lanes = lambda t: jnp.repeat(t.T, RET_DK, axis=1)
    qdf = lanes(jnp.exp(lg_f[:, None] * (idx + 1.0)))
    qdb = lanes(jnp.exp(lg_b[:, None] * (C - idx)))
    kdf = lanes(jnp.exp(lg_f[:, None] * (C - 1.0 - idx)))
    kdb = lanes(jnp.exp(lg_b[:, None] * idx))
    cdf = jnp.repeat(jnp.exp(lg_f * C), RET_DV).reshape(1, -1)
    cdb = jnp.repeat(jnp.exp(lg_b * C), RET_DV).reshape(1, -1)
    return intra, qdf, qdb, kdf, kdb, cdf, cdb


def _ret_kernel(q_ref, k_ref, v_ref, g_ref, qx_ref, kx_ref, vx_ref, gx_ref,
                d_ref, qdf_ref, qdb_ref, kdf_ref, kdb_ref, cdf_ref, cdb_ref,
                *rest, ctx_out):
    if ctx_out:
        o_ref, ox_ref, sf, sb, sfx, sbx = rest
    else:
        o_ref, sf, sb, sfx, sbx = rest
        ox_ref = None
    C = RET_CHUNK
    nc = q_ref.shape[1] // C
    ncx = qx_ref.shape[1] // C
    NG = RET_HEADS // 2
    r = lax.broadcasted_iota(jnp.int32, (LANES, LANES), 0)
    cidx = lax.broadcasted_iota(jnp.int32, (LANES, LANES), 1)
    blockdiag = (r // 64) == (cidx // 64)
    lane = _lane_iota((C, LANES))
    lo = lane < 64

    def kv_sum(kr, vr, c0, kd_ref, j):
        gs = slice(j * LANES, (j + 1) * LANES)
        kd = (kr[0, pl.ds(c0, C), gs].astype(F32) * kd_ref[:, gs]).astype(BF16)
        return jnp.where(blockdiag, _dot_tn(kd, vr[0, pl.ds(c0, C), gs]), 0.0)

    def state_pass(kr, vr, st, n, kd_ref, cd_ref, init, reverse):
        def body(t, carry):
            c = (n - 1 - t) if reverse else t
            c0 = pl.multiple_of(c * C, C)
            new = []
            for j in range(NG):
                st[c, j] = carry[j]
                new.append(carry[j] * cd_ref[:, j * LANES:(j + 1) * LANES] + kv_sum(kr, vr, c0, kd_ref, j))
            return tuple(new)
        return lax.fori_loop(0, n, body, init)

    zero = tuple(jnp.zeros((LANES, LANES), F32) for _ in range(NG))
    s_f = state_pass(kx_ref, vx_ref, sfx, ncx, kdf_ref, cdf_ref, zero, False)
    state_pass(k_ref, v_ref, sf, nc, kdf_ref, cdf_ref, s_f, False)
    s_b = state_pass(kx_ref, vx_ref, sbx, ncx, kdb_ref, cdb_ref, zero, True)
    state_pass(k_ref, v_ref, sb, nc, kdb_ref, cdb_ref, s_b, True)

    def out_pass(qr, kr, vr, gr, orf, stf, stb, n):
        def body(c, _):
            c0 = pl.multiple_of(c * C, C)
            for j in range(NG):
                gs = slice(j * LANES, (j + 1) * LANES)
                qg, kg, vg = qr[0, pl.ds(c0, C), gs], kr[0, pl.ds(c0, C), gs], vr[0, pl.ds(c0, C), gs]
                halves = []
                for half in range(2):
                    zq = jnp.zeros_like(qg)
                    qh = jnp.where(lo, qg, zq) if half == 0 else jnp.where(lo, zq, qg)
                    att = _dot_nt(qh, kg) * d_ref[2 * j + half]
                    halves.append(_dot(att.astype(BF16), vg))
                o = jnp.where(lo, halves[0], halves[1])
                qf = qg.astype(F32)
                qd = jnp.concatenate([(qf * qdf_ref[:, gs]).astype(BF16),
                                      (qf * qdb_ref[:, gs]).astype(BF16)], axis=1)
                s_cat = jnp.concatenate([stf[c, j], stb[c, j]], axis=0).astype(BF16)
                o = o + _dot(qd, s_cat)
                o2 = o * o
                ms = jnp.where(lo, jnp.sum(jnp.where(lo, o2, 0.0), axis=-1, keepdims=True),
                               jnp.sum(jnp.where(lo, 0.0, o2), axis=-1, keepdims=True)) * (1.0 / RET_DV)
                gate = gr[0, pl.ds(c0, C), gs].astype(F32)
                y = o * lax.rsqrt(ms + NORM_EPS) * (gate * jax.nn.sigmoid(gate))
                orf[0, pl.ds(c0, C), gs] = y.astype(orf.dtype)
            return 0
        lax.fori_loop(0, n, body, 0)

    out_pass(q_ref, k_ref, v_ref, g_ref, o_ref, sf, sb, nc)
    if ctx_out:
        out_pass(qx_ref, kx_ref, vx_ref, gx_ref, ox_ref, sfx, sbx, ncx)


def _retention(q, k, v, g, qx, kx, vx, gx, tabs, ctx_out):
    B, L, W = q.shape
    Lc = qx.shape[1]
    nc, ncx = L // RET_CHUNK, Lc // RET_CHUNK
    lat = pl.BlockSpec((1, L, W), lambda b: (b, 0, 0))
    cx = pl.BlockSpec((1, Lc, W), lambda b: (b, 0, 0))
    full = lambda a: pl.BlockSpec(a.shape, lambda b: (0,) * a.ndim)
    out_shape = [jax.ShapeDtypeStruct((B, L, W), BF16)]
    out_specs = [lat]
    if ctx_out:
        out_shape.append(jax.ShapeDtypeStruct((B, Lc, W), BF16))
        out_specs.append(cx)
    res = pl.pallas_call(
        functools.partial(_ret_kernel, ctx_out=ctx_out),
        out_shape=out_shape,
        grid=(B,),
        in_specs=[lat, lat, lat, lat, cx, cx, cx, cx] + [full(t) for t in tabs],
        out_specs=out_specs,
        scratch_shapes=[pltpu.VMEM((nc, 2, LANES, LANES), F32), pltpu.VMEM((nc, 2, LANES, LANES), F32),
                        pltpu.VMEM((ncx, 2, LANES, LANES), F32), pltpu.VMEM((ncx, 2, LANES, LANES), F32)],
        compiler_params=_cparams(("arbitrary",)),
        name="retention",
    )(q, k, v, g, qx, kx, vx, gx, *tabs)
    return (res[0], res[1]) if ctx_out else (res[0], None)


def _split_hi_lo(a):
    hi = a.astype(BF16)
    return hi, (a - hi.astype(F32)).astype(BF16)


def _out_proj_kernel(oa_ref, ob_ref, oc_ref, x_ref, g1_ref, a2_ref, sh2_ref, wo_ref, *rest, route):
    if route:
        r_ref, x1_ref, h2_ref, rt_ref = rest
    else:
        x1_ref, h2_ref = rest
    y = (_dot(oa_ref[0], wo_ref[0:256, :]) + _dot(ob_ref[0], wo_ref[256:768, :])
         + _dot(oc_ref[0], wo_ref[768:1024, :]))
    x1 = x_ref[0] + g1_ref[0] * y
    x1_ref[0] = x1
    h2 = x1 * lax.rsqrt(jnp.mean(x1 * x1, axis=-1, keepdims=True) + NORM_EPS) * a2_ref[0] + sh2_ref[0]
    h2_ref[0] = h2.astype(h2_ref.dtype)
    if route:
        tm = h2.shape[0]
        h_hi, h_lo = _split_hi_lo(h2)
        prod = _dot(jnp.concatenate([h_hi, h_lo], axis=0), r_ref[...])
        logits = prod[:tm] + pltpu.roll(prod[:tm], LANES - N_EXPERTS, 1) + prod[tm:]
        lane = _lane_iota(logits.shape)
        logits = jnp.where(lane < N_EXPERTS, logits, -jnp.inf)
        v1 = jnp.max(logits, axis=-1, keepdims=True)
        i1 = jnp.min(jnp.where(logits == v1, lane, LANES), axis=-1, keepdims=True)
        rest_l = jnp.where(lane == i1, -jnp.inf, logits)
        v2 = jnp.max(rest_l, axis=-1, keepdims=True)
        i2 = jnp.min(jnp.where(rest_l == v2, lane, LANES), axis=-1, keepdims=True)
        e2 = jnp.exp(v2 - v1)
        w1 = 1.0 / (1.0 + e2)
        w2 = e2 * w1
        rt_ref[0] = jnp.where(lane == 0, i1.astype(F32),
                              jnp.where(lane == 1, i2.astype(F32),
                                        jnp.where(lane == 2, w1, jnp.where(lane == 3, w2, 0.0))))


def _out_proj(oa, ob, oc, x, g1, a2, sh2, wo, router, tm):
    B, L, D = x.shape
    tm = min(tm, L)
    route = router is not None
    h2_dtype, h2_w = (F32 if route else BF16), D
    bm = (lambda b: b) if g1.shape[0] == B else (lambda b: 0)
    tok = lambda w: pl.BlockSpec((1, tm, w), lambda b, i: (b, i, 0))
    vec = pl.BlockSpec((1, 1, D), lambda b, i: (bm(b), 0, 0))
    in_specs = [tok(256), tok(512), tok(256), tok(D), vec, vec, vec,
                pl.BlockSpec(wo.shape, lambda b, i: (0, 0))]
    args = [oa, ob, oc, x, g1, a2, sh2, wo]
    out_shape = [jax.ShapeDtypeStruct((B, L, D), F32), jax.ShapeDtypeStruct((B, L, h2_w), h2_dtype)]
    out_specs = [tok(D), tok(h2_w)]
    if route:
        in_specs.append(pl.BlockSpec(router.shape, lambda b, i: (0, 0)))
        args.append(router)
        out_shape.append(jax.ShapeDtypeStruct((B, L, LANES), F32))
        out_specs.append(tok(LANES))
    return pl.pallas_call(
        functools.partial(_out_proj_kernel, route=route),
        out_shape=out_shape,
        grid=(B, L // tm),
        in_specs=in_specs,
        out_specs=out_specs,
        compiler_params=_cparams(("arbitrary", "arbitrary")),
        name="out_proj_route" if route else "out_proj",
    )(*args)


def _swiglu_act(g, u):
    h = 0.5 * g
    return ((h + h * jnp.tanh(h)) * u).astype(BF16)


def _swiglu_chunks(x, wg, wu, wd, bounds):
    acc = None
    for c0, c1 in bounds:
        part = _dot(_swiglu_act(_dot(x, wg[:, c0:c1]), _dot(x, wu[:, c0:c1])), wd[c0:c1, :])
        acc = part if acc is None else acc + part
    return acc


def _chunk_bounds(total, size):
    return tuple((c, min(c + size, total)) for c in range(0, total, size))


def _ffn_kernel(h_ref, x_ref, g2_ref, wg_ref, wu_ref, wd_ref, o_ref, *, bounds):
    o_ref[0] = x_ref[0] + g2_ref[0] * _swiglu_chunks(h_ref[0], wg_ref, wu_ref, wd_ref, bounds)


def _ffn(h2, x1, g2, wg, wu, wd, tm):
    B, L, D = x1.shape
    tm = min(tm, L)
    bm = (lambda b: b) if g2.shape[0] == B else (lambda b: 0)
    tok = pl.BlockSpec((1, tm, D), lambda b, i: (b, i, 0))
    wspec = lambda w: pl.BlockSpec(w.shape, lambda b, i: (0, 0), pipeline_mode=pl.Buffered(1))
    return pl.pallas_call(
        functools.partial(_ffn_kernel, bounds=_chunk_bounds(wg.shape[1], TF_SUB)),
        out_shape=jax.ShapeDtypeStruct((B, L, D), F32),
        grid=(B, L // tm),
        in_specs=[tok, tok, pl.BlockSpec((1, 1, D), lambda b, i: (bm(b), 0, 0)),
                  wspec(wg), wspec(wu), wspec(wd)],
        out_specs=tok,
        compiler_params=_cparams(("arbitrary", "arbitrary")),
        name="dense_ffn",
    )(h2, x1, g2, wg, wu, wd)


ROW_UNROLL = 8


def _dispatch_kernel(ends_ref, pad_ref, d0_ref, d1_ref, h_ref, zeros_ref, o_ref, sem, zsem):
    rows = h_ref.shape[0]
    tm = zeros_ref.shape[0]

    @pl.when(pl.program_id(0) == 0)
    def _():
        def clear(row0):
            return pltpu.make_async_copy(zeros_ref, o_ref.at[pl.ds(row0, tm)], zsem)
        jobs = [(pad_ref[e] > 0, ends_ref[e] - tm) for e in range(N_EXPERTS)]
        total = ends_ref[N_EXPERTS - 1]
        jobs += [(total + t * tm < o_ref.shape[0], total + t * tm) for t in range(N_EXPERTS)]
        for cond, row0 in jobs:
            @pl.when(cond)
            def _():
                clear(pl.multiple_of(row0, tm)).start()
        for cond, row0 in jobs:
            @pl.when(cond)
            def _():
                clear(pl.multiple_of(row0, tm)).wait()

    def start(r, _):
        pltpu.make_async_copy(h_ref.at[pl.ds(r, 1)], o_ref.at[pl.ds(d0_ref[0, 0, r], 1)], sem).start()
        pltpu.make_async_copy(h_ref.at[pl.ds(r, 1)], o_ref.at[pl.ds(d1_ref[0, 0, r], 1)], sem).start()
        return 0

    lax.fori_loop(0, rows, start, 0, unroll=ROW_UNROLL)
    for _ in range(2):
        pltpu.make_async_copy(h_ref, o_ref.at[pl.ds(0, rows)], sem).wait()


def _dispatch(h, d0, d1, ends, padding, n_slots, tm, td):
    N, D = h.shape
    idx_spec = pl.BlockSpec((1, 1, td), lambda i, *_: (i, 0, 0), memory_space=pltpu.SMEM)
    return pl.pallas_call(
        _dispatch_kernel,
        out_shape=jax.ShapeDtypeStruct((n_slots, D), h.dtype),
        grid_spec=pltpu.PrefetchScalarGridSpec(
            num_scalar_prefetch=2,
            grid=(N // td,),
            in_specs=[idx_spec, idx_spec, pl.BlockSpec((td, D), lambda i, *_: (i, 0)),
                      pl.BlockSpec(memory_space=pl.ANY)],
            out_specs=pl.BlockSpec(memory_space=pl.ANY),
            scratch_shapes=[pltpu.SemaphoreType.DMA, pltpu.SemaphoreType.DMA]),
        compiler_params=_cparams(("arbitrary",)),
        name="moe_dispatch",
    )(ends, padding, d0.reshape(N // td, 1, td), d1.reshape(N // td, 1, td), h, jnp.zeros((tm, D), h.dtype))


def _gmm_kernel(te_ref, nu_ref, x_ref, wg_ref, wu_ref, wd_ref, o_ref, xb_ref, *, bounds):
    i, f = pl.program_id(0), pl.program_id(1)
    used = i < nu_ref[0]

    @pl.when(used & (f == 0))
    def _():
        xb_ref[...] = x_ref[...].astype(BF16)

    @pl.when(used)
    def _():
        part = _swiglu_chunks(xb_ref[...], wg_ref.at[0], wu_ref.at[0], wd_ref.at[0], bounds)

        @pl.when(f == 0)
        def _():
            o_ref[...] = part

        @pl.when(f > 0)
        def _():
            o_ref[...] += part

    @pl.when(jnp.logical_not(used) & (f == pl.num_programs(1) - 1))
    def _():
        o_ref[...] = jnp.zeros_like(o_ref)


def _grouped_swiglu(xs, tile_expert, n_used, wg, wu, wd, tm, tf):
    P, D = xs.shape
    E, _, F = wg.shape
    nf = F // tf
    n_tiles = P // tm

    def fsel(i, f, nu):
        return jnp.where(i < nu[0], f, nf - 1)

    return pl.pallas_call(
        functools.partial(_gmm_kernel, bounds=_chunk_bounds(tf, TF_SUB)),
        out_shape=jax.ShapeDtypeStruct((P, D), F32),
        grid_spec=pltpu.PrefetchScalarGridSpec(
            num_scalar_prefetch=2,
            grid=(n_tiles, nf),
            in_specs=[pl.BlockSpec((tm, D), lambda i, f, te, nu: (jnp.minimum(i, nu[0] - 1), 0)),
                      pl.BlockSpec((1, D, tf), lambda i, f, te, nu: (te[i], 0, fsel(i, f, nu))),
                      pl.BlockSpec((1, D, tf), lambda i, f, te, nu: (te[i], 0, fsel(i, f, nu))),
                      pl.BlockSpec((1, tf, D), lambda i, f, te, nu: (te[i], fsel(i, f, nu), 0))],
            out_specs=pl.BlockSpec((tm, D), lambda i, f, te, nu: (i, 0)),
            scratch_shapes=[pltpu.VMEM((tm, D), BF16)]),
        compiler_params=_cparams(("arbitrary", "arbitrary")),
        name="moe_grouped_swiglu",
    )(tile_expert, n_used, xs, wg, wu, wd)


def _combine_kernel(d0_ref, d1_ref, y_ref, x_ref, g2_ref, rt_ref, fg_ref, o_ref, buf, sem):
    rows = o_ref.shape[1]

    def start(r, _):
        pltpu.make_async_copy(y_ref.at[pl.ds(d0_ref[0, 0, r], 1)], buf.at[0, pl.ds(r, 1)], sem).start()
        pltpu.make_async_copy(y_ref.at[pl.ds(d1_ref[0, 0, r], 1)], buf.at[1, pl.ds(r, 1)], sem).start()
        return 0

    lax.fori_loop(0, rows, start, 0, unroll=ROW_UNROLL)
    for k in range(2):
        pltpu.make_async_copy(y_ref.at[pl.ds(0, rows)], buf.at[k], sem).wait()
    rt = rt_ref[0]
    moe = rt[:, 2:3] * buf[0] + rt[:, 3:4] * buf[1]
    x2 = x_ref[0] + g2_ref[0] * moe
    o_ref[0] = x2 * lax.rsqrt(jnp.mean(x2 * x2, axis=-1, keepdims=True) + NORM_EPS) * fg_ref[...]


def _combine(y, d0, d1, x1, g2, rt, final_g, tc):
    B, L, D = x1.shape
    tc = min(tc, L)
    nt = L // tc
    idx_spec = pl.BlockSpec((1, 1, tc), lambda b, i: (b * nt + i, 0, 0), memory_space=pltpu.SMEM)
    tok = lambda w: pl.BlockSpec((1, tc, w), lambda b, i: (b, i, 0))
    return pl.pallas_call(
        _combine_kernel,
        out_shape=jax.ShapeDtypeStruct((B, L, D), F32),
        grid=(B, nt),
        in_specs=[idx_spec, idx_spec, pl.BlockSpec(memory_space=pl.ANY), tok(D),
                  pl.BlockSpec((1, 1, D), lambda b, i: (b, 0, 0)), tok(LANES),
                  pl.BlockSpec((1, D), lambda b, i: (0, 0))],
        out_specs=tok(D),
        scratch_shapes=[pltpu.VMEM((2, tc, D), F32), pltpu.SemaphoreType.DMA],
        compiler_params=_cparams(("arbitrary", "arbitrary")),
        name="moe_combine_norm",
    )(d0.reshape(B * nt, 1, tc), d1.reshape(B * nt, 1, tc), y, x1, g2, rt, final_g.reshape(1, D))


def _routing(rt, tm):
    n = rt.shape[0]
    e = rt[:, :2].astype(jnp.int32).reshape(-1)
    onehot = (e[:, None] == jnp.arange(N_EXPERTS)[None, :]).astype(jnp.int32)
    csum = jnp.cumsum(onehot, axis=0)
    rank = jnp.sum((csum - onehot) * onehot, axis=1)
    counts = csum[-1]
    padded = ((counts + tm - 1) // tm) * tm
    ends = jnp.cumsum(padded)
    offs = ends - padded
    dest = jnp.sum(onehot * offs[None, :], axis=1) + rank
    n_slots = 2 * n + N_EXPERTS * tm
    tile_start = jnp.arange(n_slots // tm, dtype=jnp.int32) * tm
    n_used = (ends[-1] // tm).astype(jnp.int32)
    te = jnp.sum((tile_start[:, None] >= ends[None, :]).astype(jnp.int32), axis=1)
    last = jnp.sum((ends[-1] - tm >= ends).astype(jnp.int32))
    te = jnp.where(tile_start < ends[-1], te, last).astype(jnp.int32)
    d = dest.reshape(n, 2)
    return (n_slots, te, n_used.reshape(1), d[:, 0], d[:, 1], ends.astype(jnp.int32),
            (padded - counts).astype(jnp.int32))


TM_IN = 256
TQ_MLA = 256
TM_OUT = 512
TM_FFN = 512
TM_MOE = 1024
TF_MOE = 896
TF_SUB = 512
TD_DISPATCH = 512
TC_COMBINE = 256


def kernel(x, c, ctx, c_ctx, w_mod, b_mod, norm1_g, norm2_g, w_in, mla_q_norm, mla_w_uq, mla_kv_norm,
           mla_w_ukv, swa_sink, ret_decay_fwd, ret_decay_bwd, w_out, ffn_w_gate, ffn_w_up, ffn_w_down,
           moe_router, moe_w_gate, moe_w_up, moe_w_down, final_norm_g):
    B, L, D = x.shape
    Lc = ctx.shape[1]
    depth = w_mod.shape[0]
    xc = ctx

    cond = jnp.concatenate([c, c_ctx[None], jnp.zeros((16 - B - 1, D), F32)], axis=0)
    mod_all = _modulation(cond, w_mod, b_mod)
    tables = _rope_tables(L)

    for layer in range(depth):
        last = layer == depth - 1
        mod = mod_all[layer].reshape(16, 6, 1, D)
        sh1, sc1, g1, sh2, sc2, g2 = (mod[:B, j] for j in range(6))
        sh1x, sc1x, g1x, sh2x, sc2x, g2x = (mod[B:B + 1, j] for j in range(6))
        n1, n2 = norm1_g[layer], norm2_g[layer]

        wts = _prep_in_weights(w_in[layer], mla_q_norm[layer], mla_w_uq[layer],
                               mla_kv_norm[layer], mla_w_ukv[layer])
        lat = _in_proj(x, n1 * (1.0 + sc1), sh1, wts, tables, TM_IN)
        cx = _in_proj(xc, n1 * (1.0 + sc1x), sh1x, wts, None, TM_IN)
        qm, km, vm, sq, sk, sv, rq, rk, rv, rg = lat
        qmx, kmx, vmx, sqx, skx, svx, rqx, rkx, rvx, rgx = cx

        o_a = _mla_attention(qm, jnp.concatenate([km, kmx], axis=1),
                             jnp.concatenate([vm, vmx], axis=1), TQ_MLA)
        sink = swa_sink[layer].astype(F32) * LOG2E
        o_b = _swa_attention(sink, sq, sk, sv, skx, svx, True)
        rtabs = _ret_tables(ret_decay_fwd[layer], ret_decay_bwd[layer])
        o_c, oc_c = _retention(rq, rk, rv, rg, rqx, rkx, rvx, rgx, rtabs, not last)
        wo = w_out[layer].astype(BF16)

        if layer % 2 == 0:
            i = layer // 2
            wg, wu, wd = (ffn_w_gate[i].astype(BF16), ffn_w_up[i].astype(BF16),
                          ffn_w_down[i].astype(BF16))
            x1, h2 = _out_proj(o_a, o_b, o_c, x, g1, n2 * (1.0 + sc2), sh2, wo, None, TM_OUT)
            x_next = _ffn(h2, x1, g2, wg, wu, wd, TM_FFN)
        else:
            i = layer // 2
            r_hi = moe_router[i].astype(BF16)
            r_lo = (moe_router[i] - r_hi.astype(F32)).astype(BF16)
            router = jnp.pad(jnp.concatenate([r_hi, r_lo], axis=1), ((0, 0), (0, LANES - 2 * N_EXPERTS)))
            x1, h2p, rt = _out_proj(o_a, o_b, o_c, x, g1, n2 * (1.0 + sc2), sh2, wo, router, TM_OUT)
            n_slots, te, n_used, d0, d1, ends, padding = _routing(rt.reshape(B * L, LANES), TM_MOE)
            xs = _dispatch(h2p.reshape(B * L, D), d0, d1, ends, padding, n_slots, TM_MOE, TD_DISPATCH)
            y = _grouped_swiglu(xs, te, n_used, moe_w_gate[i].astype(BF16), moe_w_up[i].astype(BF16),
                                moe_w_down[i].astype(BF16), TM_MOE, TF_MOE)
            if last:
                return _combine(y, d0, d1, x1, g2, rt, final_norm_g, TC_COMBINE)
            raise NotImplementedError("expert layer is only supported as the last layer")

        if not last:
            oc_a = _mla_attention(qmx, kmx, vmx, TQ_MLA)
            oc_b = _swa_attention(sink, sqx, skx, svx, skx, svx, False)
            xc1, hc2 = _out_proj(oc_a, oc_b, oc_c, xc, g1x, n2 * (1.0 + sc2x), sh2x, wo, None, TM_OUT)
            xc = _ffn(hc2, xc1, g2x, wg, wu, wd, TM_FFN)
        x = x_next
    raise NotImplementedError("trunk must end with the expert layer")
```

```python
import functools
import math

import numpy as np
import jax
import jax.numpy as jnp
from jax import lax
from jax.experimental import pallas as pl
from jax.experimental.pallas import tpu as pltpu

F32 = jnp.float32
BF16 = jnp.bfloat16

D_MODEL = 1024
DEPTH = 2
GRID_W = 64
HEAD_DIM = 64
NORM_EPS = 1e-6
ROPE_BASE = 10000.0
NEG_INF = -1e30

MLA_HEADS = 4
MLA_Q_RANK = 192
MLA_KV_RANK = 128
MLA_NOPE = 64
MLA_ROPE = 32
MLA_V = 64

SWA_Q_HEADS = 8
SWA_KV_HEADS = 2
SWA_BLOCK = 128

RET_HEADS = 4
RET_DK = 64
RET_DV = 64
RET_CHUNK = 128

D_FF = 2816
N_EXPERTS = 8
D_FF_EXPERT = 3584

LOG2E = math.log2(math.e)
MLA_ONE_LANE = (64, 0)
LANES = 128
VMEM_LIMIT = 56 * 1024 * 1024

C_SQ, C_SK, C_SV = 0, 512, 768
C_RQ, C_RK, C_RV, C_RG = 1024, 1280, 1536, 1792
C_CKV, C_EXT = 2048, 2176
IN_COLS = 2432


def _cparams(sem, vmem=VMEM_LIMIT):
    return pltpu.CompilerParams(dimension_semantics=sem, vmem_limit_bytes=vmem)


def _dot(a, b):
    return jnp.dot(a, b, preferred_element_type=F32)


def _dot_nt(a, b):
    return lax.dot_general(a, b, (((1,), (1,)), ((), ())), preferred_element_type=F32)


def _dot_tn(a, b):
    return lax.dot_general(a, b, (((0,), (0,)), ((), ())), preferred_element_type=F32)


def _lane_iota(shape):
    return lax.broadcasted_iota(jnp.int32, shape, len(shape) - 1)


def _mod_kernel(c_ref, w_ref, b_ref, o_ref):
    c = c_ref[...]
    c = c * jax.nn.sigmoid(c)
    o_ref[0] = jnp.dot(c, w_ref[0], preferred_element_type=F32,
                       precision=lax.Precision.HIGHEST) + b_ref[0]


def _modulation(cond, w_mod, b_mod):
    depth, d, n = w_mod.shape
    rows = cond.shape[0]
    tn = 1024
    return pl.pallas_call(
        _mod_kernel,
        out_shape=jax.ShapeDtypeStruct((depth, rows, n), F32),
        grid=(depth, n // tn),
        in_specs=[pl.BlockSpec((rows, d), lambda l, j: (0, 0)),
                  pl.BlockSpec((1, d, tn), lambda l, j: (l, 0, j)),
                  pl.BlockSpec((1, 1, tn), lambda l, j: (l, 0, j))],
        out_specs=pl.BlockSpec((1, rows, tn), lambda l, j: (l, 0, j)),
        compiler_params=_cparams(("arbitrary", "arbitrary")),
        name="modulation",
    )(cond, w_mod, b_mod.reshape(depth, 1, n))


def _angles(pos, dim):
    inv = (ROPE_BASE ** (-np.arange(0, dim, 2, dtype=np.float32) / dim)).astype(np.float32)
    ang = pos.astype(np.float32)[:, None] * inv[None, :]
    return np.concatenate([ang, ang], axis=-1).astype(np.float64)


def _rope_tables(length):
    t = np.arange(length)
    rows, cols = t // GRID_W, t % GRID_W
    ar, ac = _angles(rows, 32), _angles(cols, 32)
    sign32 = np.concatenate([-np.ones(16), np.ones(16)])
    cos_a = np.concatenate([np.cos(ar), np.cos(ac)], axis=-1)
    sin_a = np.concatenate([np.sin(ar) * sign32, np.sin(ac) * sign32], axis=-1)
    cos_a, sin_a = np.tile(cos_a, (1, 2)), np.tile(sin_a, (1, 2))
    at = _angles(t, 64)
    sign64 = np.concatenate([-np.ones(32), np.ones(32)])
    cos_r, sin_r = np.tile(np.cos(at), (1, 2)), np.tile(np.sin(at) * sign64, (1, 2))
    mr, mc = _angles(rows, 16), _angles(cols, 16)
    cos_m = np.ones((length, LANES))
    sin_m = np.zeros((length, LANES))
    cos_m[:, 64:96] = np.concatenate([np.cos(mr), np.cos(mc)], axis=-1)
    sin_m[:, 64:96] = np.concatenate([np.sin(mr), np.sin(mc)], axis=-1)
    return tuple(jnp.asarray(a, F32) for a in (cos_a, sin_a, cos_r, sin_r, cos_m, sin_m))


def _prep_in_weights(w_in, q_norm, w_uq, kv_norm, w_ukv):
    cuts = np.cumsum([MLA_Q_RANK, MLA_KV_RANK, MLA_ROPE, 512, 128, 128, 256, 256, 256, 256])[:-1]
    cq, ckv, kpe, sq, sk, sv, rq, rk, rv, rg = jnp.split(w_in, [int(v) for v in cuts], axis=1)
    dup = lambda w: jnp.concatenate([w[:, :64], w[:, :64], w[:, 64:], w[:, 64:]], axis=1)
    d = w_in.shape[0]
    w_main = jnp.concatenate(
        [sq * (HEAD_DIM ** -0.5 * LOG2E), dup(sk), dup(sv), rq, rk * RET_DK ** -0.5, rv, rg, ckv,
         cq, kpe, jnp.zeros((d, 32), F32)], axis=1).astype(BF16)

    scale = (MLA_NOPE + MLA_ROPE) ** -0.5 * LOG2E
    wq = (w_uq * scale).reshape(MLA_Q_RANK, MLA_HEADS, MLA_NOPE + MLA_ROPE)
    wq = jnp.pad(wq, ((0, 64), (0, 0), (0, 32))).reshape(256, 512)
    place = np.zeros((256, 512), np.float32)
    for h in range(MLA_HEADS):
        for dd in range(MLA_ROPE):
            place[MLA_Q_RANK + dd, h * LANES + MLA_NOPE + dd] = 1.0
    wz = jnp.concatenate([wq, jnp.asarray(place)], axis=1)
    perm = np.zeros((1024, 1024), np.float32)
    for g in range(8):
        for dd in range(MLA_ROPE):
            e = dd % 16
            src = dd + 8 if e < 8 else dd - 8
            perm[g * LANES + MLA_NOPE + src, g * LANES + MLA_NOPE + dd] = -1.0 if e < 8 else 1.0
    wz_rot = wz @ jnp.asarray(perm)
    qn_ext = jnp.pad(q_norm, (0, 64)).reshape(1, 256)

    wkv = w_ukv.reshape(MLA_KV_RANK, MLA_HEADS, MLA_NOPE + MLA_V)
    kn = jnp.pad(wkv[:, :, :MLA_NOPE], ((0, 0), (0, 0), (0, 64))).reshape(MLA_KV_RANK, 512)
    vals = wkv[:, :, MLA_NOPE:]
    vv = jnp.stack([jnp.pad(vals[:, h], ((0, 0), (64, 0) if h % 2 else (0, 64))) for h in range(MLA_HEADS)],
                   axis=1).reshape(MLA_KV_RANK, 512)
    w_kv = jnp.concatenate([kn, vv], axis=1)
    return (w_main, wz.astype(BF16), wz_rot.astype(BF16), qn_ext, w_kv.astype(BF16),
            kv_norm.reshape(1, MLA_KV_RANK))


def _rope_roll(x, cos, sin_signed, half):
    lane = _lane_iota(x.shape)
    rot = jnp.where((lane % (2 * half)) < half,
                    pltpu.roll(x, LANES - half, 1), pltpu.roll(x, half, 1))
    return x * cos + rot * sin_signed


def _in_proj_kernel(*refs, rope):
    if rope:
        (x_ref, a_ref, sh_ref, w_ref, wz_ref, wzr_ref, qn_ref, wkv_ref, kvn_ref,
         ca_ref, sa_ref, cr_ref, sr_ref, cm_ref, sm_ref, *outs) = refs
    else:
        (x_ref, a_ref, sh_ref, w_ref, wz_ref, wzr_ref, qn_ref, wkv_ref, kvn_ref, *outs) = refs
    qm_ref, km_ref, vm_ref, sq_ref, sk_ref, sv_ref, rq_ref, rk_ref, rv_ref, rg_ref = outs

    x = x_ref[0]
    h = x * lax.rsqrt(jnp.mean(x * x, axis=-1, keepdims=True) + NORM_EPS) * a_ref[0] + sh_ref[0]
    p = _dot(h.astype(BF16), w_ref[...])

    def put(ref, col, width, tables=None, half=None):
        for g in range(width // LANES):
            blk = p[:, col + g * LANES: col + (g + 1) * LANES]
            if tables is not None:
                blk = _rope_roll(blk, tables[0][...], tables[1][...], half)
            ref[0, :, g * LANES:(g + 1) * LANES] = blk.astype(ref.dtype)

    axial = (ca_ref, sa_ref) if rope else None
    flat = (cr_ref, sr_ref) if rope else None
    put(sq_ref, C_SQ, 512, axial, 16)
    put(sk_ref, C_SK, 256, axial, 16)
    put(sv_ref, C_SV, 256)
    put(rq_ref, C_RQ, 256, flat, 32)
    put(rk_ref, C_RK, 256, flat, 32)
    put(rv_ref, C_RV, 256)
    put(rg_ref, C_RG, 256)

    ext = p[:, C_EXT:C_EXT + 256]
    lane = _lane_iota(ext.shape)
    is_cq = lane < MLA_Q_RANK
    cq_sq = jnp.where(is_cq, ext * ext, 0.0)
    inv = lax.rsqrt(jnp.sum(cq_sq, axis=-1, keepdims=True) * (1.0 / MLA_Q_RANK) + NORM_EPS)
    z = jnp.where(is_cq, ext * inv * qn_ref[...], ext).astype(BF16)
    zw = _dot(z, wz_ref[...])
    ckv = p[:, C_CKV:C_CKV + MLA_KV_RANK]
    ckv = ckv * lax.rsqrt(jnp.mean(ckv * ckv, axis=-1, keepdims=True) + NORM_EPS) * kvn_ref[...]
    kv = _dot(ckv.astype(BF16), wkv_ref[...])
    if rope:
        zr = _dot(z, wzr_ref[...])
    lane_g = _lane_iota((ext.shape[0], LANES))
    for g in range(MLA_HEADS):
        sl = slice(g * LANES, (g + 1) * LANES)
        sk_ = slice(512 + g * LANES, 512 + (g + 1) * LANES)
        q_g, kpe_g = zw[:, sl], zw[:, sk_]
        if rope:
            q_g = q_g * cm_ref[...] + zr[:, sl] * sm_ref[...]
            kpe_g = kpe_g * cm_ref[...] + zr[:, sk_] * sm_ref[...]
        qm_ref[0, :, sl] = q_g.astype(BF16)
        km_ref[0, :, sl] = (kv[:, sl] + kpe_g).astype(BF16)
        vm_ref[0, :, sl] = jnp.where(lane_g == MLA_ONE_LANE[g % 2], 1.0, kv[:, sk_]).astype(BF16)


def _in_proj(x, a, sh, wts, tables, tm):
    B, L, D = x.shape
    w_main, wz, wzr, qn_ext, w_kv, kvn = wts
    rope = tables is not None
    tm = min(tm, L)
    bm = (lambda b: b) if a.shape[0] == B else (lambda b: 0)
    const = lambda i, b: (0, 0)
    in_specs = [pl.BlockSpec((1, tm, D), lambda i, b: (b, i, 0)),
                pl.BlockSpec((1, 1, D), lambda i, b: (bm(b), 0, 0)),
                pl.BlockSpec((1, 1, D), lambda i, b: (bm(b), 0, 0)),
                pl.BlockSpec(w_main.shape, const), pl.BlockSpec(wz.shape, const),
                pl.BlockSpec(wzr.shape, const), pl.BlockSpec(qn_ext.shape, const),
                pl.BlockSpec(w_kv.shape, const), pl.BlockSpec(kvn.shape, const)]
    args = [x, a, sh, w_main, wz, wzr, qn_ext, w_kv, kvn]
    if rope:
        in_specs += [pl.BlockSpec((tm, LANES), lambda i, b: (i, 0))] * 6
        args += list(tables)
    widths = (512, 512, 512, 512, 256, 256, 256, 256, 256, 256)
    return pl.pallas_call(
        functools.partial(_in_proj_kernel, rope=rope),
        out_shape=[jax.ShapeDtypeStruct((B, L, w), BF16) for w in widths],
        grid=(L // tm, B),
        in_specs=in_specs,
        out_specs=[pl.BlockSpec((1, tm, w), lambda i, b: (b, i, 0)) for w in widths],
        compiler_params=_cparams(("arbitrary", "arbitrary")),
        name="in_proj_rope" if rope else "in_proj_ctx",
    )(*args)


def _mla_kernel(q_ref, k_ref, v_ref, o_ref):
    lane = _lane_iota((q_ref.shape[1], LANES))
    outs = []
    for h in range(MLA_HEADS):
        sl = slice(h * LANES, (h + 1) * LANES)
        s = _dot_nt(q_ref[0, :, sl], k_ref[0, :, sl])
        p = jnp.exp2(s - jnp.max(s, axis=-1, keepdims=True))
        o = _dot(p.astype(BF16), v_ref[0, :, sl])
        one = MLA_ONE_LANE[h % 2]
        outs.append(o * (1.0 / o[:, one:one + 1]))
    for g in range(MLA_HEADS // 2):
        o_ref[0, :, g * LANES:(g + 1) * LANES] = jnp.where(
            lane < 64, outs[2 * g], outs[2 * g + 1]).astype(o_ref.dtype)


def _mla_attention(qm, km, vm, tq):
    B, L, _ = qm.shape
    Lk = km.shape[1]
    tq = min(tq, L)
    return pl.pallas_call(
        _mla_kernel,
        out_shape=jax.ShapeDtypeStruct((B, L, MLA_HEADS * MLA_V), BF16),
        grid=(B, L // tq),
        in_specs=[pl.BlockSpec((1, tq, 512), lambda b, i: (b, i, 0)),
                  pl.BlockSpec((1, Lk, 512), lambda b, i: (b, 0, 0)),
                  pl.BlockSpec((1, Lk, 512), lambda b, i: (b, 0, 0))],
        out_specs=pl.BlockSpec((1, tq, MLA_HEADS * MLA_V), lambda b, i: (b, i, 0)),
        compiler_params=_cparams(("arbitrary", "arbitrary")),
        name="mla_attention",
    )(qm, km, vm)


def _swa_bias(n_ctx):
    W = SWA_BLOCK
    G = SWA_Q_HEADS // SWA_KV_HEADS
    qq = np.arange(G * W)[:, None] % W
    kk = np.arange(3 * W + n_ctx)[None, :]
    in_band = np.abs(kk - W - qq) <= W
    is_ctx = kk >= 3 * W
    first = is_ctx | (in_band & (kk >= W))
    last = is_ctx | (in_band & (kk < 2 * W))
    masks = np.stack([first, is_ctx | in_band, last])
    return jnp.asarray(np.where(masks, 0.0, NEG_INF), F32)


def _swa_kernel(sink_ref, q_ref, k_ref, v_ref, kc_ref, vc_ref, *rest, banded):
    W = SWA_BLOCK
    i = pl.program_id(1)
    nb = pl.num_programs(1)
    if banded:
        bias_ref, o_ref = rest
        prev = pl.multiple_of(jnp.maximum(i - 1, 0) * W, W)
        cur = pl.multiple_of(i * W, W)
        nxt = pl.multiple_of(jnp.minimum(i + 1, nb - 1) * W, W)
        k_all = jnp.concatenate([k_ref[0, pl.ds(prev, W), :], k_ref[0, pl.ds(cur, W), :],
                                 k_ref[0, pl.ds(nxt, W), :], kc_ref[0]], axis=0)
        v_all = jnp.concatenate([v_ref[0, pl.ds(prev, W), :], v_ref[0, pl.ds(cur, W), :],
                                 v_ref[0, pl.ds(nxt, W), :], vc_ref[0]], axis=0)
    else:
        (o_ref,) = rest
        k_all, v_all = kc_ref[0], vc_ref[0]
    tq = q_ref.shape[1]
    G = SWA_Q_HEADS // SWA_KV_HEADS
    row = lax.broadcasted_iota(jnp.int32, (G * tq, 1), 0)
    lo = _lane_iota((tq, LANES)) < 64
    for g in range(SWA_KV_HEADS):
        gs = slice(g * LANES, (g + 1) * LANES)
        k_g, v_g = k_all[:, gs], v_all[:, gs]
        pieces = []
        sink = jnp.zeros((G * tq, 1), F32)
        for jj in range(G // 2):
            j = g * (G // 2) + jj
            q_pair = q_ref[0, :, j * LANES:(j + 1) * LANES]
            zq = jnp.zeros_like(q_pair)
            pieces += [jnp.where(lo, q_pair, zq), jnp.where(lo, zq, q_pair)]
        for hh in range(G):
            sink = jnp.where(row // tq == hh, sink_ref[g * G + hh], sink)
        s = _dot_nt(jnp.concatenate(pieces, axis=0), k_g)
        if banded:
            s = s + bias_ref[0]
        m = jnp.maximum(jnp.max(s, axis=-1, keepdims=True), sink)
        p = jnp.exp2(s - m)
        l = jnp.sum(p, axis=-1, keepdims=True) + jnp.exp2(sink - m)
        o = _dot(p.astype(BF16), v_g) * (1.0 / l)
        for jj in range(G // 2):
            j = g * (G // 2) + jj
            o_ref[0, :, j * LANES:(j + 1) * LANES] = jnp.where(
                lo, o[2 * jj * tq:(2 * jj + 1) * tq], o[(2 * jj + 1) * tq:(2 * jj + 2) * tq]).astype(o_ref.dtype)


def _swa_attention(sink, q, k, v, kc, vc, banded):
    B, L, _ = q.shape
    Lc = kc.shape[1]
    tq = SWA_BLOCK if banded else L
    nb = L // tq
    Lkv = k.shape[1]
    in_specs = [pl.BlockSpec(memory_space=pltpu.SMEM),
                pl.BlockSpec((1, tq, 512), lambda b, i: (b, i, 0)),
                pl.BlockSpec((1, Lkv, 256), lambda b, i: (b, 0, 0)),
                pl.BlockSpec((1, Lkv, 256), lambda b, i: (b, 0, 0)),
                pl.BlockSpec((1, Lc, 256), lambda b, i: (b, 0, 0)),
                pl.BlockSpec((1, Lc, 256), lambda b, i: (b, 0, 0))]
    args = [sink, q, k, v, kc, vc]
    if banded:
        assert nb >= 2, "band masks assume distinct first and last query blocks"
        bias = _swa_bias(Lc)
        in_specs.append(pl.BlockSpec((1,) + bias.shape[1:],
                                     lambda b, i: (jnp.where(i == 0, 0, jnp.where(i == nb - 1, 2, 1)), 0, 0)))
        args.append(bias)
    return pl.pallas_call(
        functools.partial(_swa_kernel, banded=banded),
        out_shape=jax.ShapeDtypeStruct((B, L, 512), BF16),
        grid=(B, nb),
        in_specs=in_specs,
        out_specs=pl.BlockSpec((1, tq, 512), lambda b, i: (b, i, 0)),
        compiler_params=_cparams(("arbitrary", "arbitrary")),
        name="swa_banded" if banded else "swa_context",
    )(*args)


def _ret_tables(decay_f, decay_b):
    C = RET_CHUNK
    lg_f = jnp.log(jax.nn.sigmoid(decay_f.astype(F32)))
    lg_b = jnp.log(jax.nn.sigmoid(decay_b.astype(F32)))
    idx = jnp.arange(C, dtype=F32)
    diff = idx[:, None] - idx[None, :]
    intra = (jnp.where(diff >= 0, jnp.exp(lg_f[:, None, None] * jnp.maximum(diff, 0.0)), 0.0)
             + jnp.where(diff <= 0, jnp.exp(lg_b[:, None, None] * jnp.maximum(-diff, 0.0)), 0.0))
    lanes = lambda t: jnp.repeat(t.T, RET_DK, axis=1)
    qdf = lanes(jnp.exp(lg_f[:, None] * (idx + 1.0)))
    qdb = lanes(jnp.exp(lg_b[:, None] * (C - idx)))
    kdf = lanes(jnp.exp(lg_f[:, None] * (C - 1.0 - idx)))
    kdb = lanes(jnp.exp(lg_b[:, None] * idx))
    cdf = jnp.repeat(jnp.exp(lg_f * C), RET_DV).reshape(1, -1)
    cdb = jnp.repeat(jnp.exp(lg_b * C), RET_DV).reshape(1, -1)
    return intra, qdf, qdb, kdf, kdb, cdf, cdb


def _ret_kernel(q_ref, k_ref, v_ref, g_ref, qx_ref, kx_ref, vx_ref, gx_ref,
                d_ref, qdf_ref, qdb_ref, kdf_ref, kdb_ref, cdf_ref, cdb_ref,
                *rest, ctx_out):
    if ctx_out:
        o_ref, ox_ref, sf, sb, sfx, sbx = rest
    else:
        o_ref, sf, sb, sfx, sbx = rest
        ox_ref = None
    C = RET_CHUNK
    nc = q_ref.shape[1] // C
    ncx = qx_ref.shape[1] // C
    NG = RET_HEADS // 2
    r = lax.broadcasted_iota(jnp.int32, (LANES, LANES), 0)
    cidx = lax.broadcasted_iota(jnp.int32, (LANES, LANES), 1)
    blockdiag = (r // 64) == (cidx // 64)
    lane = _lane_iota((C, LANES))
    lo = lane < 64

    def kv_sum(kr, vr, c0, kd_ref, j):
        gs = slice(j * LANES, (j + 1) * LANES)
        kd = (kr[0, pl.ds(c0, C), gs].astype(F32) * kd_ref[:, gs]).astype(BF16)
        return jnp.where(blockdiag, _dot_tn(kd, vr[0, pl.ds(c0, C), gs]), 0.0)

    def state_pass(kr, vr, stf, stb, n):
        def body(t, _):
            cf, cb = t, n - 1 - t
            f0, b0 = pl.multiple_of(cf * C, C), pl.multiple_of(cb * C, C)
            for j in range(NG):
                gs = slice(j * LANES, (j + 1) * LANES)
                stf[cf + 1, j] = stf[cf, j] * cdf_ref[:, gs] + kv_sum(kr, vr, f0, kdf_ref, j)
                stb[cb, j] = stb[cb + 1, j] * cdb_ref[:, gs] + kv_sum(kr, vr, b0, kdb_ref, j)
            return 0
        lax.fori_loop(0, n, body, 0, unroll=2)

    zero = jnp.zeros((NG, LANES, LANES), F32)
    sfx[0] = zero
    sbx[ncx] = zero
    state_pass(kx_ref, vx_ref, sfx, sbx, ncx)
    sf[0] = sfx[ncx]
    sb[nc] = sbx[0]
    state_pass(k_ref, v_ref, sf, sb, nc)

    def out_pass(qr, kr, vr, gr, orf, stf, stb, n):
        def body(c, _):
            c0 = pl.multiple_of(c * C, C)
            for j in range(NG):
                gs = slice(j * LANES, (j + 1) * LANES)
                qg, kg, vg = qr[0, pl.ds(c0, C), gs], kr[0, pl.ds(c0, C), gs], vr[0, pl.ds(c0, C), gs]
                halves = []
                for half in range(2):
                    zq = jnp.zeros_like(qg)
                    qh = jnp.where(lo, qg, zq) if half == 0 else jnp.where(lo, zq, qg)
                    att = _dot_nt(qh, kg) * d_ref[2 * j + half]
                    halves.append(_dot(att.astype(BF16), vg))
                o = jnp.where(lo, halves[0], halves[1])
                qf = qg.astype(F32)
                qd = jnp.concatenate([(qf * qdf_ref[:, gs]).astype(BF16),
                                      (qf * qdb_ref[:, gs]).astype(BF16)], axis=1)
                s_cat = jnp.concatenate([stf[c, j], stb[c + 1, j]], axis=0).astype(BF16)
                o = o + _dot(qd, s_cat)
                o2 = o * o
                ms = jnp.where(lo, jnp.sum(jnp.where(lo, o2, 0.0), axis=-1, keepdims=True),
                               jnp.sum(jnp.where(lo, 0.0, o2), axis=-1, keepdims=True)) * (1.0 / RET_DV)
                gate = gr[0, pl.ds(c0, C), gs].astype(F32)
                y = o * lax.rsqrt(ms + NORM_EPS) * (gate * jax.nn.sigmoid(gate))
                orf[0, pl.ds(c0, C), gs] = y.astype(orf.dtype)
            return 0
        lax.fori_loop(0, n, body, 0, unroll=2)

    out_pass(q_ref, k_ref, v_ref, g_ref, o_ref, sf, sb, nc)
    if ctx_out:
        out_pass(qx_ref, kx_ref, vx_ref, gx_ref, ox_ref, sfx, sbx, ncx)


def _retention(q, k, v, g, qx, kx, vx, gx, tabs, ctx_out):
    B, L, W = q.shape
    Lc = qx.shape[1]
    nc, ncx = L // RET_CHUNK, Lc // RET_CHUNK
    lat = pl.BlockSpec((1, L, W), lambda b: (b, 0, 0))
    cx = pl.BlockSpec((1, Lc, W), lambda b: (b, 0, 0))
    full = lambda a: pl.BlockSpec(a.shape, lambda b: (0,) * a.ndim)
    out_shape = [jax.ShapeDtypeStruct((B, L, W), BF16)]
    out_specs = [lat]
    if ctx_out:
        out_shape.append(jax.ShapeDtypeStruct((B, Lc, W), BF16))
        out_specs.append(cx)
    res = pl.pallas_call(
        functools.partial(_ret_kernel, ctx_out=ctx_out),
        out_shape=out_shape,
        grid=(B,),
        in_specs=[lat, lat, lat, lat, cx, cx, cx, cx] + [full(t) for t in tabs],
        out_specs=out_specs,
        scratch_shapes=[pltpu.VMEM((n + 1, RET_HEADS // 2, LANES, LANES), F32) for n in (nc, nc, ncx, ncx)],
        compiler_params=_cparams(("arbitrary",)),
        name="retention",
    )(q, k, v, g, qx, kx, vx, gx, *tabs)
    return (res[0], res[1]) if ctx_out else (res[0], None)


def _split_hi_lo(a):
    hi = a.astype(BF16)
    return hi, (a - hi.astype(F32)).astype(BF16)


def _out_proj_kernel(oa_ref, ob_ref, oc_ref, x_ref, g1_ref, a2_ref, sh2_ref, wo_ref, *rest, route):
    if route:
        r_ref, x1_ref, h2_ref, rt_ref = rest
    else:
        x1_ref, h2_ref = rest
    y = (_dot(oa_ref[0], wo_ref[0:256, :]) + _dot(ob_ref[0], wo_ref[256:768, :])
         + _dot(oc_ref[0], wo_ref[768:1024, :]))
    x1 = x_ref[0] + g1_ref[0] * y
    x1_ref[0] = x1
    h2 = x1 * lax.rsqrt(jnp.mean(x1 * x1, axis=-1, keepdims=True) + NORM_EPS) * a2_ref[0] + sh2_ref[0]
    h2_ref[0] = h2.astype(h2_ref.dtype)
    if route:
        tm = h2.shape[0]
        h_hi, h_lo = _split_hi_lo(h2)
        prod = _dot(jnp.concatenate([h_hi, h_lo], axis=0), r_ref[...])
        logits = prod[:tm] + pltpu.roll(prod[:tm], LANES - N_EXPERTS, 1) + prod[tm:]
        lane = _lane_iota(logits.shape)
        logits = jnp.where(lane < N_EXPERTS, logits, -jnp.inf)
        v1 = jnp.max(logits, axis=-1, keepdims=True)
        i1 = jnp.min(jnp.where(logits == v1, lane, LANES), axis=-1, keepdims=True)
        rest_l = jnp.where(lane == i1, -jnp.inf, logits)
        v2 = jnp.max(rest_l, axis=-1, keepdims=True)
        i2 = jnp.min(jnp.where(rest_l == v2, lane, LANES), axis=-1, keepdims=True)
        e2 = jnp.exp(v2 - v1)
        w1 = 1.0 / (1.0 + e2)
        w2 = e2 * w1
        rt_ref[0] = jnp.where(lane == 0, i1.astype(F32),
                              jnp.where(lane == 1, i2.astype(F32),
                                        jnp.where(lane == 2, w1, jnp.where(lane == 3, w2, 0.0))))


def _out_proj(oa, ob, oc, x, g1, a2, sh2, wo, router, tm):
    B, L, D = x.shape
    tm = min(tm, L)
    route = router is not None
    h2_dtype, h2_w = (F32 if route else BF16), D
    bm = (lambda b: b) if g1.shape[0] == B else (lambda b: 0)
    tok = lambda w: pl.BlockSpec((1, tm, w), lambda b, i: (b, i, 0))
    vec = pl.BlockSpec((1, 1, D), lambda b, i: (bm(b), 0, 0))
    in_specs = [tok(256), tok(512), tok(256), tok(D), vec, vec, vec,
                pl.BlockSpec(wo.shape, lambda b, i: (0, 0))]
    args = [oa, ob, oc, x, g1, a2, sh2, wo]
    out_shape = [jax.ShapeDtypeStruct((B, L, D), F32), jax.ShapeDtypeStruct((B, L, h2_w), h2_dtype)]
    out_specs = [tok(D), tok(h2_w)]
    if route:
        in_specs.append(pl.BlockSpec(router.shape, lambda b, i: (0, 0)))
        args.append(router)
        out_shape.append(jax.ShapeDtypeStruct((B, L, LANES), F32))
        out_specs.append(tok(LANES))
    return pl.pallas_call(
        functools.partial(_out_proj_kernel, route=route),
        out_shape=out_shape,
        grid=(B, L // tm),
        in_specs=in_specs,
        out_specs=out_specs,
        compiler_params=_cparams(("arbitrary", "arbitrary")),
        name="out_proj_route" if route else "out_proj",
    )(*args)


def _swiglu_act(g, u):
    h = 0.5 * g
    return ((h + h * jnp.tanh(h)) * u).astype(BF16)


def _swiglu_chunks(x, wg, wu, wd, bounds):
    acc = None
    for c0, c1 in bounds:
        part = _dot(_swiglu_act(_dot(x, wg[:, c0:c1]), _dot(x, wu[:, c0:c1])), wd[c0:c1, :])
        acc = part if acc is None else acc + part
    return acc


def _chunk_bounds(total, size):
    return tuple((c, min(c + size, total)) for c in range(0, total, size))


def _ffn_kernel(h_ref, x_ref, g2_ref, wg_ref, wu_ref, wd_ref, o_ref, *, bounds):
    o_ref[0] = x_ref[0] + g2_ref[0] * _swiglu_chunks(h_ref[0], wg_ref, wu_ref, wd_ref, bounds)


def _ffn(h2, x1, g2, wg, wu, wd, tm):
    B, L, D = x1.shape
    tm = min(tm, L)
    bm = (lambda b: b) if g2.shape[0] == B else (lambda b: 0)
    tok = pl.BlockSpec((1, tm, D), lambda b, i: (b, i, 0))
    wspec = lambda w: pl.BlockSpec(w.shape, lambda b, i: (0, 0), pipeline_mode=pl.Buffered(1))
    return pl.pallas_call(
        functools.partial(_ffn_kernel, bounds=_chunk_bounds(wg.shape[1], 2 * TF_SUB)),
        out_shape=jax.ShapeDtypeStruct((B, L, D), F32),
        grid=(B, L // tm),
        in_specs=[tok, tok, pl.BlockSpec((1, 1, D), lambda b, i: (bm(b), 0, 0)),
                  wspec(wg), wspec(wu), wspec(wd)],
        out_specs=tok,
        compiler_params=_cparams(("arbitrary", "arbitrary")),
        name="dense_ffn",
    )(h2, x1, g2, wg, wu, wd)


ROW_UNROLL = 8
ZERO_ROWS = 256


def _dispatch_kernel(ends_ref, pad_ref, d0_ref, d1_ref, h_ref, o_ref, zeros_ref, sem, zsem, *, tm):
    rows = h_ref.shape[0]
    zr = zeros_ref.shape[0]

    @pl.when(pl.program_id(0) == 0)
    def _():
        zeros_ref[...] = jnp.zeros_like(zeros_ref)

        def clear(row0, part):
            dst = o_ref.at[pl.ds(pl.multiple_of(row0, tm) + part * zr, zr)]
            return pltpu.make_async_copy(zeros_ref, dst, zsem)
        jobs = [(pad_ref[e] > 0, ends_ref[e] - tm) for e in range(N_EXPERTS)]
        total = ends_ref[N_EXPERTS - 1]
        jobs += [(total + t * tm < o_ref.shape[0], total + t * tm) for t in range(N_EXPERTS)]
        for cond, row0 in jobs:
            @pl.when(cond)
            def _():
                for part in range(tm // zr):
                    clear(row0, part).start()
        for cond, row0 in jobs:
            @pl.when(cond)
            def _():
                for part in range(tm // zr):
                    clear(row0, part).wait()

    def start(r, _):
        pltpu.make_async_copy(h_ref.at[pl.ds(r, 1)], o_ref.at[pl.ds(d0_ref[0, 0, r], 1)], sem).start(priority=0)
        pltpu.make_async_copy(h_ref.at[pl.ds(r, 1)], o_ref.at[pl.ds(d1_ref[0, 0, r], 1)], sem).start(priority=1)
        return 0

    lax.fori_loop(0, rows, start, 0, unroll=ROW_UNROLL)
    for _ in range(2):
        pltpu.make_async_copy(h_ref, o_ref.at[pl.ds(0, rows)], sem).wait()


def _dispatch(h, d0, d1, ends, padding, n_slots, tm, td):
    N, D = h.shape
    idx_spec = pl.BlockSpec((1, 1, td), lambda i, *_: (i, 0, 0), memory_space=pltpu.SMEM)
    return pl.pallas_call(
        functools.partial(_dispatch_kernel, tm=tm),
        out_shape=jax.ShapeDtypeStruct((n_slots, D), h.dtype),
        grid_spec=pltpu.PrefetchScalarGridSpec(
            num_scalar_prefetch=2,
            grid=(N // td,),
            in_specs=[idx_spec, idx_spec, pl.BlockSpec((td, D), lambda i, *_: (i, 0))],
            out_specs=pl.BlockSpec(memory_space=pl.ANY),
            scratch_shapes=[pltpu.VMEM((ZERO_ROWS, D), h.dtype), pltpu.SemaphoreType.DMA,
                            pltpu.SemaphoreType.DMA]),
        compiler_params=_cparams(("arbitrary",)),
        name="moe_dispatch",
    )(ends, padding, d0.reshape(N // td, 1, td), d1.reshape(N // td, 1, td), h)


def _gmm_kernel(te_ref, nu_ref, x_ref, wg_ref, wu_ref, wd_ref, o_ref, xb_ref, *, bounds):
    i, f = pl.program_id(0), pl.program_id(1)
    used = i < nu_ref[0]

    @pl.when(used & (f == 0))
    def _():
        xb_ref[...] = x_ref[...].astype(BF16)

    @pl.when(used)
    def _():
        part = _swiglu_chunks(xb_ref[...], wg_ref.at[0], wu_ref.at[0], wd_ref.at[0], bounds)

        @pl.when(f == 0)
        def _():
            o_ref[...] = part

        @pl.when(f > 0)
        def _():
            o_ref[...] += part

    @pl.when(jnp.logical_not(used) & (f == pl.num_programs(1) - 1))
    def _():
        o_ref[...] = jnp.zeros_like(o_ref)


def _grouped_swiglu(xs, tile_expert, n_used, wg, wu, wd, tm, tf):
    P, D = xs.shape
    E, _, F = wg.shape
    nf = F // tf
    n_tiles = P // tm

    def fsel(i, f, nu):
        return jnp.where(i < nu[0], f, nf - 1)

    return pl.pallas_call(
        functools.partial(_gmm_kernel, bounds=_chunk_bounds(tf, TF_SUB)),
        out_shape=jax.ShapeDtypeStruct((P, D), F32),
        grid_spec=pltpu.PrefetchScalarGridSpec(
            num_scalar_prefetch=2,
            grid=(n_tiles, nf),
            in_specs=[pl.BlockSpec((tm, D), lambda i, f, te, nu: (jnp.minimum(i, nu[0] - 1), 0)),
                      pl.BlockSpec((1, D, tf), lambda i, f, te, nu: (te[i], 0, fsel(i, f, nu))),
                      pl.BlockSpec((1, D, tf), lambda i, f, te, nu: (te[i], 0, fsel(i, f, nu))),
                      pl.BlockSpec((1, tf, D), lambda i, f, te, nu: (te[i], fsel(i, f, nu), 0))],
            out_specs=pl.BlockSpec((tm, D), lambda i, f, te, nu: (i, 0)),
            scratch_shapes=[pltpu.VMEM((tm, D), BF16)]),
        compiler_params=_cparams(("arbitrary", "arbitrary")),
        name="moe_grouped_swiglu",
    )(tile_expert, n_used, xs, wg, wu, wd)


def _combine_kernel(d0_ref, d1_ref, y_ref, x_ref, g2_ref, rt_ref, fg_ref, o_ref, buf, sem):
    rows = o_ref.shape[1]

    def start(r, _):
        pltpu.make_async_copy(y_ref.at[pl.ds(d0_ref[0, 0, r], 1)], buf.at[0, pl.ds(r, 1)], sem).start(priority=0)
        pltpu.make_async_copy(y_ref.at[pl.ds(d1_ref[0, 0, r], 1)], buf.at[1, pl.ds(r, 1)], sem).start(priority=1)
        return 0

    lax.fori_loop(0, rows, start, 0, unroll=ROW_UNROLL)
    for k in range(2):
        pltpu.make_async_copy(y_ref.at[pl.ds(0, rows)], buf.at[k], sem).wait()
    rt = rt_ref[0]
    moe = rt[:, 2:3] * buf[0] + rt[:, 3:4] * buf[1]
    x2 = x_ref[0] + g2_ref[0] * moe
    o_ref[0] = x2 * lax.rsqrt(jnp.mean(x2 * x2, axis=-1, keepdims=True) + NORM_EPS) * fg_ref[...]


def _combine(y, d0, d1, x1, g2, rt, final_g, tc):
    B, L, D = x1.shape
    tc = min(tc, L)
    nt = L // tc
    idx_spec = pl.BlockSpec((1, 1, tc), lambda b, i: (b * nt + i, 0, 0), memory_space=pltpu.SMEM)
    tok = lambda w: pl.BlockSpec((1, tc, w), lambda b, i: (b, i, 0))
    return pl.pallas_call(
        _combine_kernel,
        out_shape=jax.ShapeDtypeStruct((B, L, D), F32),
        grid=(B, nt),
        in_specs=[idx_spec, idx_spec, pl.BlockSpec(memory_space=pl.ANY), tok(D),
                  pl.BlockSpec((1, 1, D), lambda b, i: (b, 0, 0)), tok(LANES),
                  pl.BlockSpec((1, D), lambda b, i: (0, 0))],
        out_specs=tok(D),
        scratch_shapes=[pltpu.VMEM((2, tc, D), F32), pltpu.SemaphoreType.DMA],
        compiler_params=_cparams(("arbitrary", "arbitrary")),
        name="moe_combine_norm",
    )(d0.reshape(B * nt, 1, tc), d1.reshape(B * nt, 1, tc), y, x1, g2, rt, final_g.reshape(1, D))


def _routing(rt, tm):
    n = rt.shape[0]
    e = rt[:, :2].astype(jnp.int32).reshape(-1)
    onehot = (e[:, None] == jnp.arange(N_EXPERTS)[None, :]).astype(jnp.int32)
    csum = jnp.cumsum(onehot, axis=0)
    rank = jnp.sum((csum - onehot) * onehot, axis=1)
    counts = csum[-1]
    padded = ((counts + tm - 1) // tm) * tm
    ends = jnp.cumsum(padded)
    offs = ends - padded
    dest = jnp.sum(onehot * offs[None, :], axis=1) + rank
    n_slots = 2 * n + N_EXPERTS * tm
    tile_start = jnp.arange(n_slots // tm, dtype=jnp.int32) * tm
    n_used = (ends[-1] // tm).astype(jnp.int32)
    te = jnp.sum((tile_start[:, None] >= ends[None, :]).astype(jnp.int32), axis=1)
    last = jnp.sum((ends[-1] - tm >= ends).astype(jnp.int32))
    te = jnp.where(tile_start < ends[-1], te, last).astype(jnp.int32)
    d = dest.reshape(n, 2)
    return (n_slots, te, n_used.reshape(1), d[:, 0], d[:, 1], ends.astype(jnp.int32),
            (padded - counts).astype(jnp.int32))


TM_IN = 512
TQ_MLA = 256
TM_OUT = 512
TM_FFN = 512
TM_MOE = 1024
TF_MOE = 512
TF_SUB = 256
TD_DISPATCH = 512
TC_COMBINE = 256


def kernel(x, c, ctx, c_ctx, w_mod, b_mod, norm1_g, norm2_g, w_in, mla_q_norm, mla_w_uq, mla_kv_norm,
           mla_w_ukv, swa_sink, ret_decay_fwd, ret_decay_bwd, w_out, ffn_w_gate, ffn_w_up, ffn_w_down,
           moe_router, moe_w_gate, moe_w_up, moe_w_down, final_norm_g):
    B, L, D = x.shape
    Lc = ctx.shape[1]
    depth = w_mod.shape[0]
    xc = ctx

    cond = jnp.concatenate([c, c_ctx[None], jnp.zeros((16 - B - 1, D), F32)], axis=0)
    mod_all = _modulation(cond, w_mod, b_mod)
    tables = _rope_tables(L)

    for layer in range(depth):
        last = layer == depth - 1
        mod = mod_all[layer].reshape(16, 6, 1, D)
        sh1, sc1, g1, sh2, sc2, g2 = (mod[:B, j] for j in range(6))
        sh1x, sc1x, g1x, sh2x, sc2x, g2x = (mod[B:B + 1, j] for j in range(6))
        n1, n2 = norm1_g[layer], norm2_g[layer]

        wts = _prep_in_weights(w_in[layer], mla_q_norm[layer], mla_w_uq[layer],
                               mla_kv_norm[layer], mla_w_ukv[layer])
        lat = _in_proj(x, n1 * (1.0 + sc1), sh1, wts, tables, TM_IN)
        cx = _in_proj(xc, n1 * (1.0 + sc1x), sh1x, wts, None, TM_IN)
        qm, km, vm, sq, sk, sv, rq, rk, rv, rg = lat
        qmx, kmx, vmx, sqx, skx, svx, rqx, rkx, rvx, rgx = cx

        o_a = _mla_attention(qm, jnp.concatenate([km, kmx], axis=1),
                             jnp.concatenate([vm, vmx], axis=1), TQ_MLA)
        sink = swa_sink[layer].astype(F32) * LOG2E
        o_b = _swa_attention(sink, sq, sk, sv, skx, svx, True)
        rtabs = _ret_tables(ret_decay_fwd[layer], ret_decay_bwd[layer])
        o_c, oc_c = _retention(rq, rk, rv, rg, rqx, rkx, rvx, rgx, rtabs, not last)
        wo = w_out[layer].astype(BF16)

        if layer % 2 == 0:
            i = layer // 2
            wg, wu, wd = (ffn_w_gate[i].astype(BF16), ffn_w_up[i].astype(BF16),
                          ffn_w_down[i].astype(BF16))
            x1, h2 = _out_proj(o_a, o_b, o_c, x, g1, n2 * (1.0 + sc2), sh2, wo, None, TM_OUT)
            x_next = _ffn(h2, x1, g2, wg, wu, wd, TM_FFN)
        else:
            i = layer // 2
            r_hi = moe_router[i].astype(BF16)
            r_lo = (moe_router[i] - r_hi.astype(F32)).astype(BF16)
            router = jnp.pad(jnp.concatenate([r_hi, r_lo], axis=1), ((0, 0), (0, LANES - 2 * N_EXPERTS)))
            x1, h2p, rt = _out_proj(o_a, o_b, o_c, x, g1, n2 * (1.0 + sc2), sh2, wo, router, TM_OUT)
            n_slots, te, n_used, d0, d1, ends, padding = _routing(rt.reshape(B * L, LANES), TM_MOE)
            xs = _dispatch(h2p.reshape(B * L, D), d0, d1, ends, padding, n_slots, TM_MOE, TD_DISPATCH)
            y = _grouped_swiglu(xs, te, n_used, moe_w_gate[i].astype(BF16), moe_w_up[i].astype(BF16),
                                moe_w_down[i].astype(BF16), TM_MOE, TF_MOE)
            if last:
                return _combine(y, d0, d1, x1, g2, rt, final_norm_g, TC_COMBINE)
            raise NotImplementedError("expert layer is only supported as the last layer")

        if not last:
            oc_a = _mla_attention(qmx, kmx, vmx, TQ_MLA)
            oc_b = _swa_attention(sink, sqx, skx, svx, skx, svx, False)
            xc1, hc2 = _out_proj(oc_a, oc_b, oc_c, xc, g1x, n2 * (1.0 + sc2x), sh2x, wo, None, TM_OUT)
            xc = _ffn(hc2, xc1, g2x, wg, wu, wd, TM_FFN)
        x = x_next
    raise NotImplementedError("trunk must end with the expert layer")
```

```python
import functools
import math

import numpy as np
import jax
import jax.numpy as jnp
from jax import lax
from jax.experimental import pallas as pl
from jax.experimental.pallas import tpu as pltpu

F32 = jnp.float32
BF16 = jnp.bfloat16

D_MODEL = 1024
DEPTH = 2
GRID_W = 64
HEAD_DIM = 64
NORM_EPS = 1e-6
ROPE_BASE = 10000.0
NEG_INF = -1e30

MLA_HEADS = 4
MLA_Q_RANK = 192
MLA_KV_RANK = 128
MLA_NOPE = 64
MLA_ROPE = 32
MLA_V = 64

SWA_Q_HEADS = 8
SWA_KV_HEADS = 2
SWA_BLOCK = 128

RET_HEADS = 4
RET_DK = 64
RET_DV = 64
RET_CHUNK = 128

D_FF = 2816
N_EXPERTS = 8
D_FF_EXPERT = 3584

LOG2E = math.log2(math.e)
MLA_ONE_LANE = (64, 0)
LANES = 128
VMEM_LIMIT = 56 * 1024 * 1024

C_SQ, C_SK, C_SV = 0, 512, 768
C_RQ, C_RK, C_RV, C_RG = 1024, 1280, 1536, 1792
C_CKV, C_EXT = 2048, 2176
IN_COLS = 2432


def _cparams(sem, vmem=VMEM_LIMIT):
    return pltpu.CompilerParams(dimension_semantics=sem, vmem_limit_bytes=vmem)


def _dot(a, b):
    return jnp.dot(a, b, preferred_element_type=F32)


def _dot_nt(a, b):
    return lax.dot_general(a, b, (((1,), (1,)), ((), ())), preferred_element_type=F32)


def _dot_tn(a, b):
    return lax.dot_general(a, b, (((0,), (0,)), ((), ())), preferred_element_type=F32)


def _lane_iota(shape):
    return lax.broadcasted_iota(jnp.int32, shape, len(shape) - 1)


def _mod_kernel(c_ref, w_ref, b_ref, o_ref):
    c = c_ref[...]
    c = c * jax.nn.sigmoid(c)
    o_ref[0] = jnp.dot(c, w_ref[0], preferred_element_type=F32,
                       precision=lax.Precision.HIGHEST) + b_ref[0]


def _modulation(cond, w_mod, b_mod):
    depth, d, n = w_mod.shape
    rows = cond.shape[0]
    tn = 1024
    return pl.pallas_call(
        _mod_kernel,
        out_shape=jax.ShapeDtypeStruct((depth, rows, n), F32),
        grid=(depth, n // tn),
        in_specs=[pl.BlockSpec((rows, d), lambda l, j: (0, 0)),
                  pl.BlockSpec((1, d, tn), lambda l, j: (l, 0, j)),
                  pl.BlockSpec((1, 1, tn), lambda l, j: (l, 0, j))],
        out_specs=pl.BlockSpec((1, rows, tn), lambda l, j: (l, 0, j)),
        compiler_params=_cparams(("arbitrary", "arbitrary")),
        name="modulation",
    )(cond, w_mod, b_mod.reshape(depth, 1, n))


def _angles(pos, dim):
    inv = (ROPE_BASE ** (-np.arange(0, dim, 2, dtype=np.float32) / dim)).astype(np.float32)
    ang = pos.astype(np.float32)[:, None] * inv[None, :]
    return np.concatenate([ang, ang], axis=-1).astype(np.float64)


def _rope_tables(length):
    t = np.arange(length)
    rows, cols = t // GRID_W, t % GRID_W
    ar, ac = _angles(rows, 32), _angles(cols, 32)
    sign32 = np.concatenate([-np.ones(16), np.ones(16)])
    cos_a = np.concatenate([np.cos(ar), np.cos(ac)], axis=-1)
    sin_a = np.concatenate([np.sin(ar) * sign32, np.sin(ac) * sign32], axis=-1)
    cos_a, sin_a = np.tile(cos_a, (1, 2)), np.tile(sin_a, (1, 2))
    at = _angles(t, 64)
    sign64 = np.concatenate([-np.ones(32), np.ones(32)])
    cos_r, sin_r = np.tile(np.cos(at), (1, 2)), np.tile(np.sin(at) * sign64, (1, 2))
    mr, mc = _angles(rows, 16), _angles(cols, 16)
    cos_m = np.ones((length, LANES))
    sin_m = np.zeros((length, LANES))
    cos_m[:, 64:96] = np.concatenate([np.cos(mr), np.cos(mc)], axis=-1)
    sin_m[:, 64:96] = np.concatenate([np.sin(mr), np.sin(mc)], axis=-1)
    return tuple(jnp.asarray(a, F32) for a in (cos_a, sin_a, cos_r, sin_r, cos_m, sin_m))


def _prep_in_weights(w_in, q_norm, w_uq, kv_norm, w_ukv):
    cuts = np.cumsum([MLA_Q_RANK, MLA_KV_RANK, MLA_ROPE, 512, 128, 128, 256, 256, 256, 256])[:-1]
    cq, ckv, kpe, sq, sk, sv, rq, rk, rv, rg = jnp.split(w_in, [int(v) for v in cuts], axis=1)
    dup = lambda w: jnp.concatenate([w[:, :64], w[:, :64], w[:, 64:], w[:, 64:]], axis=1)
    d = w_in.shape[0]
    w_main = jnp.concatenate(
        [sq * (HEAD_DIM ** -0.5 * LOG2E), dup(sk), dup(sv), rq, rk * RET_DK ** -0.5, rv, rg, ckv,
         cq, kpe, jnp.zeros((d, 32), F32)], axis=1).astype(BF16)

    scale = (MLA_NOPE + MLA_ROPE) ** -0.5 * LOG2E
    wq = (w_uq * scale).reshape(MLA_Q_RANK, MLA_HEADS, MLA_NOPE + MLA_ROPE)
    wq = jnp.pad(wq, ((0, 64), (0, 0), (0, 32))).reshape(256, 512)
    place = np.zeros((256, 512), np.float32)
    for h in range(MLA_HEADS):
        for dd in range(MLA_ROPE):
            place[MLA_Q_RANK + dd, h * LANES + MLA_NOPE + dd] = 1.0
    wz = jnp.concatenate([wq, jnp.asarray(place)], axis=1)
    perm = np.zeros((1024, 1024), np.float32)
    for g in range(8):
        for dd in range(MLA_ROPE):
            e = dd % 16
            src = dd + 8 if e < 8 else dd - 8
            perm[g * LANES + MLA_NOPE + src, g * LANES + MLA_NOPE + dd] = -1.0 if e < 8 else 1.0
    wz_rot = wz @ jnp.asarray(perm)
    qn_ext = jnp.pad(q_norm, (0, 64)).reshape(1, 256)

    wkv = w_ukv.reshape(MLA_KV_RANK, MLA_HEADS, MLA_NOPE + MLA_V)
    kn = jnp.pad(wkv[:, :, :MLA_NOPE], ((0, 0), (0, 0), (0, 64))).reshape(MLA_KV_RANK, 512)
    vals = wkv[:, :, MLA_NOPE:]
    vv = jnp.stack([jnp.pad(vals[:, h], ((0, 0), (64, 0) if h % 2 else (0, 64))) for h in range(MLA_HEADS)],
                   axis=1).reshape(MLA_KV_RANK, 512)
    w_kv = jnp.concatenate([kn, vv], axis=1)
    return (w_main, wz.astype(BF16), wz_rot.astype(BF16), qn_ext, w_kv.astype(BF16),
            kv_norm.reshape(1, MLA_KV_RANK))


def _rope_roll(x, cos, sin_signed, half):
    lane = _lane_iota(x.shape)
    rot = jnp.where((lane % (2 * half)) < half,
                    pltpu.roll(x, LANES - half, 1), pltpu.roll(x, half, 1))
    return x * cos + rot * sin_signed


def _in_proj_kernel(*refs, rope):
    if rope:
        (x_ref, a_ref, sh_ref, w_ref, wz_ref, wzr_ref, qn_ref, wkv_ref, kvn_ref,
         ca_ref, sa_ref, cr_ref, sr_ref, cm_ref, sm_ref, *outs) = refs
    else:
        (x_ref, a_ref, sh_ref, w_ref, wz_ref, wzr_ref, qn_ref, wkv_ref, kvn_ref, *outs) = refs
    qm_ref, km_ref, vm_ref, sq_ref, sk_ref, sv_ref, rq_ref, rk_ref, rv_ref, rg_ref = outs

    x = x_ref[0]
    h = x * lax.rsqrt(jnp.mean(x * x, axis=-1, keepdims=True) + NORM_EPS) * a_ref[0] + sh_ref[0]
    p = _dot(h.astype(BF16), w_ref[...])

    def put(ref, col, width, tables=None, half=None):
        for g in range(width // LANES):
            blk = p[:, col + g * LANES: col + (g + 1) * LANES]
            if tables is not None:
                blk = _rope_roll(blk, tables[0][...], tables[1][...], half)
            ref[0, :, g * LANES:(g + 1) * LANES] = blk.astype(ref.dtype)

    axial = (ca_ref, sa_ref) if rope else None
    flat = (cr_ref, sr_ref) if rope else None
    put(sq_ref, C_SQ, 512, axial, 16)
    put(sk_ref, C_SK, 256, axial, 16)
    put(sv_ref, C_SV, 256)
    put(rq_ref, C_RQ, 256, flat, 32)
    put(rk_ref, C_RK, 256, flat, 32)
    put(rv_ref, C_RV, 256)
    put(rg_ref, C_RG, 256)

    ext = p[:, C_EXT:C_EXT + 256]
    lane = _lane_iota(ext.shape)
    is_cq = lane < MLA_Q_RANK
    cq_sq = jnp.where(is_cq, ext * ext, 0.0)
    inv = lax.rsqrt(jnp.sum(cq_sq, axis=-1, keepdims=True) * (1.0 / MLA_Q_RANK) + NORM_EPS)
    z = jnp.where(is_cq, ext * inv * qn_ref[...], ext).astype(BF16)
    zw = _dot(z, wz_ref[...])
    ckv = p[:, C_CKV:C_CKV + MLA_KV_RANK]
    ckv = ckv * lax.rsqrt(jnp.mean(ckv * ckv, axis=-1, keepdims=True) + NORM_EPS) * kvn_ref[...]
    kv = _dot(ckv.astype(BF16), wkv_ref[...])
    if rope:
        zr = _dot(z, wzr_ref[...])
    lane_g = _lane_iota((ext.shape[0], LANES))
    for g in range(MLA_HEADS):
        sl = slice(g * LANES, (g + 1) * LANES)
        sk_ = slice(512 + g * LANES, 512 + (g + 1) * LANES)
        q_g, kpe_g = zw[:, sl], zw[:, sk_]
        if rope:
            q_g = q_g * cm_ref[...] + zr[:, sl] * sm_ref[...]
            kpe_g = kpe_g * cm_ref[...] + zr[:, sk_] * sm_ref[...]
        qm_ref[0, :, sl] = q_g.astype(BF16)
        km_ref[0, :, sl] = (kv[:, sl] + kpe_g).astype(BF16)
        vm_ref[0, :, sl] = jnp.where(lane_g == MLA_ONE_LANE[g % 2], 1.0, kv[:, sk_]).astype(BF16)


def _in_proj(x, a, sh, wts, tables, tm):
    B, L, D = x.shape
    w_main, wz, wzr, qn_ext, w_kv, kvn = wts
    rope = tables is not None
    tm = min(tm, L)
    bm = (lambda b: b) if a.shape[0] == B else (lambda b: 0)
    const = lambda i, b: (0, 0)
    in_specs = [pl.BlockSpec((1, tm, D), lambda i, b: (b, i, 0)),
                pl.BlockSpec((1, 1, D), lambda i, b: (bm(b), 0, 0)),
                pl.BlockSpec((1, 1, D), lambda i, b: (bm(b), 0, 0)),
                pl.BlockSpec(w_main.shape, const), pl.BlockSpec(wz.shape, const),
                pl.BlockSpec(wzr.shape, const), pl.BlockSpec(qn_ext.shape, const),
                pl.BlockSpec(w_kv.shape, const), pl.BlockSpec(kvn.shape, const)]
    args = [x, a, sh, w_main, wz, wzr, qn_ext, w_kv, kvn]
    if rope:
        in_specs += [pl.BlockSpec((tm, LANES), lambda i, b: (i, 0))] * 6
        args += list(tables)
    widths = (512, 512, 512, 512, 256, 256, 256, 256, 256, 256)
    return pl.pallas_call(
        functools.partial(_in_proj_kernel, rope=rope),
        out_shape=[jax.ShapeDtypeStruct((B, L, w), BF16) for w in widths],
        grid=(L // tm, B),
        in_specs=in_specs,
        out_specs=[pl.BlockSpec((1, tm, w), lambda i, b: (b, i, 0)) for w in widths],
        compiler_params=_cparams(("arbitrary", "arbitrary")),
        name="in_proj_rope" if rope else "in_proj_ctx",
    )(*args)


def _mla_kernel(q_ref, k_ref, v_ref, o_ref):
    lane = _lane_iota((q_ref.shape[1], LANES))
    outs = []
    for h in range(MLA_HEADS):
        sl = slice(h * LANES, (h + 1) * LANES)
        s = _dot_nt(q_ref[0, :, sl], k_ref[0, :, sl])
        p = jnp.exp2(s - jnp.max(s, axis=-1, keepdims=True))
        o = _dot(p.astype(BF16), v_ref[0, :, sl])
        one = MLA_ONE_LANE[h % 2]
        outs.append(o * (1.0 / o[:, one:one + 1]))
    for g in range(MLA_HEADS // 2):
        o_ref[0, :, g * LANES:(g + 1) * LANES] = jnp.where(
            lane < 64, outs[2 * g], outs[2 * g + 1]).astype(o_ref.dtype)


def _mla_attention(qm, km, vm, tq):
    B, L, _ = qm.shape
    Lk = km.shape[1]
    tq = min(tq, L)
    return pl.pallas_call(
        _mla_kernel,
        out_shape=jax.ShapeDtypeStruct((B, L, MLA_HEADS * MLA_V), BF16),
        grid=(B, L // tq),
        in_specs=[pl.BlockSpec((1, tq, 512), lambda b, i: (b, i, 0)),
                  pl.BlockSpec((1, Lk, 512), lambda b, i: (b, 0, 0)),
                  pl.BlockSpec((1, Lk, 512), lambda b, i: (b, 0, 0))],
        out_specs=pl.BlockSpec((1, tq, MLA_HEADS * MLA_V), lambda b, i: (b, i, 0)),
        compiler_params=_cparams(("arbitrary", "arbitrary")),
        name="mla_attention",
    )(qm, km, vm)


SWA_TQ = 2 * SWA_BLOCK
SWA_BAND = SWA_TQ + 2 * SWA_BLOCK


def _swa_bias(n_ctx):
    W = SWA_BLOCK
    G = SWA_Q_HEADS // SWA_KV_HEADS
    qq = np.arange(G * SWA_TQ)[:, None] % SWA_TQ
    kk = np.arange(SWA_BAND + n_ctx)[None, :]
    is_ctx = kk >= SWA_BAND
    masks = [is_ctx | (np.abs(kk - shift - qq) <= W) for shift in (0, W, 2 * W)]
    return jnp.asarray(np.where(np.stack(masks), 0.0, NEG_INF), F32)


def _swa_kernel(sink_ref, q_ref, k_ref, v_ref, kc_ref, vc_ref, *rest, banded):
    W = SWA_BLOCK
    i = pl.program_id(1)
    if banded:
        bias_ref, o_ref = rest
        start = jnp.clip(i * SWA_TQ - W, 0, k_ref.shape[1] - SWA_BAND)
        start = pl.multiple_of(start, W)
        k_all = jnp.concatenate([k_ref[0, pl.ds(start, SWA_BAND), :], kc_ref[0]], axis=0)
        v_all = jnp.concatenate([v_ref[0, pl.ds(start, SWA_BAND), :], vc_ref[0]], axis=0)
    else:
        (o_ref,) = rest
        k_all, v_all = kc_ref[0], vc_ref[0]
    tq = q_ref.shape[1]
    G = SWA_Q_HEADS // SWA_KV_HEADS
    row = lax.broadcasted_iota(jnp.int32, (G * tq, 1), 0)
    lo = _lane_iota((tq, LANES)) < 64
    for g in range(SWA_KV_HEADS):
        gs = slice(g * LANES, (g + 1) * LANES)
        k_g, v_g = k_all[:, gs], v_all[:, gs]
        pieces = []
        sink = jnp.zeros((G * tq, 1), F32)
        for jj in range(G // 2):
            j = g * (G // 2) + jj
            q_pair = q_ref[0, :, j * LANES:(j + 1) * LANES]
            zq = jnp.zeros_like(q_pair)
            pieces += [jnp.where(lo, q_pair, zq), jnp.where(lo, zq, q_pair)]
        for hh in range(G):
            sink = jnp.where(row // tq == hh, sink_ref[g * G + hh], sink)
        s = _dot_nt(jnp.concatenate(pieces, axis=0), k_g)
        if banded:
            s = s + bias_ref[0]
        m = jnp.maximum(jnp.max(s, axis=-1, keepdims=True), sink)
        p = jnp.exp2(s - m)
        l = jnp.sum(p, axis=-1, keepdims=True) + jnp.exp2(sink - m)
        o = _dot(p.astype(BF16), v_g) * (1.0 / l)
        for jj in range(G // 2):
            j = g * (G // 2) + jj
            o_ref[0, :, j * LANES:(j + 1) * LANES] = jnp.where(
                lo, o[2 * jj * tq:(2 * jj + 1) * tq], o[(2 * jj + 1) * tq:(2 * jj + 2) * tq]).astype(o_ref.dtype)


def _swa_attention(sink, q, k, v, kc, vc, banded):
    B, L, _ = q.shape
    Lc = kc.shape[1]
    tq = SWA_TQ if banded else L
    nb = L // tq
    Lkv = k.shape[1]
    in_specs = [pl.BlockSpec(memory_space=pltpu.SMEM),
                pl.BlockSpec((1, tq, 512), lambda b, i: (b, i, 0)),
                pl.BlockSpec((1, Lkv, 256), lambda b, i: (b, 0, 0)),
                pl.BlockSpec((1, Lkv, 256), lambda b, i: (b, 0, 0)),
                pl.BlockSpec((1, Lc, 256), lambda b, i: (b, 0, 0)),
                pl.BlockSpec((1, Lc, 256), lambda b, i: (b, 0, 0))]
    args = [sink, q, k, v, kc, vc]
    if banded:
        assert nb >= 2 and L >= SWA_BAND, "band masks assume distinct first and last query tiles"
        bias = _swa_bias(Lc)
        in_specs.append(pl.BlockSpec((1,) + bias.shape[1:],
                                     lambda b, i: (jnp.where(i == 0, 0, jnp.where(i == nb - 1, 2, 1)), 0, 0)))
        args.append(bias)
    return pl.pallas_call(
        functools.partial(_swa_kernel, banded=banded),
        out_shape=jax.ShapeDtypeStruct((B, L, 512), BF16),
        grid=(B, nb),
        in_specs=in_specs,
        out_specs=pl.BlockSpec((1, tq, 512), lambda b, i: (b, i, 0)),
        compiler_params=_cparams(("arbitrary", "arbitrary")),
        name="swa_banded" if banded else "swa_context",
    )(*args)


def _ret_tables(decay_f, decay_b):
    C = RET_CHUNK
    lg_f = jnp.log(jax.nn.sigmoid(decay_f.astype(F32)))
    lg_b = jnp.log(jax.nn.sigmoid(decay_b.astype(F32)))
    idx = jnp.arange(C, dtype=F32)
    diff = idx[:, None] - idx[None, :]
    intra = (jnp.where(diff >= 0, jnp.exp(lg_f[:, None, None] * jnp.maximum(diff, 0.0)), 0.0)
             + jnp.where(diff <= 0, jnp.exp(lg_b[:, None, None] * jnp.maximum(-diff, 0.0)), 0.0))
    lanes = lambda t: jnp.repeat(t.T, RET_DK, axis=1)
    qdf = lanes(jnp.exp(lg_f[:, None] * (idx + 1.0)))
    qdb = lanes(jnp.exp(lg_b[:, None] * (C - idx)))
    kdf = lanes(jnp.exp(lg_f[:, None] * (C - 1.0 - idx)))
    kdb = lanes(jnp.exp(lg_b[:, None] * idx))
    cdf = jnp.repeat(jnp.exp(lg_f * C), RET_DV).reshape(1, -1)
    cdb = jnp.repeat(jnp.exp(lg_b * C), RET_DV).reshape(1, -1)
    return intra, qdf, qdb, kdf, kdb, cdf, cdb


def _ret_kernel(q_ref, k_ref, v_ref, g_ref, qx_ref, kx_ref, vx_ref, gx_ref,
                d_ref, qdf_ref, qdb_ref, kdf_ref, kdb_ref, cdf_ref, cdb_ref,
                *rest, ctx_out):
    if ctx_out:
        o_ref, ox_ref, sf, sb, sfx, sbx = rest
    else:
        o_ref, sf, sb, sfx, sbx = rest
        ox_ref = None
    C = RET_CHUNK
    nc = q_ref.shape[1] // C
    ncx = qx_ref.shape[1] // C
    NG = RET_HEADS // 2
    r = lax.broadcasted_iota(jnp.int32, (LANES, LANES), 0)
    cidx = lax.broadcasted_iota(jnp.int32, (LANES, LANES), 1)
    blockdiag = (r // 64) == (cidx // 64)
    lane = _lane_iota((C, LANES))
    lo = lane < 64

    def kv_sum(kr, vr, c0, kd_ref, j):
        gs = slice(j * LANES, (j + 1) * LANES)
        kd = (kr[0, pl.ds(c0, C), gs].astype(F32) * kd_ref[:, gs]).astype(BF16)
        return jnp.where(blockdiag, _dot_tn(kd, vr[0, pl.ds(c0, C), gs]), 0.0)

    def state_pass(kr, vr, stf, stb, n):
        def body(t, _):
            cf, cb = t, n - 1 - t
            f0, b0 = pl.multiple_of(cf * C, C), pl.multiple_of(cb * C, C)
            for j in range(NG):
                gs = slice(j * LANES, (j + 1) * LANES)
                stf[cf + 1, j] = stf[cf, j] * cdf_ref[:, gs] + kv_sum(kr, vr, f0, kdf_ref, j)
                stb[cb, j] = stb[cb + 1, j] * cdb_ref[:, gs] + kv_sum(kr, vr, b0, kdb_ref, j)
            return 0
        lax.fori_loop(0, n, body, 0, unroll=2)

    zero = jnp.zeros((NG, LANES, LANES), F32)
    sfx[0] = zero
    sbx[ncx] = zero
    state_pass(kx_ref, vx_ref, sfx, sbx, ncx)
    sf[0] = sfx[ncx]
    sb[nc] = sbx[0]
    state_pass(k_ref, v_ref, sf, sb, nc)

    def out_pass(qr, kr, vr, gr, orf, stf, stb, n):
        def body(c, _):
            c0 = pl.multiple_of(c * C, C)
            for j in range(NG):
                gs = slice(j * LANES, (j + 1) * LANES)
                qg, kg, vg = qr[0, pl.ds(c0, C), gs], kr[0, pl.ds(c0, C), gs], vr[0, pl.ds(c0, C), gs]
                halves = []
                for half in range(2):
                    zq = jnp.zeros_like(qg)
                    qh = jnp.where(lo, qg, zq) if half == 0 else jnp.where(lo, zq, qg)
                    att = _dot_nt(qh, kg) * d_ref[2 * j + half]
                    halves.append(_dot(att.astype(BF16), vg))
                o = jnp.where(lo, halves[0], halves[1])
                qf = qg.astype(F32)
                qd = jnp.concatenate([(qf * qdf_ref[:, gs]).astype(BF16),
                                      (qf * qdb_ref[:, gs]).astype(BF16)], axis=1)
                s_cat = jnp.concatenate([stf[c, j], stb[c + 1, j]], axis=0).astype(BF16)
                o = o + _dot(qd, s_cat)
                o2 = o * o
                ms = jnp.where(lo, jnp.sum(jnp.where(lo, o2, 0.0), axis=-1, keepdims=True),
                               jnp.sum(jnp.where(lo, 0.0, o2), axis=-1, keepdims=True)) * (1.0 / RET_DV)
                gate = gr[0, pl.ds(c0, C), gs].astype(F32)
                y = o * lax.rsqrt(ms + NORM_EPS) * (gate * jax.nn.sigmoid(gate))
                orf[0, pl.ds(c0, C), gs] = y.astype(orf.dtype)
            return 0
        lax.fori_loop(0, n, body, 0, unroll=2)

    out_pass(q_ref, k_ref, v_ref, g_ref, o_ref, sf, sb, nc)
    if ctx_out:
        out_pass(qx_ref, kx_ref, vx_ref, gx_ref, ox_ref, sfx, sbx, ncx)


def _retention(q, k, v, g, qx, kx, vx, gx, tabs, ctx_out):
    B, L, W = q.shape
    Lc = qx.shape[1]
    nc, ncx = L // RET_CHUNK, Lc // RET_CHUNK
    lat = pl.BlockSpec((1, L, W), lambda b: (b, 0, 0))
    cx = pl.BlockSpec((1, Lc, W), lambda b: (b, 0, 0))
    full = lambda a: pl.BlockSpec(a.shape, lambda b: (0,) * a.ndim)
    out_shape = [jax.ShapeDtypeStruct((B, L, W), BF16)]
    out_specs = [lat]
    if ctx_out:
        out_shape.append(jax.ShapeDtypeStruct((B, Lc, W), BF16))
        out_specs.append(cx)
    res = pl.pallas_call(
        functools.partial(_ret_kernel, ctx_out=ctx_out),
        out_shape=out_shape,
        grid=(B,),
        in_specs=[lat, lat, lat, lat, cx, cx, cx, cx] + [full(t) for t in tabs],
        out_specs=out_specs,
        scratch_shapes=[pltpu.VMEM((n + 1, RET_HEADS // 2, LANES, LANES), F32) for n in (nc, nc, ncx, ncx)],
        compiler_params=_cparams(("arbitrary",)),
        name="retention",
    )(q, k, v, g, qx, kx, vx, gx, *tabs)
    return (res[0], res[1]) if ctx_out else (res[0], None)


def _swiglu_act(g, u):
    h = 0.5 * g
    return ((h + h * jnp.tanh(h)) * u).astype(BF16)


def _swiglu_chunks(x, wg, wu, wd, bounds):
    acc = None
    for c0, c1 in bounds:
        part = _dot(_swiglu_act(_dot(x, wg[:, c0:c1]), _dot(x, wu[:, c0:c1])), wd[c0:c1, :])
        acc = part if acc is None else acc + part
    return acc


def _chunk_bounds(total, size):
    return tuple((c, min(c + size, total)) for c in range(0, total, size))


def _split_hi_lo(a):
    hi = a.astype(BF16)
    return hi, (a - hi.astype(F32)).astype(BF16)


def _mix_out(oa_ref, ob_ref, oc_ref, x_ref, g1_ref, a2_ref, sh2_ref, wo_ref):
    y = (_dot(oa_ref[0], wo_ref[0:256, :]) + _dot(ob_ref[0], wo_ref[256:768, :])
         + _dot(oc_ref[0], wo_ref[768:1024, :]))
    x1 = x_ref[0] + g1_ref[0] * y
    h2 = x1 * lax.rsqrt(jnp.mean(x1 * x1, axis=-1, keepdims=True) + NORM_EPS) * a2_ref[0] + sh2_ref[0]
    return x1, h2


def _out_ffn_kernel(oa_ref, ob_ref, oc_ref, x_ref, g1_ref, a2_ref, sh2_ref, g2_ref, wo_ref,
                    wg_ref, wu_ref, wd_ref, o_ref, *, bounds):
    x1, h2 = _mix_out(oa_ref, ob_ref, oc_ref, x_ref, g1_ref, a2_ref, sh2_ref, wo_ref)
    o_ref[0] = x1 + g2_ref[0] * _swiglu_chunks(h2.astype(BF16), wg_ref, wu_ref, wd_ref, bounds)


def _out_ffn(oa, ob, oc, x, g1, a2, sh2, g2, wo, wg, wu, wd, tm):
    B, L, D = x.shape
    tm = min(tm, L)
    bm = (lambda b: b) if g1.shape[0] == B else (lambda b: 0)
    tok = lambda w: pl.BlockSpec((1, tm, w), lambda b, i: (b, i, 0))
    vec = pl.BlockSpec((1, 1, D), lambda b, i: (bm(b), 0, 0))
    wspec = lambda w: pl.BlockSpec(w.shape, lambda b, i: (0, 0), pipeline_mode=pl.Buffered(1))
    return pl.pallas_call(
        functools.partial(_out_ffn_kernel, bounds=_chunk_bounds(wg.shape[1], TF_SUB)),
        out_shape=jax.ShapeDtypeStruct((B, L, D), F32),
        grid=(B, L // tm),
        in_specs=[tok(256), tok(512), tok(256), tok(D), vec, vec, vec, vec,
                  wspec(wo), wspec(wg), wspec(wu), wspec(wd)],
        out_specs=tok(D),
        compiler_params=_cparams(("arbitrary", "arbitrary")),
        name="out_proj_dense_ffn",
    )(oa, ob, oc, x, g1, a2, sh2, g2, wo, wg, wu, wd)


def _out_proj_kernel(oa_ref, ob_ref, oc_ref, x_ref, g1_ref, a2_ref, sh2_ref, wo_ref, r_ref,
                     x1_ref, h2_ref, rt_ref):
    x1, h2 = _mix_out(oa_ref, ob_ref, oc_ref, x_ref, g1_ref, a2_ref, sh2_ref, wo_ref)
    x1_ref[0] = x1
    h2_ref[0] = h2
    tm = h2.shape[0]
    h_hi, h_lo = _split_hi_lo(h2)
    prod = _dot(jnp.concatenate([h_hi, h_lo], axis=0), r_ref[...])
    logits = prod[:tm] + pltpu.roll(prod[:tm], LANES - N_EXPERTS, 1) + prod[tm:]
    lane = _lane_iota(logits.shape)
    logits = jnp.where(lane < N_EXPERTS, logits, -jnp.inf)
    v1 = jnp.max(logits, axis=-1, keepdims=True)
    i1 = jnp.min(jnp.where(logits == v1, lane, LANES), axis=-1, keepdims=True)
    rest_l = jnp.where(lane == i1, -jnp.inf, logits)
    v2 = jnp.max(rest_l, axis=-1, keepdims=True)
    i2 = jnp.min(jnp.where(rest_l == v2, lane, LANES), axis=-1, keepdims=True)
    e2 = jnp.exp(v2 - v1)
    w1 = 1.0 / (1.0 + e2)
    w2 = e2 * w1
    rt_ref[0] = jnp.where(lane == 0, i1.astype(F32),
                          jnp.where(lane == 1, i2.astype(F32),
                                    jnp.where(lane == 2, w1, jnp.where(lane == 3, w2, 0.0))))


def _out_proj_route(oa, ob, oc, x, g1, a2, sh2, wo, router, tm):
    B, L, D = x.shape
    tm = min(tm, L)
    tok = lambda w: pl.BlockSpec((1, tm, w), lambda b, i: (b, i, 0))
    vec = pl.BlockSpec((1, 1, D), lambda b, i: (b, 0, 0))
    return pl.pallas_call(
        _out_proj_kernel,
        out_shape=[jax.ShapeDtypeStruct((B, L, D), F32), jax.ShapeDtypeStruct((B, L, D), F32),
                   jax.ShapeDtypeStruct((B, L, LANES), F32)],
        grid=(B, L // tm),
        in_specs=[tok(256), tok(512), tok(256), tok(D), vec, vec, vec,
                  pl.BlockSpec(wo.shape, lambda b, i: (0, 0)), pl.BlockSpec(router.shape, lambda b, i: (0, 0))],
        out_specs=[tok(D), tok(D), tok(LANES)],
        compiler_params=_cparams(("arbitrary", "arbitrary")),
        name="out_proj_route",
    )(oa, ob, oc, x, g1, a2, sh2, wo, router)


ROW_UNROLL = 8
ZERO_ROWS = 256


def _dispatch_kernel(ends_ref, pad_ref, d0_ref, d1_ref, h_ref, o_ref, zeros_ref, sem, zsem, *, tm):
    rows = h_ref.shape[0]
    zr = zeros_ref.shape[0]

    @pl.when(pl.program_id(0) == 0)
    def _():
        zeros_ref[...] = jnp.zeros_like(zeros_ref)

        def clear(row0, part):
            dst = o_ref.at[pl.ds(pl.multiple_of(row0, tm) + part * zr, zr)]
            return pltpu.make_async_copy(zeros_ref, dst, zsem)
        jobs = [(pad_ref[e] > 0, ends_ref[e] - tm) for e in range(N_EXPERTS)]
        total = ends_ref[N_EXPERTS - 1]
        jobs += [(total + t * tm < o_ref.shape[0], total + t * tm) for t in range(N_EXPERTS)]
        for cond, row0 in jobs:
            @pl.when(cond)
            def _():
                for part in range(tm // zr):
                    clear(row0, part).start()
        for cond, row0 in jobs:
            @pl.when(cond)
            def _():
                for part in range(tm // zr):
                    clear(row0, part).wait()

    def start(r, _):
        pltpu.make_async_copy(h_ref.at[pl.ds(r, 1)], o_ref.at[pl.ds(d0_ref[0, 0, r], 1)], sem).start(priority=0)
        pltpu.make_async_copy(h_ref.at[pl.ds(r, 1)], o_ref.at[pl.ds(d1_ref[0, 0, r], 1)], sem).start(priority=1)
        return 0

    lax.fori_loop(0, rows, start, 0, unroll=ROW_UNROLL)
    for _ in range(2):
        pltpu.make_async_copy(h_ref, o_ref.at[pl.ds(0, rows)], sem).wait()


def _dispatch(h, d0, d1, ends, padding, n_slots, tm, td):
    N, D = h.shape
    idx_spec = pl.BlockSpec((1, 1, td), lambda i, *_: (i, 0, 0), memory_space=pltpu.SMEM)
    return pl.pallas_call(
        functools.partial(_dispatch_kernel, tm=tm),
        out_shape=jax.ShapeDtypeStruct((n_slots, D), h.dtype),
        grid_spec=pltpu.PrefetchScalarGridSpec(
            num_scalar_prefetch=2,
            grid=(N // td,),
            in_specs=[idx_spec, idx_spec, pl.BlockSpec((td, D), lambda i, *_: (i, 0))],
            out_specs=pl.BlockSpec(memory_space=pl.ANY),
            scratch_shapes=[pltpu.VMEM((ZERO_ROWS, D), h.dtype), pltpu.SemaphoreType.DMA,
                            pltpu.SemaphoreType.DMA]),
        compiler_params=_cparams(("arbitrary",)),
        name="moe_dispatch",
    )(ends, padding, d0.reshape(N // td, 1, td), d1.reshape(N // td, 1, td), h)


def _gmm_kernel(te_ref, nu_ref, x_ref, wg_ref, wu_ref, wd_ref, o_ref, *, bounds):
    @pl.when(pl.program_id(0) < nu_ref[0])
    def _():
        o_ref[...] = _swiglu_chunks(x_ref[...].astype(BF16), wg_ref.at[0], wu_ref.at[0], wd_ref.at[0], bounds)

    @pl.when(pl.program_id(0) >= nu_ref[0])
    def _():
        o_ref[...] = jnp.zeros_like(o_ref)


def _grouped_swiglu(xs, tile_expert, n_used, wg, wu, wd, tm):
    P, D = xs.shape
    E, _, F = wg.shape
    resident = lambda shape: pl.BlockSpec((1,) + shape, lambda i, te, nu: (te[i], 0, 0),
                                          pipeline_mode=pl.Buffered(1))
    return pl.pallas_call(
        functools.partial(_gmm_kernel, bounds=_chunk_bounds(F, TF_SUB)),
        out_shape=jax.ShapeDtypeStruct((P, D), F32),
        grid_spec=pltpu.PrefetchScalarGridSpec(
            num_scalar_prefetch=2,
            grid=(P // tm,),
            in_specs=[pl.BlockSpec((tm, D), lambda i, te, nu: (jnp.minimum(i, nu[0] - 1), 0)),
                      resident((D, F)), resident((D, F)), resident((F, D))],
            out_specs=pl.BlockSpec((tm, D), lambda i, te, nu: (i, 0))),
        compiler_params=_cparams(("arbitrary",)),
        name="moe_grouped_swiglu",
    )(tile_expert, n_used, xs, wg, wu, wd)


def _combine_kernel(d0_ref, d1_ref, y_ref, x_ref, g2_ref, rt_ref, fg_ref, o_ref, buf, sem):
    rows = o_ref.shape[1]

    def start(r, _):
        pltpu.make_async_copy(y_ref.at[pl.ds(d0_ref[0, 0, r], 1)], buf.at[0, pl.ds(r, 1)], sem).start(priority=0)
        pltpu.make_async_copy(y_ref.at[pl.ds(d1_ref[0, 0, r], 1)], buf.at[1, pl.ds(r, 1)], sem).start(priority=1)
        return 0

    lax.fori_loop(0, rows, start, 0, unroll=ROW_UNROLL)
    for k in range(2):
        pltpu.make_async_copy(y_ref.at[pl.ds(0, rows)], buf.at[k], sem).wait()
    rt = rt_ref[0]
    moe = rt[:, 2:3] * buf[0] + rt[:, 3:4] * buf[1]
    x2 = x_ref[0] + g2_ref[0] * moe
    o_ref[0] = x2 * lax.rsqrt(jnp.mean(x2 * x2, axis=-1, keepdims=True) + NORM_EPS) * fg_ref[...]


def _combine(y, d0, d1, x1, g2, rt, final_g, tc):
    B, L, D = x1.shape
    tc = min(tc, L)
    nt = L // tc
    idx_spec = pl.BlockSpec((1, 1, tc), lambda b, i: (b * nt + i, 0, 0), memory_space=pltpu.SMEM)
    tok = lambda w: pl.BlockSpec((1, tc, w), lambda b, i: (b, i, 0))
    return pl.pallas_call(
        _combine_kernel,
        out_shape=jax.ShapeDtypeStruct((B, L, D), F32),
        grid=(B, nt),
        in_specs=[idx_spec, idx_spec, pl.BlockSpec(memory_space=pl.ANY), tok(D),
                  pl.BlockSpec((1, 1, D), lambda b, i: (b, 0, 0)), tok(LANES),
                  pl.BlockSpec((1, D), lambda b, i: (0, 0))],
        out_specs=tok(D),
        scratch_shapes=[pltpu.VMEM((2, tc, D), F32), pltpu.SemaphoreType.DMA],
        compiler_params=_cparams(("arbitrary", "arbitrary")),
        name="moe_combine_norm",
    )(d0.reshape(B * nt, 1, tc), d1.reshape(B * nt, 1, tc), y, x1, g2, rt, final_g.reshape(1, D))


def _routing(rt, tm):
    n = rt.shape[0]
    e = rt[:, :2].astype(jnp.int32).reshape(-1)
    onehot = (e[:, None] == jnp.arange(N_EXPERTS)[None, :]).astype(jnp.int32)
    csum = jnp.cumsum(onehot, axis=0)
    rank = jnp.sum((csum - onehot) * onehot, axis=1)
    counts = csum[-1]
    padded = ((counts + tm - 1) // tm) * tm
    ends = jnp.cumsum(padded)
    offs = ends - padded
    dest = jnp.sum(onehot * offs[None, :], axis=1) + rank
    n_slots = 2 * n + N_EXPERTS * tm
    tile_start = jnp.arange(n_slots // tm, dtype=jnp.int32) * tm
    n_used = (ends[-1] // tm).astype(jnp.int32)
    te = jnp.sum((tile_start[:, None] >= ends[None, :]).astype(jnp.int32), axis=1)
    last = jnp.sum((ends[-1] - tm >= ends).astype(jnp.int32))
    te = jnp.where(tile_start < ends[-1], te, last).astype(jnp.int32)
    d = dest.reshape(n, 2)
    return (n_slots, te, n_used.reshape(1), d[:, 0], d[:, 1], ends.astype(jnp.int32),
            (padded - counts).astype(jnp.int32))


TM_IN = 512
TQ_MLA = 512
TM_OUT = 512
TM_FFN = 512
TM_MOE = 512
TF_SUB = 512
TD_DISPATCH = 512
TC_COMBINE = 256


def kernel(x, c, ctx, c_ctx, w_mod, b_mod, norm1_g, norm2_g, w_in, mla_q_norm, mla_w_uq, mla_kv_norm,
           mla_w_ukv, swa_sink, ret_decay_fwd, ret_decay_bwd, w_out, ffn_w_gate, ffn_w_up, ffn_w_down,
           moe_router, moe_w_gate, moe_w_up, moe_w_down, final_norm_g):
    B, L, D = x.shape
    Lc = ctx.shape[1]
    depth = w_mod.shape[0]
    xc = ctx

    cond = jnp.concatenate([c, c_ctx[None], jnp.zeros((16 - B - 1, D), F32)], axis=0)
    mod_all = _modulation(cond, w_mod, b_mod)
    tables = _rope_tables(L)

    for layer in range(depth):
        last = layer == depth - 1
        mod = mod_all[layer].reshape(16, 6, 1, D)
        sh1, sc1, g1, sh2, sc2, g2 = (mod[:B, j] for j in range(6))
        sh1x, sc1x, g1x, sh2x, sc2x, g2x = (mod[B:B + 1, j] for j in range(6))
        n1, n2 = norm1_g[layer], norm2_g[layer]

        wts = _prep_in_weights(w_in[layer], mla_q_norm[layer], mla_w_uq[layer],
                               mla_kv_norm[layer], mla_w_ukv[layer])
        lat = _in_proj(x, n1 * (1.0 + sc1), sh1, wts, tables, TM_IN)
        cx = _in_proj(xc, n1 * (1.0 + sc1x), sh1x, wts, None, TM_IN)
        qm, km, vm, sq, sk, sv, rq, rk, rv, rg = lat
        qmx, kmx, vmx, sqx, skx, svx, rqx, rkx, rvx, rgx = cx

        o_a = _mla_attention(qm, jnp.concatenate([km, kmx], axis=1),
                             jnp.concatenate([vm, vmx], axis=1), TQ_MLA)
        sink = swa_sink[layer].astype(F32) * LOG2E
        o_b = _swa_attention(sink, sq, sk, sv, skx, svx, True)
        rtabs = _ret_tables(ret_decay_fwd[layer], ret_decay_bwd[layer])
        o_c, oc_c = _retention(rq, rk, rv, rg, rqx, rkx, rvx, rgx, rtabs, not last)
        wo = w_out[layer].astype(BF16)

        if layer % 2 == 0:
            i = layer // 2
            wg, wu, wd = (ffn_w_gate[i].astype(BF16), ffn_w_up[i].astype(BF16),
                          ffn_w_down[i].astype(BF16))
            x_next = _out_ffn(o_a, o_b, o_c, x, g1, n2 * (1.0 + sc2), sh2, g2, wo, wg, wu, wd, TM_FFN)
        else:
            i = layer // 2
            r_hi = moe_router[i].astype(BF16)
            r_lo = (moe_router[i] - r_hi.astype(F32)).astype(BF16)
            router = jnp.pad(jnp.concatenate([r_hi, r_lo], axis=1), ((0, 0), (0, LANES - 2 * N_EXPERTS)))
            x1, h2, rt = _out_proj_route(o_a, o_b, o_c, x, g1, n2 * (1.0 + sc2), sh2, wo, router, TM_OUT)
            n_slots, te, n_used, d0, d1, ends, padding = _routing(rt.reshape(B * L, LANES), TM_MOE)
            xs = _dispatch(h2.reshape(B * L, D), d0, d1, ends, padding, n_slots, TM_MOE, TD_DISPATCH)
            y = _grouped_swiglu(xs, te, n_used, moe_w_gate[i].astype(BF16), moe_w_up[i].astype(BF16),
                                moe_w_down[i].astype(BF16), TM_MOE)
            if last:
                return _combine(y, d0, d1, x1, g2, rt, final_norm_g, TC_COMBINE)
            raise NotImplementedError("expert layer is only supported as the last layer")

        if not last:
            oc_a = _mla_attention(qmx, kmx, vmx, TQ_MLA)
            oc_b = _swa_attention(sink, sqx, skx, svx, skx, svx, False)
            xc = _out_ffn(oc_a, oc_b, oc_c, xc, g1x, n2 * (1.0 + sc2x), sh2x, g2x, wo, wg, wu, wd, TM_FFN)
        x = x_next
    raise NotImplementedError("trunk must end with the expert layer")
```

```python
import functools
import math

import numpy as np
import jax
import jax.numpy as jnp
from jax import lax
from jax.experimental import pallas as pl
from jax.experimental.pallas import tpu as pltpu

F32 = jnp.float32
BF16 = jnp.bfloat16

D_MODEL = 1024
DEPTH = 2
GRID_W = 64
HEAD_DIM = 64
NORM_EPS = 1e-6
ROPE_BASE = 10000.0
NEG_INF = -1e30

MLA_HEADS = 4
MLA_Q_RANK = 192
MLA_KV_RANK = 128
MLA_NOPE = 64
MLA_ROPE = 32
MLA_V = 64

SWA_Q_HEADS = 8
SWA_KV_HEADS = 2
SWA_BLOCK = 128

RET_HEADS = 4
RET_DK = 64
RET_DV = 64
RET_CHUNK = 128

D_FF = 2816
N_EXPERTS = 8
D_FF_EXPERT = 3584

LOG2E = math.log2(math.e)
MLA_ONE_LANE = (64, 0)
LANES = 128
VMEM_LIMIT = 56 * 1024 * 1024

C_SQ, C_SK, C_SV = 0, 512, 768
C_RQ, C_RK, C_RV, C_RG = 1024, 1280, 1536, 1792
C_CKV, C_EXT = 2048, 2176
IN_COLS = 2432


def _cparams(sem, vmem=VMEM_LIMIT):
    return pltpu.CompilerParams(dimension_semantics=sem, vmem_limit_bytes=vmem)


def _dot(a, b):
    return jnp.dot(a, b, preferred_element_type=F32)


def _dot_nt(a, b):
    return lax.dot_general(a, b, (((1,), (1,)), ((), ())), preferred_element_type=F32)


def _dot_tn(a, b):
    return lax.dot_general(a, b, (((0,), (0,)), ((), ())), preferred_element_type=F32)


def _lane_iota(shape):
    return lax.broadcasted_iota(jnp.int32, shape, len(shape) - 1)


def _mod_kernel(c_ref, w_ref, b_ref, o_ref):
    c = c_ref[...]
    c = c * jax.nn.sigmoid(c)
    o_ref[0] = jnp.dot(c, w_ref[0], preferred_element_type=F32,
                       precision=lax.Precision.HIGHEST) + b_ref[0]


def _modulation(cond, w_mod, b_mod):
    depth, d, n = w_mod.shape
    rows = cond.shape[0]
    tn = 1024
    return pl.pallas_call(
        _mod_kernel,
        out_shape=jax.ShapeDtypeStruct((depth, rows, n), F32),
        grid=(depth, n // tn),
        in_specs=[pl.BlockSpec((rows, d), lambda l, j: (0, 0)),
                  pl.BlockSpec((1, d, tn), lambda l, j: (l, 0, j)),
                  pl.BlockSpec((1, 1, tn), lambda l, j: (l, 0, j))],
        out_specs=pl.BlockSpec((1, rows, tn), lambda l, j: (l, 0, j)),
        compiler_params=_cparams(("arbitrary", "arbitrary")),
        name="modulation",
    )(cond, w_mod, b_mod.reshape(depth, 1, n))


def _angles(pos, dim):
    inv = (ROPE_BASE ** (-np.arange(0, dim, 2, dtype=np.float32) / dim)).astype(np.float32)
    ang = pos.astype(np.float32)[:, None] * inv[None, :]
    return np.concatenate([ang, ang], axis=-1).astype(np.float64)


def _rope_tables(length):
    t = np.arange(length)
    rows, cols = t // GRID_W, t % GRID_W
    ar, ac = _angles(rows, 32), _angles(cols, 32)
    sign32 = np.concatenate([-np.ones(16), np.ones(16)])
    cos_a = np.concatenate([np.cos(ar), np.cos(ac)], axis=-1)
    sin_a = np.concatenate([np.sin(ar) * sign32, np.sin(ac) * sign32], axis=-1)
    cos_a, sin_a = np.tile(cos_a, (1, 2)), np.tile(sin_a, (1, 2))
    at = _angles(t, 64)
    sign64 = np.concatenate([-np.ones(32), np.ones(32)])
    cos_r, sin_r = np.tile(np.cos(at), (1, 2)), np.tile(np.sin(at) * sign64, (1, 2))
    mr, mc = _angles(rows, 16), _angles(cols, 16)
    cos_m = np.ones((length, LANES))
    sin_m = np.zeros((length, LANES))
    cos_m[:, 64:96] = np.concatenate([np.cos(mr), np.cos(mc)], axis=-1)
    sin_m[:, 64:96] = np.concatenate([np.sin(mr), np.sin(mc)], axis=-1)
    return tuple(jnp.asarray(a, F32) for a in (cos_a, sin_a, cos_r, sin_r, cos_m, sin_m))


def _prep_in_weights(w_in, q_norm, w_uq, kv_norm, w_ukv):
    cuts = np.cumsum([MLA_Q_RANK, MLA_KV_RANK, MLA_ROPE, 512, 128, 128, 256, 256, 256, 256])[:-1]
    cq, ckv, kpe, sq, sk, sv, rq, rk, rv, rg = jnp.split(w_in, [int(v) for v in cuts], axis=1)
    dup = lambda w: jnp.concatenate([w[:, :64], w[:, :64], w[:, 64:], w[:, 64:]], axis=1)
    d = w_in.shape[0]
    w_main = jnp.concatenate(
        [sq * (HEAD_DIM ** -0.5 * LOG2E), dup(sk), dup(sv), rq, rk * RET_DK ** -0.5, rv, rg, ckv,
         cq, kpe, jnp.zeros((d, 32), F32)], axis=1).astype(BF16)

    scale = (MLA_NOPE + MLA_ROPE) ** -0.5 * LOG2E
    wq = (w_uq * scale).reshape(MLA_Q_RANK, MLA_HEADS, MLA_NOPE + MLA_ROPE)
    wq = jnp.pad(wq, ((0, 64), (0, 0), (0, 32))).reshape(256, 512)
    place = np.zeros((256, 512), np.float32)
    for h in range(MLA_HEADS):
        for dd in range(MLA_ROPE):
            place[MLA_Q_RANK + dd, h * LANES + MLA_NOPE + dd] = 1.0
    wz = jnp.concatenate([wq, jnp.asarray(place)], axis=1)
    perm = np.zeros((1024, 1024), np.float32)
    for g in range(8):
        for dd in range(MLA_ROPE):
            e = dd % 16
            src = dd + 8 if e < 8 else dd - 8
            perm[g * LANES + MLA_NOPE + src, g * LANES + MLA_NOPE + dd] = -1.0 if e < 8 else 1.0
    wz_rot = wz @ jnp.asarray(perm)
    qn_ext = jnp.pad(q_norm, (0, 64)).reshape(1, 256)

    wkv = w_ukv.reshape(MLA_KV_RANK, MLA_HEADS, MLA_NOPE + MLA_V)
    kn = jnp.pad(wkv[:, :, :MLA_NOPE], ((0, 0), (0, 0), (0, 64))).reshape(MLA_KV_RANK, 512)
    vals = wkv[:, :, MLA_NOPE:]
    vv = jnp.stack([jnp.pad(vals[:, h], ((0, 0), (64, 0) if h % 2 else (0, 64))) for h in range(MLA_HEADS)],
                   axis=1).reshape(MLA_KV_RANK, 512)
    w_kv = jnp.concatenate([kn, vv], axis=1)
    return (w_main, wz.astype(BF16), wz_rot.astype(BF16), qn_ext, w_kv.astype(BF16),
            kv_norm.reshape(1, MLA_KV_RANK))


def _rope_roll(x, cos, sin_signed, half):
    lane = _lane_iota(x.shape)
    rot = jnp.where((lane % (2 * half)) < half,
                    pltpu.roll(x, LANES - half, 1), pltpu.roll(x, half, 1))
    return x * cos + rot * sin_signed


def _in_proj_kernel(*refs, rope):
    if rope:
        (x_ref, a_ref, sh_ref, w_ref, wz_ref, wzr_ref, qn_ref, wkv_ref, kvn_ref,
         ca_ref, sa_ref, cr_ref, sr_ref, cm_ref, sm_ref, *outs) = refs
    else:
        (x_ref, a_ref, sh_ref, w_ref, wz_ref, wzr_ref, qn_ref, wkv_ref, kvn_ref, *outs) = refs
    qm_ref, km_ref, vm_ref, sq_ref, sk_ref, sv_ref, rq_ref, rk_ref, rv_ref, rg_ref = outs

    x = x_ref[0]
    h = x * lax.rsqrt(jnp.mean(x * x, axis=-1, keepdims=True) + NORM_EPS) * a_ref[0] + sh_ref[0]
    p = _dot(h.astype(BF16), w_ref[...])

    def put(ref, col, width, tables=None, half=None):
        for g in range(width // LANES):
            blk = p[:, col + g * LANES: col + (g + 1) * LANES]
            if tables is not None:
                blk = _rope_roll(blk, tables[0][...], tables[1][...], half)
            ref[0, :, g * LANES:(g + 1) * LANES] = blk.astype(ref.dtype)

    axial = (ca_ref, sa_ref) if rope else None
    flat = (cr_ref, sr_ref) if rope else None
    put(sq_ref, C_SQ, 512, axial, 16)
    put(sk_ref, C_SK, 256, axial, 16)
    put(sv_ref, C_SV, 256)
    put(rq_ref, C_RQ, 256, flat, 32)
    put(rk_ref, C_RK, 256, flat, 32)
    put(rv_ref, C_RV, 256)
    put(rg_ref, C_RG, 256)

    ext = p[:, C_EXT:C_EXT + 256]
    lane = _lane_iota(ext.shape)
    is_cq = lane < MLA_Q_RANK
    cq_sq = jnp.where(is_cq, ext * ext, 0.0)
    inv = lax.rsqrt(jnp.sum(cq_sq, axis=-1, keepdims=True) * (1.0 / MLA_Q_RANK) + NORM_EPS)
    z = jnp.where(is_cq, ext * inv * qn_ref[...], ext).astype(BF16)
    zw = _dot(z, wz_ref[...])
    ckv = p[:, C_CKV:C_CKV + MLA_KV_RANK]
    ckv = ckv * lax.rsqrt(jnp.mean(ckv * ckv, axis=-1, keepdims=True) + NORM_EPS) * kvn_ref[...]
    kv = _dot(ckv.astype(BF16), wkv_ref[...])
    if rope:
        zr = _dot(z, wzr_ref[...])
    lane_g = _lane_iota((ext.shape[0], LANES))
    for g in range(MLA_HEADS):
        sl = slice(g * LANES, (g + 1) * LANES)
        sk_ = slice(512 + g * LANES, 512 + (g + 1) * LANES)
        q_g, kpe_g = zw[:, sl], zw[:, sk_]
        if rope:
            q_g = q_g * cm_ref[...] + zr[:, sl] * sm_ref[...]
            kpe_g = kpe_g * cm_ref[...] + zr[:, sk_] * sm_ref[...]
        qm_ref[0, :, sl] = q_g.astype(BF16)
        km_ref[0, :, sl] = (kv[:, sl] + kpe_g).astype(BF16)
        vm_ref[0, :, sl] = jnp.where(lane_g == MLA_ONE_LANE[g % 2], 1.0, kv[:, sk_]).astype(BF16)


def _in_proj(x, a, sh, wts, tables, tm):
    B, L, D = x.shape
    w_main, wz, wzr, qn_ext, w_kv, kvn = wts
    rope = tables is not None
    tm = min(tm, L)
    bm = (lambda b: b) if a.shape[0] == B else (lambda b: 0)
    const = lambda i, b: (0, 0)
    in_specs = [pl.BlockSpec((1, tm, D), lambda i, b: (b, i, 0)),
                pl.BlockSpec((1, 1, D), lambda i, b: (bm(b), 0, 0)),
                pl.BlockSpec((1, 1, D), lambda i, b: (bm(b), 0, 0)),
                pl.BlockSpec(w_main.shape, const), pl.BlockSpec(wz.shape, const),
                pl.BlockSpec(wzr.shape, const), pl.BlockSpec(qn_ext.shape, const),
                pl.BlockSpec(w_kv.shape, const), pl.BlockSpec(kvn.shape, const)]
    args = [x, a, sh, w_main, wz, wzr, qn_ext, w_kv, kvn]
    if rope:
        in_specs += [pl.BlockSpec((tm, LANES), lambda i, b: (i, 0))] * 6
        args += list(tables)
    widths = (512, 512, 512, 512, 256, 256, 256, 256, 256, 256)
    return pl.pallas_call(
        functools.partial(_in_proj_kernel, rope=rope),
        out_shape=[jax.ShapeDtypeStruct((B, L, w), BF16) for w in widths],
        grid=(L // tm, B),
        in_specs=in_specs,
        out_specs=[pl.BlockSpec((1, tm, w), lambda i, b: (b, i, 0)) for w in widths],
        compiler_params=_cparams(("arbitrary", "arbitrary")),
        name="in_proj_rope" if rope else "in_proj_ctx",
    )(*args)


def _mla_kernel(q_ref, k_ref, v_ref, o_ref):
    lane = _lane_iota((q_ref.shape[1], LANES))
    outs = []
    for h in range(MLA_HEADS):
        sl = slice(h * LANES, (h + 1) * LANES)
        s = _dot_nt(q_ref[0, :, sl], k_ref[0, :, sl])
        p = jnp.exp2(s - jnp.max(s, axis=-1, keepdims=True))
        o = _dot(p.astype(BF16), v_ref[0, :, sl])
        one = MLA_ONE_LANE[h % 2]
        outs.append(o * (1.0 / o[:, one:one + 1]))
    for g in range(MLA_HEADS // 2):
        o_ref[0, :, g * LANES:(g + 1) * LANES] = jnp.where(
            lane < 64, outs[2 * g], outs[2 * g + 1]).astype(o_ref.dtype)


def _mla_attention(qm, km, vm, tq):
    B, L, _ = qm.shape
    Lk = km.shape[1]
    tq = min(tq, L)
    return pl.pallas_call(
        _mla_kernel,
        out_shape=jax.ShapeDtypeStruct((B, L, MLA_HEADS * MLA_V), BF16),
        grid=(B, L // tq),
        in_specs=[pl.BlockSpec((1, tq, 512), lambda b, i: (b, i, 0)),
                  pl.BlockSpec((1, Lk, 512), lambda b, i: (b, 0, 0)),
                  pl.BlockSpec((1, Lk, 512), lambda b, i: (b, 0, 0))],
        out_specs=pl.BlockSpec((1, tq, MLA_HEADS * MLA_V), lambda b, i: (b, i, 0)),
        compiler_params=_cparams(("arbitrary", "arbitrary")),
        name="mla_attention",
    )(qm, km, vm)


SWA_TQ = 2 * SWA_BLOCK
SWA_BAND = SWA_TQ + 2 * SWA_BLOCK


def _swa_bias(n_ctx):
    W = SWA_BLOCK
    G = SWA_Q_HEADS // SWA_KV_HEADS
    qq = np.arange(G * SWA_TQ)[:, None] % SWA_TQ
    kk = np.arange(SWA_BAND + n_ctx)[None, :]
    is_ctx = kk >= SWA_BAND
    masks = [is_ctx | (np.abs(kk - shift - qq) <= W) for shift in (0, W, 2 * W)]
    return jnp.asarray(np.where(np.stack(masks), 0.0, NEG_INF), F32)


def _swa_kernel(sink_ref, q_ref, k_ref, v_ref, kc_ref, vc_ref, *rest, banded):
    W = SWA_BLOCK
    i = pl.program_id(1)
    if banded:
        bias_ref, o_ref = rest
        start = jnp.clip(i * SWA_TQ - W, 0, k_ref.shape[1] - SWA_BAND)
        start = pl.multiple_of(start, W)
        k_all = jnp.concatenate([k_ref[0, pl.ds(start, SWA_BAND), :], kc_ref[0]], axis=0)
        v_all = jnp.concatenate([v_ref[0, pl.ds(start, SWA_BAND), :], vc_ref[0]], axis=0)
    else:
        (o_ref,) = rest
        k_all, v_all = kc_ref[0], vc_ref[0]
    tq = q_ref.shape[1]
    G = SWA_Q_HEADS // SWA_KV_HEADS
    row = lax.broadcasted_iota(jnp.int32, (G * tq, 1), 0)
    lo = _lane_iota((tq, LANES)) < 64
    for g in range(SWA_KV_HEADS):
        gs = slice(g * LANES, (g + 1) * LANES)
        k_g, v_g = k_all[:, gs], v_all[:, gs]
        pieces = []
        sink = jnp.zeros((G * tq, 1), F32)
        for jj in range(G // 2):
            j = g * (G // 2) + jj
            q_pair = q_ref[0, :, j * LANES:(j + 1) * LANES]
            zq = jnp.zeros_like(q_pair)
            pieces += [jnp.where(lo, q_pair, zq), jnp.where(lo, zq, q_pair)]
        for hh in range(G):
            sink = jnp.where(row // tq == hh, sink_ref[g * G + hh], sink)
        s = _dot_nt(jnp.concatenate(pieces, axis=0), k_g)
        if banded:
            s = s + bias_ref[0]
        m = jnp.maximum(jnp.max(s, axis=-1, keepdims=True), sink)
        p = jnp.exp2(s - m)
        l = jnp.sum(p, axis=-1, keepdims=True) + jnp.exp2(sink - m)
        o = _dot(p.astype(BF16), v_g) * (1.0 / l)
        for jj in range(G // 2):
            j = g * (G // 2) + jj
            o_ref[0, :, j * LANES:(j + 1) * LANES] = jnp.where(
                lo, o[2 * jj * tq:(2 * jj + 1) * tq], o[(2 * jj + 1) * tq:(2 * jj + 2) * tq]).astype(o_ref.dtype)


def _swa_attention(sink, q, k, v, kc, vc, banded):
    B, L, _ = q.shape
    Lc = kc.shape[1]
    tq = SWA_TQ if banded else L
    nb = L // tq
    Lkv = k.shape[1]
    in_specs = [pl.BlockSpec(memory_space=pltpu.SMEM),
                pl.BlockSpec((1, tq, 512), lambda b, i: (b, i, 0)),
                pl.BlockSpec((1, Lkv, 256), lambda b, i: (b, 0, 0)),
                pl.BlockSpec((1, Lkv, 256), lambda b, i: (b, 0, 0)),
                pl.BlockSpec((1, Lc, 256), lambda b, i: (b, 0, 0)),
                pl.BlockSpec((1, Lc, 256), lambda b, i: (b, 0, 0))]
    args = [sink, q, k, v, kc, vc]
    if banded:
        assert nb >= 2 and L >= SWA_BAND, "band masks assume distinct first and last query tiles"
        bias = _swa_bias(Lc)
        in_specs.append(pl.BlockSpec((1,) + bias.shape[1:],
                                     lambda b, i: (jnp.where(i == 0, 0, jnp.where(i == nb - 1, 2, 1)), 0, 0)))
        args.append(bias)
    return pl.pallas_call(
        functools.partial(_swa_kernel, banded=banded),
        out_shape=jax.ShapeDtypeStruct((B, L, 512), BF16),
        grid=(B, nb),
        in_specs=in_specs,
        out_specs=pl.BlockSpec((1, tq, 512), lambda b, i: (b, i, 0)),
        compiler_params=_cparams(("arbitrary", "arbitrary")),
        name="swa_banded" if banded else "swa_context",
    )(*args)


def _ret_tables(decay_f, decay_b):
    C = RET_CHUNK
    lg_f = jnp.log(jax.nn.sigmoid(decay_f.astype(F32)))
    lg_b = jnp.log(jax.nn.sigmoid(decay_b.astype(F32)))
    idx = jnp.arange(C, dtype=F32)
    diff = idx[:, None] - idx[None, :]
    intra = (jnp.where(diff >= 0, jnp.exp(lg_f[:, None, None] * jnp.maximum(diff, 0.0)), 0.0)
             + jnp.where(diff <= 0, jnp.exp(lg_b[:, None, None] * jnp.maximum(-diff, 0.0)), 0.0))
    lanes = lambda t: jnp.repeat(t.T, RET_DK, axis=1)
    qdf = lanes(jnp.exp(lg_f[:, None] * (idx + 1.0)))
    qdb = lanes(jnp.exp(lg_b[:, None] * (C - idx)))
    kdf = lanes(jnp.exp(lg_f[:, None] * (C - 1.0 - idx)))
    kdb = lanes(jnp.exp(lg_b[:, None] * idx))
    cdf = jnp.repeat(jnp.exp(lg_f * C), RET_DV).reshape(1, -1)
    cdb = jnp.repeat(jnp.exp(lg_b * C), RET_DV).reshape(1, -1)
    return intra, qdf, qdb, kdf, kdb, cdf, cdb


def _ret_kernel(q_ref, k_ref, v_ref, g_ref, qx_ref, kx_ref, vx_ref, gx_ref,
                d_ref, qdf_ref, qdb_ref, kdf_ref, kdb_ref, cdf_ref, cdb_ref,
                *rest, ctx_out):
    if ctx_out:
        o_ref, ox_ref, sf, sb, sfx, sbx = rest
    else:
        o_ref, sf, sb, sfx, sbx = rest
        ox_ref = None
    C = RET_CHUNK
    nc = q_ref.shape[1] // C
    ncx = qx_ref.shape[1] // C
    NG = RET_HEADS // 2
    r = lax.broadcasted_iota(jnp.int32, (LANES, LANES), 0)
    cidx = lax.broadcasted_iota(jnp.int32, (LANES, LANES), 1)
    blockdiag = (r // 64) == (cidx // 64)
    lane = _lane_iota((C, LANES))
    lo = lane < 64

    def kv_sum(kr, vr, c0, kd_ref, j):
        gs = slice(j * LANES, (j + 1) * LANES)
        kd = (kr[0, pl.ds(c0, C), gs].astype(F32) * kd_ref[:, gs]).astype(BF16)
        return jnp.where(blockdiag, _dot_tn(kd, vr[0, pl.ds(c0, C), gs]), 0.0)

    def state_pass(kr, vr, stf, stb, n):
        def body(t, _):
            cf, cb = t, n - 1 - t
            f0, b0 = pl.multiple_of(cf * C, C), pl.multiple_of(cb * C, C)
            for j in range(NG):
                gs = slice(j * LANES, (j + 1) * LANES)
                stf[cf + 1, j] = stf[cf, j] * cdf_ref[:, gs] + kv_sum(kr, vr, f0, kdf_ref, j)
                stb[cb, j] = stb[cb + 1, j] * cdb_ref[:, gs] + kv_sum(kr, vr, b0, kdb_ref, j)
            return 0
        lax.fori_loop(0, n, body, 0, unroll=2)

    zero = jnp.zeros((NG, LANES, LANES), F32)
    sfx[0] = zero
    sbx[ncx] = zero
    state_pass(kx_ref, vx_ref, sfx, sbx, ncx)
    sf[0] = sfx[ncx]
    sb[nc] = sbx[0]
    state_pass(k_ref, v_ref, sf, sb, nc)

    def out_pass(qr, kr, vr, gr, orf, stf, stb, n):
        def body(c, _):
            c0 = pl.multiple_of(c * C, C)
            for j in range(NG):
                gs = slice(j * LANES, (j + 1) * LANES)
                qg, kg, vg = qr[0, pl.ds(c0, C), gs], kr[0, pl.ds(c0, C), gs], vr[0, pl.ds(c0, C), gs]
                halves = []
                for half in range(2):
                    zq = jnp.zeros_like(qg)
                    qh = jnp.where(lo, qg, zq) if half == 0 else jnp.where(lo, zq, qg)
                    att = _dot_nt(qh, kg) * d_ref[2 * j + half]
                    halves.append(_dot(att.astype(BF16), vg))
                o = jnp.where(lo, halves[0], halves[1])
                qf = qg.astype(F32)
                qd = jnp.concatenate([(qf * qdf_ref[:, gs]).astype(BF16),
                                      (qf * qdb_ref[:, gs]).astype(BF16)], axis=1)
                s_cat = jnp.concatenate([stf[c, j], stb[c + 1, j]], axis=0).astype(BF16)
                o = o + _dot(qd, s_cat)
                o2 = o * o
                ms = jnp.where(lo, jnp.sum(jnp.where(lo, o2, 0.0), axis=-1, keepdims=True),
                               jnp.sum(jnp.where(lo, 0.0, o2), axis=-1, keepdims=True)) * (1.0 / RET_DV)
                gate = gr[0, pl.ds(c0, C), gs].astype(F32)
                y = o * lax.rsqrt(ms + NORM_EPS) * (gate * jax.nn.sigmoid(gate))
                orf[0, pl.ds(c0, C), gs] = y.astype(orf.dtype)
            return 0
        lax.fori_loop(0, n, body, 0, unroll=2)

    out_pass(q_ref, k_ref, v_ref, g_ref, o_ref, sf, sb, nc)
    if ctx_out:
        out_pass(qx_ref, kx_ref, vx_ref, gx_ref, ox_ref, sfx, sbx, ncx)


def _retention(q, k, v, g, qx, kx, vx, gx, tabs, ctx_out):
    B, L, W = q.shape
    Lc = qx.shape[1]
    nc, ncx = L // RET_CHUNK, Lc // RET_CHUNK
    lat = pl.BlockSpec((1, L, W), lambda b: (b, 0, 0))
    cx = pl.BlockSpec((1, Lc, W), lambda b: (b, 0, 0))
    full = lambda a: pl.BlockSpec(a.shape, lambda b: (0,) * a.ndim)
    out_shape = [jax.ShapeDtypeStruct((B, L, W), BF16)]
    out_specs = [lat]
    if ctx_out:
        out_shape.append(jax.ShapeDtypeStruct((B, Lc, W), BF16))
        out_specs.append(cx)
    res = pl.pallas_call(
        functools.partial(_ret_kernel, ctx_out=ctx_out),
        out_shape=out_shape,
        grid=(B,),
        in_specs=[lat, lat, lat, lat, cx, cx, cx, cx] + [full(t) for t in tabs],
        out_specs=out_specs,
        scratch_shapes=[pltpu.VMEM((n + 1, RET_HEADS // 2, LANES, LANES), F32) for n in (nc, nc, ncx, ncx)],
        compiler_params=_cparams(("arbitrary",)),
        name="retention",
    )(q, k, v, g, qx, kx, vx, gx, *tabs)
    return (res[0], res[1]) if ctx_out else (res[0], None)


def _swiglu_act(g, u):
    h = 0.5 * g
    return ((h + h * jnp.tanh(h)) * u).astype(BF16)


def _swiglu_chunks(x, wg, wu, wd, bounds):
    acc = None
    for c0, c1 in bounds:
        part = _dot(_swiglu_act(_dot(x, wg[:, c0:c1]), _dot(x, wu[:, c0:c1])), wd[c0:c1, :])
        acc = part if acc is None else acc + part
    return acc


def _chunk_bounds(total, size):
    return tuple((c, min(c + size, total)) for c in range(0, total, size))


def _split_hi_lo(a):
    hi = a.astype(BF16)
    return hi, (a - hi.astype(F32)).astype(BF16)


def _mix_out(oa_ref, ob_ref, oc_ref, x_ref, g1_ref, a2_ref, sh2_ref, wo_ref):
    y = (_dot(oa_ref[0], wo_ref[0:256, :]) + _dot(ob_ref[0], wo_ref[256:768, :])
         + _dot(oc_ref[0], wo_ref[768:1024, :]))
    x1 = x_ref[0] + g1_ref[0] * y
    h2 = x1 * lax.rsqrt(jnp.mean(x1 * x1, axis=-1, keepdims=True) + NORM_EPS) * a2_ref[0] + sh2_ref[0]
    return x1, h2


def _out_ffn_kernel(oa_ref, ob_ref, oc_ref, x_ref, g1_ref, a2_ref, sh2_ref, g2_ref, wo_ref,
                    wg_ref, wu_ref, wd_ref, o_ref, *, bounds):
    x1, h2 = _mix_out(oa_ref, ob_ref, oc_ref, x_ref, g1_ref, a2_ref, sh2_ref, wo_ref)
    o_ref[0] = x1 + g2_ref[0] * _swiglu_chunks(h2.astype(BF16), wg_ref, wu_ref, wd_ref, bounds)


def _out_ffn(oa, ob, oc, x, g1, a2, sh2, g2, wo, wg, wu, wd, tm):
    B, L, D = x.shape
    tm = min(tm, L)
    bm = (lambda b: b) if g1.shape[0] == B else (lambda b: 0)
    tok = lambda w: pl.BlockSpec((1, tm, w), lambda b, i: (b, i, 0))
    vec = pl.BlockSpec((1, 1, D), lambda b, i: (bm(b), 0, 0))
    wspec = lambda w: pl.BlockSpec(w.shape, lambda b, i: (0, 0), pipeline_mode=pl.Buffered(1))
    return pl.pallas_call(
        functools.partial(_out_ffn_kernel, bounds=_chunk_bounds(wg.shape[1], TF_SUB)),
        out_shape=jax.ShapeDtypeStruct((B, L, D), F32),
        grid=(B, L // tm),
        in_specs=[tok(256), tok(512), tok(256), tok(D), vec, vec, vec, vec,
                  wspec(wo), wspec(wg), wspec(wu), wspec(wd)],
        out_specs=tok(D),
        compiler_params=_cparams(("arbitrary", "arbitrary")),
        name="out_proj_dense_ffn",
    )(oa, ob, oc, x, g1, a2, sh2, g2, wo, wg, wu, wd)


def _out_proj_kernel(oa_ref, ob_ref, oc_ref, x_ref, g1_ref, a2_ref, sh2_ref, wo_ref, r_ref, tri_ref,
                     x1_ref, h2_ref, rt_ref, cnt_ref, base_ref):
    @pl.when((pl.program_id(0) == 0) & (pl.program_id(1) == 0))
    def _():
        base_ref[...] = jnp.zeros_like(base_ref)

    x1, h2 = _mix_out(oa_ref, ob_ref, oc_ref, x_ref, g1_ref, a2_ref, sh2_ref, wo_ref)
    x1_ref[0] = x1
    h2_ref[0] = h2
    tm = h2.shape[0]
    h_hi, h_lo = _split_hi_lo(h2)
    prod = _dot(jnp.concatenate([h_hi, h_lo], axis=0), r_ref[...])
    logits = prod[:tm] + pltpu.roll(prod[:tm], LANES - N_EXPERTS, 1) + prod[tm:]
    lane = _lane_iota(logits.shape)
    logits = jnp.where(lane < N_EXPERTS, logits, -jnp.inf)
    v1 = jnp.max(logits, axis=-1, keepdims=True)
    i1 = jnp.min(jnp.where(logits == v1, lane, LANES), axis=-1, keepdims=True)
    rest_l = jnp.where(lane == i1, -jnp.inf, logits)
    v2 = jnp.max(rest_l, axis=-1, keepdims=True)
    i2 = jnp.min(jnp.where(rest_l == v2, lane, LANES), axis=-1, keepdims=True)
    e2 = jnp.exp(v2 - v1)
    w1 = 1.0 / (1.0 + e2)
    w2 = e2 * w1
    oh1 = jnp.where(lane == i1, 1.0, 0.0)
    oh2 = jnp.where(lane == i2, 1.0, 0.0)
    both = oh1 + oh2
    seen = _dot(tri_ref[...], both.astype(BF16)) + base_ref[...]
    rank1 = jnp.sum(oh1 * seen, axis=-1, keepdims=True)
    rank2 = jnp.sum(oh2 * seen, axis=-1, keepdims=True)
    base_ref[...] += jnp.sum(both, axis=0, keepdims=True)
    cnt_ref[...] = base_ref[...]
    vals = (i1.astype(F32), i2.astype(F32), w1, w2, rank1, rank2)
    row = jnp.zeros_like(logits)
    for k, v in enumerate(vals):
        row = jnp.where(lane == k, v, row)
    rt_ref[0] = row


def _out_proj_route(oa, ob, oc, x, g1, a2, sh2, wo, router, tm):
    B, L, D = x.shape
    tm = min(tm, L)
    tok = lambda w: pl.BlockSpec((1, tm, w), lambda b, i: (b, i, 0))
    vec = pl.BlockSpec((1, 1, D), lambda b, i: (b, 0, 0))
    const = lambda a: pl.BlockSpec(a.shape, lambda b, i: (0, 0))
    tri = jnp.asarray(np.tril(np.ones((tm, tm), np.float32), -1), BF16)
    return pl.pallas_call(
        _out_proj_kernel,
        out_shape=[jax.ShapeDtypeStruct((B, L, D), F32), jax.ShapeDtypeStruct((B, L, D), F32),
                   jax.ShapeDtypeStruct((B, L, LANES), F32), jax.ShapeDtypeStruct((1, LANES), F32)],
        grid=(B, L // tm),
        in_specs=[tok(256), tok(512), tok(256), tok(D), vec, vec, vec, const(wo), const(router), const(tri)],
        out_specs=[tok(D), tok(D), tok(LANES), pl.BlockSpec((1, LANES), lambda b, i: (0, 0))],
        scratch_shapes=[pltpu.VMEM((1, LANES), F32)],
        compiler_params=_cparams(("arbitrary", "arbitrary")),
        name="out_proj_route",
    )(oa, ob, oc, x, g1, a2, sh2, wo, router, tri)


ROW_UNROLL = 8
ZERO_ROWS = 256


def _dispatch_kernel(ends_ref, pad_ref, d0_ref, d1_ref, h_ref, wg_ref, wu_ref, wd_ref,
                     o_ref, wgb_ref, wub_ref, wdb_ref, zeros_ref, sem, zsem, *, tm):
    rows = h_ref.shape[0]
    zr = zeros_ref.shape[0]
    for src, dst in ((wg_ref, wgb_ref), (wu_ref, wub_ref), (wd_ref, wdb_ref)):
        dst[...] = src[...].astype(dst.dtype)

    @pl.when(pl.program_id(0) == 0)
    def _():
        zeros_ref[...] = jnp.zeros_like(zeros_ref)

        def clear(row0, part):
            dst = o_ref.at[pl.ds(pl.multiple_of(row0, tm) + part * zr, zr)]
            return pltpu.make_async_copy(zeros_ref, dst, zsem)
        jobs = [(pad_ref[e] > 0, ends_ref[e] - tm) for e in range(N_EXPERTS)]
        total = ends_ref[N_EXPERTS - 1]
        jobs += [(total + t * tm < o_ref.shape[0], total + t * tm) for t in range(N_EXPERTS)]
        for cond, row0 in jobs:
            @pl.when(cond)
            def _():
                for part in range(tm // zr):
                    clear(row0, part).start()
        for cond, row0 in jobs:
            @pl.when(cond)
            def _():
                for part in range(tm // zr):
                    clear(row0, part).wait()

    def start(r, _):
        pltpu.make_async_copy(h_ref.at[pl.ds(r, 1)], o_ref.at[pl.ds(d0_ref[0, 0, r], 1)], sem).start(priority=0)
        pltpu.make_async_copy(h_ref.at[pl.ds(r, 1)], o_ref.at[pl.ds(d1_ref[0, 0, r], 1)], sem).start(priority=1)
        return 0

    lax.fori_loop(0, rows, start, 0, unroll=ROW_UNROLL)
    for _ in range(2):
        pltpu.make_async_copy(h_ref, o_ref.at[pl.ds(0, rows)], sem).wait()


def _dispatch(h, d0, d1, ends, padding, n_slots, tm, td, wg, wu, wd):
    N, D = h.shape
    steps = N // td
    idx_spec = pl.BlockSpec((1, 1, td), lambda i, *_: (i, 0, 0), memory_space=pltpu.SMEM)
    flat = [w.reshape(-1, w.shape[-1]) for w in (wg, wu, wd)]
    w_specs = [pl.BlockSpec((w.shape[0] // steps, w.shape[1]), lambda i, *_: (i, 0)) for w in flat]
    res = pl.pallas_call(
        functools.partial(_dispatch_kernel, tm=tm),
        out_shape=[jax.ShapeDtypeStruct((n_slots, D), h.dtype)]
        + [jax.ShapeDtypeStruct(w.shape, BF16) for w in flat],
        grid_spec=pltpu.PrefetchScalarGridSpec(
            num_scalar_prefetch=2,
            grid=(steps,),
            in_specs=[idx_spec, idx_spec, pl.BlockSpec((td, D), lambda i, *_: (i, 0))] + w_specs,
            out_specs=[pl.BlockSpec(memory_space=pl.ANY)] + w_specs,
            scratch_shapes=[pltpu.VMEM((ZERO_ROWS, D), h.dtype), pltpu.SemaphoreType.DMA,
                            pltpu.SemaphoreType.DMA]),
        compiler_params=_cparams(("arbitrary",)),
        name="moe_dispatch",
    )(ends, padding, d0.reshape(steps, 1, td), d1.reshape(steps, 1, td), h, *flat)
    return res[0], res[1].reshape(wg.shape), res[2].reshape(wu.shape), res[3].reshape(wd.shape)


def _gmm_kernel(te_ref, nu_ref, x_ref, wg_ref, wu_ref, wd_ref, o_ref, *, bounds):
    @pl.when(pl.program_id(0) < nu_ref[0])
    def _():
        o_ref[...] = _swiglu_chunks(x_ref[...].astype(BF16), wg_ref.at[0], wu_ref.at[0], wd_ref.at[0], bounds)

    @pl.when(pl.program_id(0) >= nu_ref[0])
    def _():
        o_ref[...] = jnp.zeros_like(o_ref)


def _grouped_swiglu(xs, tile_expert, n_used, wg, wu, wd, tm):
    P, D = xs.shape
    E, _, F = wg.shape
    resident = lambda shape: pl.BlockSpec((1,) + shape, lambda i, te, nu: (te[i], 0, 0),
                                          pipeline_mode=pl.Buffered(1))
    return pl.pallas_call(
        functools.partial(_gmm_kernel, bounds=_chunk_bounds(F, TF_SUB)),
        out_shape=jax.ShapeDtypeStruct((P, D), F32),
        grid_spec=pltpu.PrefetchScalarGridSpec(
            num_scalar_prefetch=2,
            grid=(P // tm,),
            in_specs=[pl.BlockSpec((tm, D), lambda i, te, nu: (jnp.minimum(i, nu[0] - 1), 0)),
                      resident((D, F)), resident((D, F)), resident((F, D))],
            out_specs=pl.BlockSpec((tm, D), lambda i, te, nu: (i, 0))),
        compiler_params=_cparams(("arbitrary",)),
        name="moe_grouped_swiglu",
    )(tile_expert, n_used, xs, wg, wu, wd)


def _combine_kernel(d0_ref, d1_ref, y_ref, x_ref, g2_ref, rt_ref, fg_ref, o_ref, buf, sem):
    rows = o_ref.shape[1]

    def start(r, _):
        pltpu.make_async_copy(y_ref.at[pl.ds(d0_ref[0, 0, r], 1)], buf.at[0, pl.ds(r, 1)], sem).start(priority=0)
        pltpu.make_async_copy(y_ref.at[pl.ds(d1_ref[0, 0, r], 1)], buf.at[1, pl.ds(r, 1)], sem).start(priority=1)
        return 0

    lax.fori_loop(0, rows, start, 0, unroll=ROW_UNROLL)
    for k in range(2):
        pltpu.make_async_copy(y_ref.at[pl.ds(0, rows)], buf.at[k], sem).wait()
    rt = rt_ref[0]
    moe = rt[:, 2:3] * buf[0] + rt[:, 3:4] * buf[1]
    x2 = x_ref[0] + g2_ref[0] * moe
    o_ref[0] = x2 * lax.rsqrt(jnp.mean(x2 * x2, axis=-1, keepdims=True) + NORM_EPS) * fg_ref[...]


def _combine(y, d0, d1, x1, g2, rt, final_g, tc):
    B, L, D = x1.shape
    tc = min(tc, L)
    nt = L // tc
    idx_spec = pl.BlockSpec((1, 1, tc), lambda b, i: (b * nt + i, 0, 0), memory_space=pltpu.SMEM)
    tok = lambda w: pl.BlockSpec((1, tc, w), lambda b, i: (b, i, 0))
    return pl.pallas_call(
        _combine_kernel,
        out_shape=jax.ShapeDtypeStruct((B, L, D), F32),
        grid=(B, nt),
        in_specs=[idx_spec, idx_spec, pl.BlockSpec(memory_space=pl.ANY), tok(D),
                  pl.BlockSpec((1, 1, D), lambda b, i: (b, 0, 0)), tok(LANES),
                  pl.BlockSpec((1, D), lambda b, i: (0, 0))],
        out_specs=tok(D),
        scratch_shapes=[pltpu.VMEM((2, tc, D), F32), pltpu.SemaphoreType.DMA],
        compiler_params=_cparams(("arbitrary", "arbitrary")),
        name="moe_combine_norm",
    )(d0.reshape(B * nt, 1, tc), d1.reshape(B * nt, 1, tc), y, x1, g2, rt, final_g.reshape(1, D))


def _routing(rt, counts, tm):
    n = rt.shape[0]
    counts = counts[0, :N_EXPERTS].astype(jnp.int32)
    padded = ((counts + tm - 1) // tm) * tm
    ends = jnp.cumsum(padded)
    offs = ends - padded
    e = rt[:, 0:2].astype(jnp.int32)
    onehot = (e[:, :, None] == jnp.arange(N_EXPERTS)[None, None, :]).astype(jnp.int32)
    dest = jnp.sum(onehot * offs[None, None, :], axis=-1) + rt[:, 4:6].astype(jnp.int32)
    n_slots = 2 * n + N_EXPERTS * tm
    tile_start = jnp.arange(n_slots // tm, dtype=jnp.int32) * tm
    n_used = (ends[-1] // tm).astype(jnp.int32)
    te = jnp.sum((tile_start[:, None] >= ends[None, :]).astype(jnp.int32), axis=1)
    last = jnp.sum((ends[-1] - tm >= ends).astype(jnp.int32))
    te = jnp.where(tile_start < ends[-1], te, last).astype(jnp.int32)
    return (n_slots, te, n_used.reshape(1), dest[:, 0], dest[:, 1], ends.astype(jnp.int32),
            (padded - counts).astype(jnp.int32))


TM_IN = 512
TQ_MLA = 256
TM_OUT = 512
TM_FFN = 512
TM_MOE = 512
TF_SUB = 512
TD_DISPATCH = 512
TC_COMBINE = 256


def kernel(x, c, ctx, c_ctx, w_mod, b_mod, norm1_g, norm2_g, w_in, mla_q_norm, mla_w_uq, mla_kv_norm,
           mla_w_ukv, swa_sink, ret_decay_fwd, ret_decay_bwd, w_out, ffn_w_gate, ffn_w_up, ffn_w_down,
           moe_router, moe_w_gate, moe_w_up, moe_w_down, final_norm_g):
    B, L, D = x.shape
    Lc = ctx.shape[1]
    depth = w_mod.shape[0]
    xc = ctx

    cond = jnp.concatenate([c, c_ctx[None], jnp.zeros((16 - B - 1, D), F32)], axis=0)
    mod_all = _modulation(cond, w_mod, b_mod)
    tables = _rope_tables(L)

    for layer in range(depth):
        last = layer == depth - 1
        mod = mod_all[layer].reshape(16, 6, 1, D)
        sh1, sc1, g1, sh2, sc2, g2 = (mod[:B, j] for j in range(6))
        sh1x, sc1x, g1x, sh2x, sc2x, g2x = (mod[B:B + 1, j] for j in range(6))
        n1, n2 = norm1_g[layer], norm2_g[layer]

        wts = _prep_in_weights(w_in[layer], mla_q_norm[layer], mla_w_uq[layer],
                               mla_kv_norm[layer], mla_w_ukv[layer])
        lat = _in_proj(x, n1 * (1.0 + sc1), sh1, wts, tables, TM_IN)
        cx = _in_proj(xc, n1 * (1.0 + sc1x), sh1x, wts, None, TM_IN)
        qm, km, vm, sq, sk, sv, rq, rk, rv, rg = lat
        qmx, kmx, vmx, sqx, skx, svx, rqx, rkx, rvx, rgx = cx

        o_a = _mla_attention(qm, jnp.concatenate([km, kmx], axis=1),
                             jnp.concatenate([vm, vmx], axis=1), TQ_MLA)
        sink = swa_sink[layer].astype(F32) * LOG2E
        o_b = _swa_attention(sink, sq, sk, sv, skx, svx, True)
        rtabs = _ret_tables(ret_decay_fwd[layer], ret_decay_bwd[layer])
        o_c, oc_c = _retention(rq, rk, rv, rg, rqx, rkx, rvx, rgx, rtabs, not last)
        wo = w_out[layer].astype(BF16)

        if layer % 2 == 0:
            i = layer // 2
            wg, wu, wd = (ffn_w_gate[i].astype(BF16), ffn_w_up[i].astype(BF16),
                          ffn_w_down[i].astype(BF16))
            x_next = _out_ffn(o_a, o_b, o_c, x, g1, n2 * (1.0 + sc2), sh2, g2, wo, wg, wu, wd, TM_FFN)
        else:
            i = layer // 2
            r_hi = moe_router[i].astype(BF16)
            r_lo = (moe_router[i] - r_hi.astype(F32)).astype(BF16)
            router = jnp.pad(jnp.concatenate([r_hi, r_lo], axis=1), ((0, 0), (0, LANES - 2 * N_EXPERTS)))
            x1, h2, rt, counts = _out_proj_route(o_a, o_b, o_c, x, g1, n2 * (1.0 + sc2), sh2, wo, router, TM_OUT)
            n_slots, te, n_used, d0, d1, ends, padding = _routing(rt.reshape(B * L, LANES), counts, TM_MOE)
            xs, ewg, ewu, ewd = _dispatch(h2.reshape(B * L, D), d0, d1, ends, padding, n_slots, TM_MOE,
                                          TD_DISPATCH, moe_w_gate[i], moe_w_up[i], moe_w_down[i])
            y = _grouped_swiglu(xs, te, n_used, ewg, ewu, ewd, TM_MOE)
            if last:
                return _combine(y, d0, d1, x1, g2, rt, final_norm_g, TC_COMBINE)
            raise NotImplementedError("expert layer is only supported as the last layer")

        if not last:
            oc_a = _mla_attention(qmx, kmx, vmx, TQ_MLA)
            oc_b = _swa_attention(sink, sqx, skx, svx, skx, svx, False)
            xc = _out_ffn(oc_a, oc_b, oc_c, xc, g1x, n2 * (1.0 + sc2x), sh2x, g2x, wo, wg, wu, wd, TM_FFN)
        x = x_next
    raise NotImplementedError("trunk must end with the expert layer")
```

```python
import functools
import math

import numpy as np
import jax
import jax.numpy as jnp
from jax import lax
from jax.experimental import pallas as pl
from jax.experimental.pallas import tpu as pltpu

F32 = jnp.float32
BF16 = jnp.bfloat16

D_MODEL = 1024
DEPTH = 2
GRID_W = 64
HEAD_DIM = 64
NORM_EPS = 1e-6
ROPE_BASE = 10000.0
NEG_INF = -1e30

MLA_HEADS = 4
MLA_Q_RANK = 192
MLA_KV_RANK = 128
MLA_NOPE = 64
MLA_ROPE = 32
MLA_V = 64

SWA_Q_HEADS = 8
SWA_KV_HEADS = 2
SWA_BLOCK = 128

RET_HEADS = 4
RET_DK = 64
RET_DV = 64
RET_CHUNK = 128

D_FF = 2816
N_EXPERTS = 8
D_FF_EXPERT = 3584

LOG2E = math.log2(math.e)
MLA_ONE_LANE = (64, 0)
LANES = 128
VMEM_LIMIT = 56 * 1024 * 1024

C_SQ, C_SK, C_SV = 0, 512, 768
C_RQ, C_RK, C_RV, C_RG = 1024, 1280, 1536, 1792
C_CKV, C_EXT = 2048, 2176
IN_COLS = 2432


def _cparams(sem, vmem=VMEM_LIMIT):
    return pltpu.CompilerParams(dimension_semantics=sem, vmem_limit_bytes=vmem)


def _dot(a, b):
    return jnp.dot(a, b, preferred_element_type=F32)


def _dot_nt(a, b):
    return lax.dot_general(a, b, (((1,), (1,)), ((), ())), preferred_element_type=F32)


def _dot_tn(a, b):
    return lax.dot_general(a, b, (((0,), (0,)), ((), ())), preferred_element_type=F32)


def _lane_iota(shape):
    return lax.broadcasted_iota(jnp.int32, shape, len(shape) - 1)


def _mod_kernel(c_ref, w_ref, b_ref, o_ref):
    c = c_ref[...]
    c = c * jax.nn.sigmoid(c)
    o_ref[0] = jnp.dot(c, w_ref[0], preferred_element_type=F32,
                       precision=lax.Precision.HIGHEST) + b_ref[0]


def _modulation(cond, w_mod, b_mod):
    depth, d, n = w_mod.shape
    rows = cond.shape[0]
    tn = 1024
    return pl.pallas_call(
        _mod_kernel,
        out_shape=jax.ShapeDtypeStruct((depth, rows, n), F32),
        grid=(depth, n // tn),
        in_specs=[pl.BlockSpec((rows, d), lambda l, j: (0, 0)),
                  pl.BlockSpec((1, d, tn), lambda l, j: (l, 0, j)),
                  pl.BlockSpec((1, 1, tn), lambda l, j: (l, 0, j))],
        out_specs=pl.BlockSpec((1, rows, tn), lambda l, j: (l, 0, j)),
        compiler_params=_cparams(("arbitrary", "arbitrary")),
        name="modulation",
    )(cond, w_mod, b_mod.reshape(depth, 1, n))


def _angles(pos, dim):
    inv = (ROPE_BASE ** (-np.arange(0, dim, 2, dtype=np.float32) / dim)).astype(np.float32)
    ang = pos.astype(np.float32)[:, None] * inv[None, :]
    return np.concatenate([ang, ang], axis=-1).astype(np.float64)


def _rope_tables(length):
    t = np.arange(length)
    rows, cols = t // GRID_W, t % GRID_W
    ar, ac = _angles(rows, 32), _angles(cols, 32)
    sign32 = np.concatenate([-np.ones(16), np.ones(16)])
    cos_a = np.concatenate([np.cos(ar), np.cos(ac)], axis=-1)
    sin_a = np.concatenate([np.sin(ar) * sign32, np.sin(ac) * sign32], axis=-1)
    cos_a, sin_a = np.tile(cos_a, (1, 2)), np.tile(sin_a, (1, 2))
    at = _angles(t, 64)
    sign64 = np.concatenate([-np.ones(32), np.ones(32)])
    cos_r, sin_r = np.tile(np.cos(at), (1, 2)), np.tile(np.sin(at) * sign64, (1, 2))
    mr, mc = _angles(rows, 16), _angles(cols, 16)
    cos_m = np.ones((length, LANES))
    sin_m = np.zeros((length, LANES))
    cos_m[:, 64:96] = np.concatenate([np.cos(mr), np.cos(mc)], axis=-1)
    sin_m[:, 64:96] = np.concatenate([np.sin(mr), np.sin(mc)], axis=-1)
    return tuple(jnp.asarray(a, F32) for a in (cos_a, sin_a, cos_r, sin_r, cos_m, sin_m))


def _prep_in_weights(w_in, q_norm, w_uq, kv_norm, w_ukv):
    cuts = np.cumsum([MLA_Q_RANK, MLA_KV_RANK, MLA_ROPE, 512, 128, 128, 256, 256, 256, 256])[:-1]
    cq, ckv, kpe, sq, sk, sv, rq, rk, rv, rg = jnp.split(w_in, [int(v) for v in cuts], axis=1)
    dup = lambda w: jnp.concatenate([w[:, :64], w[:, :64], w[:, 64:], w[:, 64:]], axis=1)
    d = w_in.shape[0]
    w_main = jnp.concatenate(
        [sq * (HEAD_DIM ** -0.5 * LOG2E), dup(sk), dup(sv), rq, rk * RET_DK ** -0.5, rv, rg, ckv,
         cq, kpe, jnp.zeros((d, 32), F32)], axis=1).astype(BF16)

    scale = (MLA_NOPE + MLA_ROPE) ** -0.5 * LOG2E
    wq = (w_uq * scale).reshape(MLA_Q_RANK, MLA_HEADS, MLA_NOPE + MLA_ROPE)
    wq = jnp.pad(wq, ((0, 64), (0, 0), (0, 32))).reshape(256, 512)
    place = np.zeros((256, 512), np.float32)
    for h in range(MLA_HEADS):
        for dd in range(MLA_ROPE):
            place[MLA_Q_RANK + dd, h * LANES + MLA_NOPE + dd] = 1.0
    wz = jnp.concatenate([wq, jnp.asarray(place)], axis=1)
    perm = np.zeros((1024, 1024), np.float32)
    for g in range(8):
        for dd in range(MLA_ROPE):
            e = dd % 16
            src = dd + 8 if e < 8 else dd - 8
            perm[g * LANES + MLA_NOPE + src, g * LANES + MLA_NOPE + dd] = -1.0 if e < 8 else 1.0
    wz_rot = wz @ jnp.asarray(perm)
    qn_ext = jnp.pad(q_norm, (0, 64)).reshape(1, 256)

    wkv = w_ukv.reshape(MLA_KV_RANK, MLA_HEADS, MLA_NOPE + MLA_V)
    kn = jnp.pad(wkv[:, :, :MLA_NOPE], ((0, 0), (0, 0), (0, 64))).reshape(MLA_KV_RANK, 512)
    vals = wkv[:, :, MLA_NOPE:]
    vv = jnp.stack([jnp.pad(vals[:, h], ((0, 0), (64, 0) if h % 2 else (0, 64))) for h in range(MLA_HEADS)],
                   axis=1).reshape(MLA_KV_RANK, 512)
    w_kv = jnp.concatenate([kn, vv], axis=1)
    return (w_main, wz.astype(BF16), wz_rot.astype(BF16), qn_ext, w_kv.astype(BF16),
            kv_norm.reshape(1, MLA_KV_RANK))


def _rope_roll(x, cos, sin_signed, half):
    lane = _lane_iota(x.shape)
    rot = jnp.where((lane % (2 * half)) < half,
                    pltpu.roll(x, LANES - half, 1), pltpu.roll(x, half, 1))
    return x * cos + rot * sin_signed


def _in_proj_kernel(*refs, rope):
    if rope:
        (x_ref, a_ref, sh_ref, w_ref, wz_ref, wzr_ref, qn_ref, wkv_ref, kvn_ref,
         ca_ref, sa_ref, cr_ref, sr_ref, cm_ref, sm_ref, *outs) = refs
    else:
        (x_ref, a_ref, sh_ref, w_ref, wz_ref, wzr_ref, qn_ref, wkv_ref, kvn_ref, *outs) = refs
    qm_ref, km_ref, vm_ref, sq_ref, sk_ref, sv_ref, rq_ref, rk_ref, rv_ref, rg_ref = outs

    x = x_ref[0]
    h = x * lax.rsqrt(jnp.mean(x * x, axis=-1, keepdims=True) + NORM_EPS) * a_ref[0] + sh_ref[0]
    p = _dot(h.astype(BF16), w_ref[...])

    def put(ref, col, width, tables=None, half=None):
        for g in range(width // LANES):
            blk = p[:, col + g * LANES: col + (g + 1) * LANES]
            if tables is not None:
                blk = _rope_roll(blk, tables[0][...], tables[1][...], half)
            ref[0, :, g * LANES:(g + 1) * LANES] = blk.astype(ref.dtype)

    axial = (ca_ref, sa_ref) if rope else None
    flat = (cr_ref, sr_ref) if rope else None
    put(sq_ref, C_SQ, 512, axial, 16)
    put(sk_ref, C_SK, 256, axial, 16)
    put(sv_ref, C_SV, 256)
    put(rq_ref, C_RQ, 256, flat, 32)
    put(rk_ref, C_RK, 256, flat, 32)
    put(rv_ref, C_RV, 256)
    put(rg_ref, C_RG, 256)

    ext = p[:, C_EXT:C_EXT + 256]
    lane = _lane_iota(ext.shape)
    is_cq = lane < MLA_Q_RANK
    cq_sq = jnp.where(is_cq, ext * ext, 0.0)
    inv = lax.rsqrt(jnp.sum(cq_sq, axis=-1, keepdims=True) * (1.0 / MLA_Q_RANK) + NORM_EPS)
    z = jnp.where(is_cq, ext * inv * qn_ref[...], ext).astype(BF16)
    zw = _dot(z, wz_ref[...])
    ckv = p[:, C_CKV:C_CKV + MLA_KV_RANK]
    ckv = ckv * lax.rsqrt(jnp.mean(ckv * ckv, axis=-1, keepdims=True) + NORM_EPS) * kvn_ref[...]
    kv = _dot(ckv.astype(BF16), wkv_ref[...])
    if rope:
        zr = _dot(z, wzr_ref[...])
    lane_g = _lane_iota((ext.shape[0], LANES))
    for g in range(MLA_HEADS):
        sl = slice(g * LANES, (g + 1) * LANES)
        sk_ = slice(512 + g * LANES, 512 + (g + 1) * LANES)
        q_g, kpe_g = zw[:, sl], zw[:, sk_]
        if rope:
            q_g = q_g * cm_ref[...] + zr[:, sl] * sm_ref[...]
            kpe_g = kpe_g * cm_ref[...] + zr[:, sk_] * sm_ref[...]
        qm_ref[0, :, sl] = q_g.astype(BF16)
        km_ref[0, :, sl] = (kv[:, sl] + kpe_g).astype(BF16)
        vm_ref[0, :, sl] = jnp.where(lane_g == MLA_ONE_LANE[g % 2], 1.0, kv[:, sk_]).astype(BF16)


def _in_proj(x, a, sh, wts, tables, tm):
    B, L, D = x.shape
    w_main, wz, wzr, qn_ext, w_kv, kvn = wts
    rope = tables is not None
    tm = min(tm, L)
    bm = (lambda b: b) if a.shape[0] == B else (lambda b: 0)
    const = lambda i, b: (0, 0)
    in_specs = [pl.BlockSpec((1, tm, D), lambda i, b: (b, i, 0)),
                pl.BlockSpec((1, 1, D), lambda i, b: (bm(b), 0, 0)),
                pl.BlockSpec((1, 1, D), lambda i, b: (bm(b), 0, 0)),
                pl.BlockSpec(w_main.shape, const), pl.BlockSpec(wz.shape, const),
                pl.BlockSpec(wzr.shape, const), pl.BlockSpec(qn_ext.shape, const),
                pl.BlockSpec(w_kv.shape, const), pl.BlockSpec(kvn.shape, const)]
    args = [x, a, sh, w_main, wz, wzr, qn_ext, w_kv, kvn]
    if rope:
        in_specs += [pl.BlockSpec((tm, LANES), lambda i, b: (i, 0))] * 6
        args += list(tables)
    widths = (512, 512, 512, 512, 256, 256, 256, 256, 256, 256)
    return pl.pallas_call(
        functools.partial(_in_proj_kernel, rope=rope),
        out_shape=[jax.ShapeDtypeStruct((B, L, w), BF16) for w in widths],
        grid=(L // tm, B),
        in_specs=in_specs,
        out_specs=[pl.BlockSpec((1, tm, w), lambda i, b: (b, i, 0)) for w in widths],
        compiler_params=_cparams(("arbitrary", "arbitrary")),
        name="in_proj_rope" if rope else "in_proj_ctx",
    )(*args)


def _mla_kernel(q_ref, *refs):
    *kv_refs, o_ref = refs
    k_refs, v_refs = kv_refs[0::2], kv_refs[1::2]
    lane = _lane_iota((q_ref.shape[1], LANES))
    outs = []
    for h in range(MLA_HEADS):
        sl = slice(h * LANES, (h + 1) * LANES)
        q = q_ref[0, :, sl]
        s = [_dot_nt(q, k_ref[0, :, sl]) for k_ref in k_refs]
        m = functools.reduce(jnp.maximum, [jnp.max(si, axis=-1, keepdims=True) for si in s])
        o = sum(_dot(jnp.exp2(si - m).astype(BF16), v_ref[0, :, sl]) for si, v_ref in zip(s, v_refs))
        one = MLA_ONE_LANE[h % 2]
        outs.append(o * (1.0 / o[:, one:one + 1]))
    for g in range(MLA_HEADS // 2):
        o_ref[0, :, g * LANES:(g + 1) * LANES] = jnp.where(
            lane < 64, outs[2 * g], outs[2 * g + 1]).astype(o_ref.dtype)


def _mla_attention(qm, kvs, tq):
    B, L, _ = qm.shape
    tq = min(tq, L)
    flat = [a for kv in kvs for a in kv]
    return pl.pallas_call(
        _mla_kernel,
        out_shape=jax.ShapeDtypeStruct((B, L, MLA_HEADS * MLA_V), BF16),
        grid=(B, L // tq),
        in_specs=[pl.BlockSpec((1, tq, 512), lambda b, i: (b, i, 0))]
        + [pl.BlockSpec((1,) + a.shape[1:], lambda b, i: (b, 0, 0)) for a in flat],
        out_specs=pl.BlockSpec((1, tq, MLA_HEADS * MLA_V), lambda b, i: (b, i, 0)),
        compiler_params=_cparams(("arbitrary", "arbitrary")),
        name="mla_attention",
    )(qm, *flat)


SWA_TQ = 2 * SWA_BLOCK
SWA_BAND = SWA_TQ + 2 * SWA_BLOCK


def _swa_bias(n_ctx):
    W = SWA_BLOCK
    G = SWA_Q_HEADS // SWA_KV_HEADS
    qq = np.arange(G * SWA_TQ)[:, None] % SWA_TQ
    kk = np.arange(SWA_BAND + n_ctx)[None, :]
    is_ctx = kk >= SWA_BAND
    masks = [is_ctx | (np.abs(kk - shift - qq) <= W) for shift in (0, W, 2 * W)]
    return jnp.asarray(np.where(np.stack(masks), 0.0, NEG_INF), F32)


def _swa_kernel(sink_ref, q_ref, k_ref, v_ref, kc_ref, vc_ref, *rest, banded):
    W = SWA_BLOCK
    i = pl.program_id(1)
    if banded:
        bias_ref, o_ref = rest
        start = jnp.clip(i * SWA_TQ - W, 0, k_ref.shape[1] - SWA_BAND)
        start = pl.multiple_of(start, W)
        k_all = jnp.concatenate([k_ref[0, pl.ds(start, SWA_BAND), :], kc_ref[0]], axis=0)
        v_all = jnp.concatenate([v_ref[0, pl.ds(start, SWA_BAND), :], vc_ref[0]], axis=0)
    else:
        (o_ref,) = rest
        k_all, v_all = kc_ref[0], vc_ref[0]
    tq = q_ref.shape[1]
    G = SWA_Q_HEADS // SWA_KV_HEADS
    row = lax.broadcasted_iota(jnp.int32, (G * tq, 1), 0)
    lo = _lane_iota((tq, LANES)) < 64
    for g in range(SWA_KV_HEADS):
        gs = slice(g * LANES, (g + 1) * LANES)
        k_g, v_g = k_all[:, gs], v_all[:, gs]
        pieces = []
        sink = jnp.zeros((G * tq, 1), F32)
        for jj in range(G // 2):
            j = g * (G // 2) + jj
            q_pair = q_ref[0, :, j * LANES:(j + 1) * LANES]
            zq = jnp.zeros_like(q_pair)
            pieces += [jnp.where(lo, q_pair, zq), jnp.where(lo, zq, q_pair)]
        for hh in range(G):
            sink = jnp.where(row // tq == hh, sink_ref[g * G + hh], sink)
        s = _dot_nt(jnp.concatenate(pieces, axis=0), k_g)
        if banded:
            s = s + bias_ref[0]
        m = jnp.maximum(jnp.max(s, axis=-1, keepdims=True), sink)
        p = jnp.exp2(s - m)
        l = jnp.sum(p, axis=-1, keepdims=True) + jnp.exp2(sink - m)
        o = _dot(p.astype(BF16), v_g) * (1.0 / l)
        for jj in range(G // 2):
            j = g * (G // 2) + jj
            o_ref[0, :, j * LANES:(j + 1) * LANES] = jnp.where(
                lo, o[2 * jj * tq:(2 * jj + 1) * tq], o[(2 * jj + 1) * tq:(2 * jj + 2) * tq]).astype(o_ref.dtype)


def _swa_attention(sink, q, k, v, kc, vc, banded):
    B, L, _ = q.shape
    Lc = kc.shape[1]
    tq = SWA_TQ if banded else L
    nb = L // tq
    Lkv = k.shape[1]
    in_specs = [pl.BlockSpec(memory_space=pltpu.SMEM),
                pl.BlockSpec((1, tq, 512), lambda b, i: (b, i, 0)),
                pl.BlockSpec((1, Lkv, 256), lambda b, i: (b, 0, 0)),
                pl.BlockSpec((1, Lkv, 256), lambda b, i: (b, 0, 0)),
                pl.BlockSpec((1, Lc, 256), lambda b, i: (b, 0, 0)),
                pl.BlockSpec((1, Lc, 256), lambda b, i: (b, 0, 0))]
    args = [sink, q, k, v, kc, vc]
    if banded:
        assert nb >= 2 and L >= SWA_BAND, "band masks assume distinct first and last query tiles"
        bias = _swa_bias(Lc)
        in_specs.append(pl.BlockSpec((1,) + bias.shape[1:],
                                     lambda b, i: (jnp.where(i == 0, 0, jnp.where(i == nb - 1, 2, 1)), 0, 0)))
        args.append(bias)
    return pl.pallas_call(
        functools.partial(_swa_kernel, banded=banded),
        out_shape=jax.ShapeDtypeStruct((B, L, 512), BF16),
        grid=(B, nb),
        in_specs=in_specs,
        out_specs=pl.BlockSpec((1, tq, 512), lambda b, i: (b, i, 0)),
        compiler_params=_cparams(("arbitrary", "arbitrary")),
        name="swa_banded" if banded else "swa_context",
    )(*args)


def _ret_tables(decay_f, decay_b):
    C = RET_CHUNK
    lg_f = jnp.log(jax.nn.sigmoid(decay_f.astype(F32)))
    lg_b = jnp.log(jax.nn.sigmoid(decay_b.astype(F32)))
    idx = jnp.arange(C, dtype=F32)
    diff = idx[:, None] - idx[None, :]
    intra = (jnp.where(diff >= 0, jnp.exp(lg_f[:, None, None] * jnp.maximum(diff, 0.0)), 0.0)
             + jnp.where(diff <= 0, jnp.exp(lg_b[:, None, None] * jnp.maximum(-diff, 0.0)), 0.0))
    lanes = lambda t: jnp.repeat(t.T, RET_DK, axis=1)
    qdf = lanes(jnp.exp(lg_f[:, None] * (idx + 1.0)))
    qdb = lanes(jnp.exp(lg_b[:, None] * (C - idx)))
    kdf = lanes(jnp.exp(lg_f[:, None] * (C - 1.0 - idx)))
    kdb = lanes(jnp.exp(lg_b[:, None] * idx))
    cdf = jnp.repeat(jnp.exp(lg_f * C), RET_DV).reshape(1, -1)
    cdb = jnp.repeat(jnp.exp(lg_b * C), RET_DV).reshape(1, -1)
    return intra, qdf, qdb, kdf, kdb, cdf, cdb


def _ret_kernel(q_ref, k_ref, v_ref, g_ref, qx_ref, kx_ref, vx_ref, gx_ref,
                d_ref, qdf_ref, qdb_ref, kdf_ref, kdb_ref, cdf_ref, cdb_ref,
                *rest, ctx_out):
    if ctx_out:
        o_ref, ox_ref, sf, sb, sfx, sbx = rest
    else:
        o_ref, sf, sb, sfx, sbx = rest
        ox_ref = None
    C = RET_CHUNK
    nc = q_ref.shape[1] // C
    ncx = qx_ref.shape[1] // C
    NG = RET_HEADS // 2
    r = lax.broadcasted_iota(jnp.int32, (LANES, LANES), 0)
    cidx = lax.broadcasted_iota(jnp.int32, (LANES, LANES), 1)
    blockdiag = (r // 64) == (cidx // 64)
    lane = _lane_iota((C, LANES))
    lo = lane < 64

    def state_pass(kr, vr, stf, stb, n):
        def sums(c, _):
            c0 = pl.multiple_of(c * C, C)
            for j in range(NG):
                gs = slice(j * LANES, (j + 1) * LANES)
                kf = kr[0, pl.ds(c0, C), gs].astype(F32)
                kd = jnp.concatenate([(kf * kdf_ref[:, gs]).astype(BF16),
                                      (kf * kdb_ref[:, gs]).astype(BF16)], axis=1)
                kv = _dot_tn(kd, vr[0, pl.ds(c0, C), gs])
                stf[c + 1, j] = jnp.where(blockdiag, kv[:LANES], 0.0)
                stb[c, j] = jnp.where(blockdiag, kv[LANES:], 0.0)
            return 0
        lax.fori_loop(0, n, sums, 0, unroll=min(4, n))

        def scan(t, _):
            cf, cb = t, n - 1 - t
            for j in range(NG):
                gs = slice(j * LANES, (j + 1) * LANES)
                stf[cf + 1, j] = stf[cf, j] * cdf_ref[:, gs] + stf[cf + 1, j]
                stb[cb, j] = stb[cb + 1, j] * cdb_ref[:, gs] + stb[cb, j]
            return 0
        lax.fori_loop(0, n, scan, 0, unroll=min(4, n))

    zero = jnp.zeros((NG, LANES, LANES), F32)
    sfx[0] = zero
    sbx[ncx] = zero
    state_pass(kx_ref, vx_ref, sfx, sbx, ncx)
    sf[0] = sfx[ncx]
    sb[nc] = sbx[0]
    state_pass(k_ref, v_ref, sf, sb, nc)

    def out_pass(qr, kr, vr, gr, orf, stf, stb, n):
        def body(c, _):
            c0 = pl.multiple_of(c * C, C)
            for j in range(NG):
                gs = slice(j * LANES, (j + 1) * LANES)
                qg, kg, vg = qr[0, pl.ds(c0, C), gs], kr[0, pl.ds(c0, C), gs], vr[0, pl.ds(c0, C), gs]
                halves = []
                for half in range(2):
                    zq = jnp.zeros_like(qg)
                    qh = jnp.where(lo, qg, zq) if half == 0 else jnp.where(lo, zq, qg)
                    att = _dot_nt(qh, kg) * d_ref[2 * j + half]
                    halves.append(_dot(att.astype(BF16), vg))
                o = jnp.where(lo, halves[0], halves[1])
                qf = qg.astype(F32)
                qd = jnp.concatenate([(qf * qdf_ref[:, gs]).astype(BF16),
                                      (qf * qdb_ref[:, gs]).astype(BF16)], axis=1)
                s_cat = jnp.concatenate([stf[c, j], stb[c + 1, j]], axis=0).astype(BF16)
                o = o + _dot(qd, s_cat)
                o2 = o * o
                ms = jnp.where(lo, jnp.sum(jnp.where(lo, o2, 0.0), axis=-1, keepdims=True),
                               jnp.sum(jnp.where(lo, 0.0, o2), axis=-1, keepdims=True)) * (1.0 / RET_DV)
                gate = gr[0, pl.ds(c0, C), gs].astype(F32)
                y = o * lax.rsqrt(ms + NORM_EPS) * (gate * jax.nn.sigmoid(gate))
                orf[0, pl.ds(c0, C), gs] = y.astype(orf.dtype)
            return 0
        lax.fori_loop(0, n, body, 0, unroll=min(4, n))

    out_pass(q_ref, k_ref, v_ref, g_ref, o_ref, sf, sb, nc)
    if ctx_out:
        out_pass(qx_ref, kx_ref, vx_ref, gx_ref, ox_ref, sfx, sbx, ncx)


def _retention(q, k, v, g, qx, kx, vx, gx, tabs, ctx_out):
    B, L, W = q.shape
    Lc = qx.shape[1]
    nc, ncx = L // RET_CHUNK, Lc // RET_CHUNK
    lat = pl.BlockSpec((1, L, W), lambda b: (b, 0, 0))
    cx = pl.BlockSpec((1, Lc, W), lambda b: (b, 0, 0))
    full = lambda a: pl.BlockSpec(a.shape, lambda b: (0,) * a.ndim)
    out_shape = [jax.ShapeDtypeStruct((B, L, W), BF16)]
    out_specs = [lat]
    if ctx_out:
        out_shape.append(jax.ShapeDtypeStruct((B, Lc, W), BF16))
        out_specs.append(cx)
    res = pl.pallas_call(
        functools.partial(_ret_kernel, ctx_out=ctx_out),
        out_shape=out_shape,
        grid=(B,),
        in_specs=[lat, lat, lat, lat, cx, cx, cx, cx] + [full(t) for t in tabs],
        out_specs=out_specs,
        scratch_shapes=[pltpu.VMEM((n + 1, RET_HEADS // 2, LANES, LANES), F32) for n in (nc, nc, ncx, ncx)],
        compiler_params=_cparams(("arbitrary",)),
        name="retention",
    )(q, k, v, g, qx, kx, vx, gx, *tabs)
    return (res[0], res[1]) if ctx_out else (res[0], None)


def _swiglu_act(g, u):
    h = 0.5 * g
    return ((h + h * jnp.tanh(h)) * u).astype(BF16)


def _swiglu_chunks(x, wg, wu, wd, bounds):
    acc = None
    for c0, c1 in bounds:
        part = _dot(_swiglu_act(_dot(x, wg[:, c0:c1]), _dot(x, wu[:, c0:c1])), wd[c0:c1, :])
        acc = part if acc is None else acc + part
    return acc


def _chunk_bounds(total, size):
    return tuple((c, min(c + size, total)) for c in range(0, total, size))


def _split_hi_lo(a):
    hi = a.astype(BF16)
    return hi, (a - hi.astype(F32)).astype(BF16)


def _mix_out(oa_ref, ob_ref, oc_ref, x_ref, g1_ref, a2_ref, sh2_ref, wo_ref):
    y = (_dot(oa_ref[0], wo_ref[0:256, :]) + _dot(ob_ref[0], wo_ref[256:768, :])
         + _dot(oc_ref[0], wo_ref[768:1024, :]))
    x1 = x_ref[0] + g1_ref[0] * y
    h2 = x1 * lax.rsqrt(jnp.mean(x1 * x1, axis=-1, keepdims=True) + NORM_EPS) * a2_ref[0] + sh2_ref[0]
    return x1, h2


def _out_ffn_kernel(oa_ref, ob_ref, oc_ref, x_ref, g1_ref, a2_ref, sh2_ref, g2_ref, wo_ref,
                    wg_ref, wu_ref, wd_ref, o_ref, *, bounds):
    x1, h2 = _mix_out(oa_ref, ob_ref, oc_ref, x_ref, g1_ref, a2_ref, sh2_ref, wo_ref)
    o_ref[0] = x1 + g2_ref[0] * _swiglu_chunks(h2.astype(BF16), wg_ref, wu_ref, wd_ref, bounds)


def _out_ffn(oa, ob, oc, x, g1, a2, sh2, g2, wo, wg, wu, wd, tm):
    B, L, D = x.shape
    tm = min(tm, L)
    bm = (lambda b: b) if g1.shape[0] == B else (lambda b: 0)
    tok = lambda w: pl.BlockSpec((1, tm, w), lambda b, i: (b, i, 0))
    vec = pl.BlockSpec((1, 1, D), lambda b, i: (bm(b), 0, 0))
    wspec = lambda w: pl.BlockSpec(w.shape, lambda b, i: (0, 0), pipeline_mode=pl.Buffered(1))
    return pl.pallas_call(
        functools.partial(_out_ffn_kernel, bounds=_chunk_bounds(wg.shape[1], TF_SUB)),
        out_shape=jax.ShapeDtypeStruct((B, L, D), F32),
        grid=(B, L // tm),
        in_specs=[tok(256), tok(512), tok(256), tok(D), vec, vec, vec, vec,
                  wspec(wo), wspec(wg), wspec(wu), wspec(wd)],
        out_specs=tok(D),
        compiler_params=_cparams(("arbitrary", "arbitrary")),
        name="out_proj_dense_ffn",
    )(oa, ob, oc, x, g1, a2, sh2, g2, wo, wg, wu, wd)


def _out_proj_kernel(oa_ref, ob_ref, oc_ref, x_ref, g1_ref, a2_ref, sh2_ref, wo_ref, r_ref, tri_ref,
                     x1_ref, h2_ref, rt_ref, cnt_ref, base_ref):
    @pl.when((pl.program_id(0) == 0) & (pl.program_id(1) == 0))
    def _():
        base_ref[...] = jnp.zeros_like(base_ref)

    x1, h2 = _mix_out(oa_ref, ob_ref, oc_ref, x_ref, g1_ref, a2_ref, sh2_ref, wo_ref)
    x1_ref[0] = x1
    h2_ref[0] = h2
    tm = h2.shape[0]
    h_hi, h_lo = _split_hi_lo(h2)
    prod = _dot(jnp.concatenate([h_hi, h_lo], axis=0), r_ref[...])
    logits = prod[:tm] + pltpu.roll(prod[:tm], LANES - N_EXPERTS, 1) + prod[tm:]
    lane = _lane_iota(logits.shape)
    logits = jnp.where(lane < N_EXPERTS, logits, -jnp.inf)
    v1 = jnp.max(logits, axis=-1, keepdims=True)
    i1 = jnp.min(jnp.where(logits == v1, lane, LANES), axis=-1, keepdims=True)
    rest_l = jnp.where(lane == i1, -jnp.inf, logits)
    v2 = jnp.max(rest_l, axis=-1, keepdims=True)
    i2 = jnp.min(jnp.where(rest_l == v2, lane, LANES), axis=-1, keepdims=True)
    e2 = jnp.exp(v2 - v1)
    w1 = 1.0 / (1.0 + e2)
    w2 = e2 * w1
    oh1 = jnp.where(lane == i1, 1.0, 0.0)
    oh2 = jnp.where(lane == i2, 1.0, 0.0)
    both = oh1 + oh2
    seen = _dot(tri_ref[...], both.astype(BF16)) + base_ref[...]
    rank1 = jnp.sum(oh1 * seen, axis=-1, keepdims=True)
    rank2 = jnp.sum(oh2 * seen, axis=-1, keepdims=True)
    base_ref[...] += jnp.sum(both, axis=0, keepdims=True)
    cnt_ref[...] = base_ref[...]
    vals = (i1.astype(F32), i2.astype(F32), w1, w2, rank1, rank2)
    row = jnp.zeros_like(logits)
    for k, v in enumerate(vals):
        row = jnp.where(lane == k, v, row)
    rt_ref[0] = row


def _out_proj_route(oa, ob, oc, x, g1, a2, sh2, wo, router, tm):
    B, L, D = x.shape
    tm = min(tm, L)
    tok = lambda w: pl.BlockSpec((1, tm, w), lambda b, i: (b, i, 0))
    vec = pl.BlockSpec((1, 1, D), lambda b, i: (b, 0, 0))
    const = lambda a: pl.BlockSpec(a.shape, lambda b, i: (0, 0))
    tri = jnp.asarray(np.tril(np.ones((tm, tm), np.float32), -1), BF16)
    return pl.pallas_call(
        _out_proj_kernel,
        out_shape=[jax.ShapeDtypeStruct((B, L, D), F32), jax.ShapeDtypeStruct((B, L, D), F32),
                   jax.ShapeDtypeStruct((B, L, LANES), F32), jax.ShapeDtypeStruct((1, LANES), F32)],
        grid=(B, L // tm),
        in_specs=[tok(256), tok(512), tok(256), tok(D), vec, vec, vec, const(wo), const(router), const(tri)],
        out_specs=[tok(D), tok(D), tok(LANES), pl.BlockSpec((1, LANES), lambda b, i: (0, 0))],
        scratch_shapes=[pltpu.VMEM((1, LANES), F32)],
        compiler_params=_cparams(("arbitrary", "arbitrary")),
        name="out_proj_route",
    )(oa, ob, oc, x, g1, a2, sh2, wo, router, tri)


ROW_UNROLL = 16
ZERO_ROWS = 256


def _dispatch_kernel(ends_ref, pad_ref, d0_ref, d1_ref, h_ref, wg_ref, wu_ref, wd_ref,
                     o_ref, wgb_ref, wub_ref, wdb_ref, zeros_ref, sem, zsem, *, tm):
    rows = h_ref.shape[0]
    zr = zeros_ref.shape[0]
    for src, dst in ((wg_ref, wgb_ref), (wu_ref, wub_ref), (wd_ref, wdb_ref)):
        dst[...] = src[...].astype(dst.dtype)

    @pl.when(pl.program_id(0) == 0)
    def _():
        zeros_ref[...] = jnp.zeros_like(zeros_ref)

        def clear(row0, part):
            dst = o_ref.at[pl.ds(pl.multiple_of(row0, tm) + part * zr, zr)]
            return pltpu.make_async_copy(zeros_ref, dst, zsem)
        jobs = [(pad_ref[e] > 0, ends_ref[e] - tm) for e in range(N_EXPERTS)]
        total = ends_ref[N_EXPERTS - 1]
        jobs += [(total + t * tm < o_ref.shape[0], total + t * tm) for t in range(N_EXPERTS)]
        for cond, row0 in jobs:
            @pl.when(cond)
            def _():
                for part in range(tm // zr):
                    clear(row0, part).start()
        for cond, row0 in jobs:
            @pl.when(cond)
            def _():
                for part in range(tm // zr):
                    clear(row0, part).wait()

    def start(r, _):
        pltpu.make_async_copy(h_ref.at[pl.ds(r, 1)], o_ref.at[pl.ds(d0_ref[0, 0, r], 1)], sem).start(priority=0)
        pltpu.make_async_copy(h_ref.at[pl.ds(r, 1)], o_ref.at[pl.ds(d1_ref[0, 0, r], 1)], sem).start(priority=1)
        return 0

    lax.fori_loop(0, rows, start, 0, unroll=ROW_UNROLL)
    for _ in range(2):
        pltpu.make_async_copy(h_ref, o_ref.at[pl.ds(0, rows)], sem).wait()


def _dispatch(h, d0, d1, ends, padding, n_slots, tm, td, wg, wu, wd):
    N, D = h.shape
    steps = N // td
    idx_spec = pl.BlockSpec((1, 1, td), lambda i, *_: (i, 0, 0), memory_space=pltpu.SMEM)
    flat = [w.reshape(-1, w.shape[-1]) for w in (wg, wu, wd)]
    w_specs = [pl.BlockSpec((w.shape[0] // steps, w.shape[1]), lambda i, *_: (i, 0)) for w in flat]
    res = pl.pallas_call(
        functools.partial(_dispatch_kernel, tm=tm),
        out_shape=[jax.ShapeDtypeStruct((n_slots, D), h.dtype)]
        + [jax.ShapeDtypeStruct(w.shape, BF16) for w in flat],
        grid_spec=pltpu.PrefetchScalarGridSpec(
            num_scalar_prefetch=2,
            grid=(steps,),
            in_specs=[idx_spec, idx_spec, pl.BlockSpec((td, D), lambda i, *_: (i, 0))] + w_specs,
            out_specs=[pl.BlockSpec(memory_space=pl.ANY)] + w_specs,
            scratch_shapes=[pltpu.VMEM((ZERO_ROWS, D), h.dtype), pltpu.SemaphoreType.DMA,
                            pltpu.SemaphoreType.DMA]),
        compiler_params=_cparams(("arbitrary",)),
        name="moe_dispatch",
    )(ends, padding, d0.reshape(steps, 1, td), d1.reshape(steps, 1, td), h, *flat)
    return res[0], res[1].reshape(wg.shape), res[2].reshape(wu.shape), res[3].reshape(wd.shape)


def _gmm_kernel(te_ref, nu_ref, x_ref, wg_ref, wu_ref, wd_ref, o_ref, *, bounds):
    @pl.when(pl.program_id(0) < nu_ref[0])
    def _():
        o_ref[...] = _swiglu_chunks(x_ref[...].astype(BF16), wg_ref.at[0], wu_ref.at[0], wd_ref.at[0], bounds)

    @pl.when(pl.program_id(0) >= nu_ref[0])
    def _():
        o_ref[...] = jnp.zeros_like(o_ref)


def _grouped_swiglu(xs, tile_expert, n_used, wg, wu, wd, tm):
    P, D = xs.shape
    E, _, F = wg.shape
    resident = lambda shape: pl.BlockSpec((1,) + shape, lambda i, te, nu: (te[i], 0, 0),
                                          pipeline_mode=pl.Buffered(1))
    return pl.pallas_call(
        functools.partial(_gmm_kernel, bounds=_chunk_bounds(F, TF_SUB)),
        out_shape=jax.ShapeDtypeStruct((P, D), F32),
        grid_spec=pltpu.PrefetchScalarGridSpec(
            num_scalar_prefetch=2,
            grid=(P // tm,),
            in_specs=[pl.BlockSpec((tm, D), lambda i, te, nu: (jnp.minimum(i, nu[0] - 1), 0)),
                      resident((D, F)), resident((D, F)), resident((F, D))],
            out_specs=pl.BlockSpec((tm, D), lambda i, te, nu: (i, 0))),
        compiler_params=_cparams(("arbitrary",)),
        name="moe_grouped_swiglu",
    )(tile_expert, n_used, xs, wg, wu, wd)


def _combine_kernel(d0_ref, d1_ref, y_ref, x_ref, g2_ref, rt_ref, fg_ref, o_ref, buf, sem):
    rows = o_ref.shape[1]

    def start(r, _):
        pltpu.make_async_copy(y_ref.at[pl.ds(d0_ref[0, 0, r], 1)], buf.at[0, pl.ds(r, 1)], sem).start(priority=0)
        pltpu.make_async_copy(y_ref.at[pl.ds(d1_ref[0, 0, r], 1)], buf.at[1, pl.ds(r, 1)], sem).start(priority=1)
        return 0

    lax.fori_loop(0, rows, start, 0, unroll=ROW_UNROLL)
    for k in range(2):
        pltpu.make_async_copy(y_ref.at[pl.ds(0, rows)], buf.at[k], sem).wait()
    rt = rt_ref[0]
    moe = rt[:, 2:3] * buf[0] + rt[:, 3:4] * buf[1]
    x2 = x_ref[0] + g2_ref[0] * moe
    o_ref[0] = x2 * lax.rsqrt(jnp.mean(x2 * x2, axis=-1, keepdims=True) + NORM_EPS) * fg_ref[...]


def _combine(y, d0, d1, x1, g2, rt, final_g, tc):
    B, L, D = x1.shape
    tc = min(tc, L)
    nt = L // tc
    idx_spec = pl.BlockSpec((1, 1, tc), lambda b, i: (b * nt + i, 0, 0), memory_space=pltpu.SMEM)
    tok = lambda w: pl.BlockSpec((1, tc, w), lambda b, i: (b, i, 0))
    return pl.pallas_call(
        _combine_kernel,
        out_shape=jax.ShapeDtypeStruct((B, L, D), F32),
        grid=(B, nt),
        in_specs=[idx_spec, idx_spec, pl.BlockSpec(memory_space=pl.ANY), tok(D),
                  pl.BlockSpec((1, 1, D), lambda b, i: (b, 0, 0)), tok(LANES),
                  pl.BlockSpec((1, D), lambda b, i: (0, 0))],
        out_specs=tok(D),
        scratch_shapes=[pltpu.VMEM((2, tc, D), F32), pltpu.SemaphoreType.DMA],
        compiler_params=_cparams(("arbitrary", "arbitrary")),
        name="moe_combine_norm",
    )(d0.reshape(B * nt, 1, tc), d1.reshape(B * nt, 1, tc), y, x1, g2, rt, final_g.reshape(1, D))


def _routing(rt, counts, tm):
    n = rt.shape[0]
    counts = counts[0, :N_EXPERTS].astype(jnp.int32)
    padded = ((counts + tm - 1) // tm) * tm
    ends = jnp.cumsum(padded)
    offs = ends - padded
    e = rt[:, 0:2].astype(jnp.int32)
    onehot = (e[:, :, None] == jnp.arange(N_EXPERTS)[None, None, :]).astype(jnp.int32)
    dest = jnp.sum(onehot * offs[None, None, :], axis=-1) + rt[:, 4:6].astype(jnp.int32)
    n_slots = 2 * n + N_EXPERTS * tm
    tile_start = jnp.arange(n_slots // tm, dtype=jnp.int32) * tm
    n_used = (ends[-1] // tm).astype(jnp.int32)
    te = jnp.sum((tile_start[:, None] >= ends[None, :]).astype(jnp.int32), axis=1)
    last = jnp.sum((ends[-1] - tm >= ends).astype(jnp.int32))
    te = jnp.where(tile_start < ends[-1], te, last).astype(jnp.int32)
    return (n_slots, te, n_used.reshape(1), dest[:, 0], dest[:, 1], ends.astype(jnp.int32),
            (padded - counts).astype(jnp.int32))


TM_IN = 512
TQ_MLA = 256
TM_OUT = 512
TM_FFN = 512
TM_MOE = 512
TF_SUB = 512
TD_DISPATCH = 512
TC_COMBINE = 512


def kernel(x, c, ctx, c_ctx, w_mod, b_mod, norm1_g, norm2_g, w_in, mla_q_norm, mla_w_uq, mla_kv_norm,
           mla_w_ukv, swa_sink, ret_decay_fwd, ret_decay_bwd, w_out, ffn_w_gate, ffn_w_up, ffn_w_down,
           moe_router, moe_w_gate, moe_w_up, moe_w_down, final_norm_g):
    B, L, D = x.shape
    Lc = ctx.shape[1]
    depth = w_mod.shape[0]
    xc = ctx

    cond = jnp.concatenate([c, c_ctx[None], jnp.zeros((16 - B - 1, D), F32)], axis=0)
    mod_all = _modulation(cond, w_mod, b_mod)
    tables = _rope_tables(L)

    for layer in range(depth):
        last = layer == depth - 1
        mod = mod_all[layer].reshape(16, 6, 1, D)
        sh1, sc1, g1, sh2, sc2, g2 = (mod[:B, j] for j in range(6))
        sh1x, sc1x, g1x, sh2x, sc2x, g2x = (mod[B:B + 1, j] for j in range(6))
        n1, n2 = norm1_g[layer], norm2_g[layer]

        wts = _prep_in_weights(w_in[layer], mla_q_norm[layer], mla_w_uq[layer],
                               mla_kv_norm[layer], mla_w_ukv[layer])
        lat = _in_proj(x, n1 * (1.0 + sc1), sh1, wts, tables, TM_IN)
        cx = _in_proj(xc, n1 * (1.0 + sc1x), sh1x, wts, None, TM_IN)
        qm, km, vm, sq, sk, sv, rq, rk, rv, rg = lat
        qmx, kmx, vmx, sqx, skx, svx, rqx, rkx, rvx, rgx = cx

        o_a = _mla_attention(qm, [(km, vm), (kmx, vmx)], TQ_MLA)
        sink = swa_sink[layer].astype(F32) * LOG2E
        o_b = _swa_attention(sink, sq, sk, sv, skx, svx, True)
        rtabs = _ret_tables(ret_decay_fwd[layer], ret_decay_bwd[layer])
        o_c, oc_c = _retention(rq, rk, rv, rg, rqx, rkx, rvx, rgx, rtabs, not last)
        wo = w_out[layer].astype(BF16)

        if layer % 2 == 0:
            i = layer // 2
            wg, wu, wd = (ffn_w_gate[i].astype(BF16), ffn_w_up[i].astype(BF16),
                          ffn_w_down[i].astype(BF16))
            x_next = _out_ffn(o_a, o_b, o_c, x, g1, n2 * (1.0 + sc2), sh2, g2, wo, wg, wu, wd, TM_FFN)
        else:
            i = layer // 2
            r_hi = moe_router[i].astype(BF16)
            r_lo = (moe_router[i] - r_hi.astype(F32)).astype(BF16)
            router = jnp.pad(jnp.concatenate([r_hi, r_lo], axis=1), ((0, 0), (0, LANES - 2 * N_EXPERTS)))
            x1, h2, rt, counts = _out_proj_route(o_a, o_b, o_c, x, g1, n2 * (1.0 + sc2), sh2, wo, router, TM_OUT)
            n_slots, te, n_used, d0, d1, ends, padding = _routing(rt.reshape(B * L, LANES), counts, TM_MOE)
            xs, ewg, ewu, ewd = _dispatch(h2.reshape(B * L, D), d0, d1, ends, padding, n_slots, TM_MOE,
                                          TD_DISPATCH, moe_w_gate[i], moe_w_up[i], moe_w_down[i])
            y = _grouped_swiglu(xs, te, n_used, ewg, ewu, ewd, TM_MOE)
            if last:
                return _combine(y, d0, d1, x1, g2, rt, final_norm_g, TC_COMBINE)
            raise NotImplementedError("expert layer is only supported as the last layer")

        if not last:
            oc_a = _mla_attention(qmx, [(kmx, vmx)], TQ_MLA)
            oc_b = _swa_attention(sink, sqx, skx, svx, skx, svx, False)
            xc = _out_ffn(oc_a, oc_b, oc_c, xc, g1x, n2 * (1.0 + sc2x), sh2x, g2x, wo, wg, wu, wd, TM_FFN)
        x = x_next
    raise NotImplementedError("trunk must end with the expert layer")
```

```python
import functools
import math

import numpy as np
import jax
import jax.numpy as jnp
from jax import lax
from jax.experimental import pallas as pl
from jax.experimental.pallas import tpu as pltpu

F32 = jnp.float32
BF16 = jnp.bfloat16

D_MODEL = 1024
DEPTH = 2
GRID_W = 64
HEAD_DIM = 64
NORM_EPS = 1e-6
ROPE_BASE = 10000.0
NEG_INF = -1e30

MLA_HEADS = 4
MLA_Q_RANK = 192
MLA_KV_RANK = 128
MLA_NOPE = 64
MLA_ROPE = 32
MLA_V = 64

SWA_Q_HEADS = 8
SWA_KV_HEADS = 2
SWA_BLOCK = 128

RET_HEADS = 4
RET_DK = 64
RET_DV = 64
RET_CHUNK = 128

D_FF = 2816
N_EXPERTS = 8
D_FF_EXPERT = 3584

LOG2E = math.log2(math.e)
MLA_ONE_LANE = (64, 0)
LANES = 128
VMEM_LIMIT = 56 * 1024 * 1024

C_SQ, C_SK, C_SV = 0, 512, 768
C_RQ, C_RK, C_RV, C_RG = 1024, 1280, 1536, 1792
C_CKV, C_EXT = 2048, 2176
IN_COLS = 2432


def _cparams(sem, vmem=VMEM_LIMIT):
    return pltpu.CompilerParams(dimension_semantics=sem, vmem_limit_bytes=vmem)


def _dot(a, b):
    return jnp.dot(a, b, preferred_element_type=F32)


def _dot_nt(a, b):
    return lax.dot_general(a, b, (((1,), (1,)), ((), ())), preferred_element_type=F32)


def _dot_tn(a, b):
    return lax.dot_general(a, b, (((0,), (0,)), ((), ())), preferred_element_type=F32)


def _lane_iota(shape):
    return lax.broadcasted_iota(jnp.int32, shape, len(shape) - 1)


def _mod_kernel(c_ref, w_ref, b_ref, o_ref):
    c = c_ref[...]
    c = c * jax.nn.sigmoid(c)
    o_ref[0] = jnp.dot(c, w_ref[0], preferred_element_type=F32,
                       precision=lax.Precision.HIGHEST) + b_ref[0]


def _modulation(cond, w_mod, b_mod):
    depth, d, n = w_mod.shape
    rows = cond.shape[0]
    tn = 1024
    return pl.pallas_call(
        _mod_kernel,
        out_shape=jax.ShapeDtypeStruct((depth, rows, n), F32),
        grid=(depth, n // tn),
        in_specs=[pl.BlockSpec((rows, d), lambda l, j: (0, 0)),
                  pl.BlockSpec((1, d, tn), lambda l, j: (l, 0, j)),
                  pl.BlockSpec((1, 1, tn), lambda l, j: (l, 0, j))],
        out_specs=pl.BlockSpec((1, rows, tn), lambda l, j: (l, 0, j)),
        compiler_params=_cparams(("arbitrary", "arbitrary")),
        name="modulation",
    )(cond, w_mod, b_mod.reshape(depth, 1, n))


def _angles(pos, dim):
    inv = (ROPE_BASE ** (-np.arange(0, dim, 2, dtype=np.float32) / dim)).astype(np.float32)
    ang = pos.astype(np.float32)[:, None] * inv[None, :]
    return np.concatenate([ang, ang], axis=-1).astype(np.float64)


def _rope_tables(length):
    t = np.arange(length)
    rows, cols = t // GRID_W, t % GRID_W
    ar, ac = _angles(rows, 32), _angles(cols, 32)
    sign32 = np.concatenate([-np.ones(16), np.ones(16)])
    cos_a = np.concatenate([np.cos(ar), np.cos(ac)], axis=-1)
    sin_a = np.concatenate([np.sin(ar) * sign32, np.sin(ac) * sign32], axis=-1)
    cos_a, sin_a = np.tile(cos_a, (1, 2)), np.tile(sin_a, (1, 2))
    at = _angles(t, 64)
    sign64 = np.concatenate([-np.ones(32), np.ones(32)])
    cos_r, sin_r = np.tile(np.cos(at), (1, 2)), np.tile(np.sin(at) * sign64, (1, 2))
    mr, mc = _angles(rows, 16), _angles(cols, 16)
    cos_m = np.ones((length, LANES))
    sin_m = np.zeros((length, LANES))
    cos_m[:, 64:96] = np.concatenate([np.cos(mr), np.cos(mc)], axis=-1)
    sin_m[:, 64:96] = np.concatenate([np.sin(mr), np.sin(mc)], axis=-1)
    return tuple(jnp.asarray(a, F32) for a in (cos_a, sin_a, cos_r, sin_r, cos_m, sin_m))


def _prep_in_weights(w_in, q_norm, w_uq, kv_norm, w_ukv):
    cuts = np.cumsum([MLA_Q_RANK, MLA_KV_RANK, MLA_ROPE, 512, 128, 128, 256, 256, 256, 256])[:-1]
    cq, ckv, kpe, sq, sk, sv, rq, rk, rv, rg = jnp.split(w_in, [int(v) for v in cuts], axis=1)
    dup = lambda w: jnp.concatenate([w[:, :64], w[:, :64], w[:, 64:], w[:, 64:]], axis=1)
    d = w_in.shape[0]
    w_main = jnp.concatenate(
        [sq * (HEAD_DIM ** -0.5 * LOG2E), dup(sk), dup(sv), rq, rk * RET_DK ** -0.5, rv, rg, ckv,
         cq, kpe, jnp.zeros((d, 32), F32)], axis=1).astype(BF16)

    scale = (MLA_NOPE + MLA_ROPE) ** -0.5 * LOG2E
    wq = (w_uq * scale).reshape(MLA_Q_RANK, MLA_HEADS, MLA_NOPE + MLA_ROPE)
    wq = jnp.pad(wq, ((0, 64), (0, 0), (0, 32))).reshape(256, 512)
    place = np.zeros((256, 512), np.float32)
    for h in range(MLA_HEADS):
        for dd in range(MLA_ROPE):
            place[MLA_Q_RANK + dd, h * LANES + MLA_NOPE + dd] = 1.0
    wz = jnp.concatenate([wq, jnp.asarray(place)], axis=1)
    perm = np.zeros((1024, 1024), np.float32)
    for g in range(8):
        for dd in range(MLA_ROPE):
            e = dd % 16
            src = dd + 8 if e < 8 else dd - 8
            perm[g * LANES + MLA_NOPE + src, g * LANES + MLA_NOPE + dd] = -1.0 if e < 8 else 1.0
    wz_rot = wz @ jnp.asarray(perm)
    qn_ext = jnp.pad(q_norm, (0, 64)).reshape(1, 256)

    wkv = w_ukv.reshape(MLA_KV_RANK, MLA_HEADS, MLA_NOPE + MLA_V)
    kn = jnp.pad(wkv[:, :, :MLA_NOPE], ((0, 0), (0, 0), (0, 64))).reshape(MLA_KV_RANK, 512)
    vals = wkv[:, :, MLA_NOPE:]
    vv = jnp.stack([jnp.pad(vals[:, h], ((0, 0), (64, 0) if h % 2 else (0, 64))) for h in range(MLA_HEADS)],
                   axis=1).reshape(MLA_KV_RANK, 512)
    w_kv = jnp.concatenate([kn, vv], axis=1)
    return (w_main, wz.astype(BF16), wz_rot.astype(BF16), qn_ext, w_kv.astype(BF16),
            kv_norm.reshape(1, MLA_KV_RANK))


def _rope_roll(x, cos, sin_signed, half):
    lane = _lane_iota(x.shape)
    rot = jnp.where((lane % (2 * half)) < half,
                    pltpu.roll(x, LANES - half, 1), pltpu.roll(x, half, 1))
    return x * cos + rot * sin_signed


def _in_proj_kernel(*refs, rope):
    if rope:
        (x_ref, a_ref, sh_ref, w_ref, wz_ref, wzr_ref, qn_ref, wkv_ref, kvn_ref,
         ca_ref, sa_ref, cr_ref, sr_ref, cm_ref, sm_ref, *outs) = refs
    else:
        (x_ref, a_ref, sh_ref, w_ref, wz_ref, wzr_ref, qn_ref, wkv_ref, kvn_ref, *outs) = refs
    qm_ref, km_ref, vm_ref, sq_ref, sk_ref, sv_ref, rq_ref, rk_ref, rv_ref, rg_ref = outs

    x = x_ref[0]
    h = x * lax.rsqrt(jnp.mean(x * x, axis=-1, keepdims=True) + NORM_EPS) * a_ref[0] + sh_ref[0]
    p = _dot(h.astype(BF16), w_ref[...])

    def put(ref, col, width, tables=None, half=None):
        for g in range(width // LANES):
            blk = p[:, col + g * LANES: col + (g + 1) * LANES]
            if tables is not None:
                blk = _rope_roll(blk, tables[0][...], tables[1][...], half)
            ref[0, :, g * LANES:(g + 1) * LANES] = blk.astype(ref.dtype)

    axial = (ca_ref, sa_ref) if rope else None
    flat = (cr_ref, sr_ref) if rope else None
    put(sq_ref, C_SQ, 512, axial, 16)
    put(sk_ref, C_SK, 256, axial, 16)
    put(sv_ref, C_SV, 256)
    put(rq_ref, C_RQ, 256, flat, 32)
    put(rk_ref, C_RK, 256, flat, 32)
    put(rv_ref, C_RV, 256)
    put(rg_ref, C_RG, 256)

    ext = p[:, C_EXT:C_EXT + 256]
    lane = _lane_iota(ext.shape)
    is_cq = lane < MLA_Q_RANK
    cq_sq = jnp.where(is_cq, ext * ext, 0.0)
    inv = lax.rsqrt(jnp.sum(cq_sq, axis=-1, keepdims=True) * (1.0 / MLA_Q_RANK) + NORM_EPS)
    z = jnp.where(is_cq, ext * inv * qn_ref[...], ext).astype(BF16)
    zw = _dot(z, wz_ref[...])
    ckv = p[:, C_CKV:C_CKV + MLA_KV_RANK]
    ckv = ckv * lax.rsqrt(jnp.mean(ckv * ckv, axis=-1, keepdims=True) + NORM_EPS) * kvn_ref[...]
    kv = _dot(ckv.astype(BF16), wkv_ref[...])
    if rope:
        zr = _dot(z, wzr_ref[...])
    lane_g = _lane_iota((ext.shape[0], LANES))
    for g in range(MLA_HEADS):
        sl = slice(g * LANES, (g + 1) * LANES)
        sk_ = slice(512 + g * LANES, 512 + (g + 1) * LANES)
        q_g, kpe_g = zw[:, sl], zw[:, sk_]
        if rope:
            q_g = q_g * cm_ref[...] + zr[:, sl] * sm_ref[...]
            kpe_g = kpe_g * cm_ref[...] + zr[:, sk_] * sm_ref[...]
        qm_ref[0, :, sl] = q_g.astype(BF16)
        km_ref[0, :, sl] = (kv[:, sl] + kpe_g).astype(BF16)
        vm_ref[0, :, sl] = jnp.where(lane_g == MLA_ONE_LANE[g % 2], 1.0, kv[:, sk_]).astype(BF16)


def _in_proj(x, a, sh, wts, tables, tm):
    B, L, D = x.shape
    w_main, wz, wzr, qn_ext, w_kv, kvn = wts
    rope = tables is not None
    tm = min(tm, L)
    bm = (lambda b: b) if a.shape[0] == B else (lambda b: 0)
    const = lambda i, b: (0, 0)
    in_specs = [pl.BlockSpec((1, tm, D), lambda i, b: (b, i, 0)),
                pl.BlockSpec((1, 1, D), lambda i, b: (bm(b), 0, 0)),
                pl.BlockSpec((1, 1, D), lambda i, b: (bm(b), 0, 0)),
                pl.BlockSpec(w_main.shape, const), pl.BlockSpec(wz.shape, const),
                pl.BlockSpec(wzr.shape, const), pl.BlockSpec(qn_ext.shape, const),
                pl.BlockSpec(w_kv.shape, const), pl.BlockSpec(kvn.shape, const)]
    args = [x, a, sh, w_main, wz, wzr, qn_ext, w_kv, kvn]
    if rope:
        in_specs += [pl.BlockSpec((tm, LANES), lambda i, b: (i, 0))] * 6
        args += list(tables)
    widths = (512, 512, 512, 512, 256, 256, 256, 256, 256, 256)
    return pl.pallas_call(
        functools.partial(_in_proj_kernel, rope=rope),
        out_shape=[jax.ShapeDtypeStruct((B, L, w), BF16) for w in widths],
        grid=(L // tm, B),
        in_specs=in_specs,
        out_specs=[pl.BlockSpec((1, tm, w), lambda i, b: (b, i, 0)) for w in widths],
        compiler_params=_cparams(("arbitrary", "arbitrary")),
        name="in_proj_rope" if rope else "in_proj_ctx",
    )(*args)


def _mla_kernel(q_ref, *refs, n_sets):
    if n_sets > 1:
        *kv_refs, o_ref, k_all, v_all = refs

        @pl.when(pl.program_id(1) == 0)
        def _():
            row = 0
            for k_ref, v_ref in zip(kv_refs[0::2], kv_refs[1::2]):
                n = k_ref.shape[1]
                k_all[row:row + n, :] = k_ref[0]
                v_all[row:row + n, :] = v_ref[0]
                row += n
        keys, values = (lambda sl: k_all[:, sl]), (lambda sl: v_all[:, sl])
    else:
        k_ref, v_ref, o_ref = refs
        keys, values = (lambda sl: k_ref[0, :, sl]), (lambda sl: v_ref[0, :, sl])
    lane = _lane_iota((q_ref.shape[1], LANES))
    outs = []
    for h in range(MLA_HEADS):
        sl = slice(h * LANES, (h + 1) * LANES)
        s = _dot_nt(q_ref[0, :, sl], keys(sl))
        p = jnp.exp2(s - jnp.max(s, axis=-1, keepdims=True))
        o = _dot(p.astype(BF16), values(sl))
        one = MLA_ONE_LANE[h % 2]
        outs.append(o * (1.0 / o[:, one:one + 1]))
    for g in range(MLA_HEADS // 2):
        o_ref[0, :, g * LANES:(g + 1) * LANES] = jnp.where(
            lane < 64, outs[2 * g], outs[2 * g + 1]).astype(o_ref.dtype)


def _mla_attention(qm, kvs, tq):
    B, L, _ = qm.shape
    tq = min(tq, L)
    flat = [a for kv in kvs for a in kv]
    lk = sum(k.shape[1] for k, _ in kvs)
    scratch = [pltpu.VMEM((lk, 512), BF16)] * 2 if len(kvs) > 1 else []
    return pl.pallas_call(
        functools.partial(_mla_kernel, n_sets=len(kvs)),
        out_shape=jax.ShapeDtypeStruct((B, L, MLA_HEADS * MLA_V), BF16),
        grid=(B, L // tq),
        in_specs=[pl.BlockSpec((1, tq, 512), lambda b, i: (b, i, 0))]
        + [pl.BlockSpec((1,) + a.shape[1:], lambda b, i: (b, 0, 0)) for a in flat],
        out_specs=pl.BlockSpec((1, tq, MLA_HEADS * MLA_V), lambda b, i: (b, i, 0)),
        scratch_shapes=scratch,
        compiler_params=_cparams(("arbitrary", "arbitrary")),
        name="mla_attention",
    )(qm, *flat)


SWA_TQ = 2 * SWA_BLOCK
SWA_BAND = SWA_TQ + 2 * SWA_BLOCK


def _swa_bias(n_ctx):
    W = SWA_BLOCK
    G = SWA_Q_HEADS // SWA_KV_HEADS
    qq = np.arange(G * SWA_TQ)[:, None] % SWA_TQ
    kk = np.arange(SWA_BAND + n_ctx)[None, :]
    is_ctx = kk >= SWA_BAND
    masks = [is_ctx | (np.abs(kk - shift - qq) <= W) for shift in (0, W, 2 * W)]
    return jnp.asarray(np.where(np.stack(masks), 0.0, NEG_INF), F32)


def _swa_kernel(sink_ref, q_ref, k_ref, v_ref, kc_ref, vc_ref, *rest, banded):
    W = SWA_BLOCK
    i = pl.program_id(1)
    if banded:
        bias_ref, o_ref = rest
        start = jnp.clip(i * SWA_TQ - W, 0, k_ref.shape[1] - SWA_BAND)
        start = pl.multiple_of(start, W)
        k_all = jnp.concatenate([k_ref[0, pl.ds(start, SWA_BAND), :], kc_ref[0]], axis=0)
        v_all = jnp.concatenate([v_ref[0, pl.ds(start, SWA_BAND), :], vc_ref[0]], axis=0)
    else:
        (o_ref,) = rest
        k_all, v_all = kc_ref[0], vc_ref[0]
    tq = q_ref.shape[1]
    G = SWA_Q_HEADS // SWA_KV_HEADS
    row = lax.broadcasted_iota(jnp.int32, (G * tq, 1), 0)
    lo = _lane_iota((tq, LANES)) < 64
    for g in range(SWA_KV_HEADS):
        gs = slice(g * LANES, (g + 1) * LANES)
        k_g, v_g = k_all[:, gs], v_all[:, gs]
        pieces = []
        sink = jnp.zeros((G * tq, 1), F32)
        for jj in range(G // 2):
            j = g * (G // 2) + jj
            q_pair = q_ref[0, :, j * LANES:(j + 1) * LANES]
            zq = jnp.zeros_like(q_pair)
            pieces += [jnp.where(lo, q_pair, zq), jnp.where(lo, zq, q_pair)]
        for hh in range(G):
            sink = jnp.where(row // tq == hh, sink_ref[g * G + hh], sink)
        s = _dot_nt(jnp.concatenate(pieces, axis=0), k_g)
        if banded:
            s = s + bias_ref[0]
        m = jnp.maximum(jnp.max(s, axis=-1, keepdims=True), sink)
        p = jnp.exp2(s - m)
        l = jnp.sum(p, axis=-1, keepdims=True) + jnp.exp2(sink - m)
        o = _dot(p.astype(BF16), v_g) * (1.0 / l)
        for jj in range(G // 2):
            j = g * (G // 2) + jj
            o_ref[0, :, j * LANES:(j + 1) * LANES] = jnp.where(
                lo, o[2 * jj * tq:(2 * jj + 1) * tq], o[(2 * jj + 1) * tq:(2 * jj + 2) * tq]).astype(o_ref.dtype)


def _swa_attention(sink, q, k, v, kc, vc, banded):
    B, L, _ = q.shape
    Lc = kc.shape[1]
    tq = SWA_TQ if banded else L
    nb = L // tq
    Lkv = k.shape[1]
    in_specs = [pl.BlockSpec(memory_space=pltpu.SMEM),
                pl.BlockSpec((1, tq, 512), lambda b, i: (b, i, 0)),
                pl.BlockSpec((1, Lkv, 256), lambda b, i: (b, 0, 0)),
                pl.BlockSpec((1, Lkv, 256), lambda b, i: (b, 0, 0)),
                pl.BlockSpec((1, Lc, 256), lambda b, i: (b, 0, 0)),
                pl.BlockSpec((1, Lc, 256), lambda b, i: (b, 0, 0))]
    args = [sink, q, k, v, kc, vc]
    if banded:
        assert nb >= 2 and L >= SWA_BAND, "band masks assume distinct first and last query tiles"
        bias = _swa_bias(Lc)
        in_specs.append(pl.BlockSpec((1,) + bias.shape[1:],
                                     lambda b, i: (jnp.where(i == 0, 0, jnp.where(i == nb - 1, 2, 1)), 0, 0)))
        args.append(bias)
    return pl.pallas_call(
        functools.partial(_swa_kernel, banded=banded),
        out_shape=jax.ShapeDtypeStruct((B, L, 512), BF16),
        grid=(B, nb),
        in_specs=in_specs,
        out_specs=pl.BlockSpec((1, tq, 512), lambda b, i: (b, i, 0)),
        compiler_params=_cparams(("arbitrary", "arbitrary")),
        name="swa_banded" if banded else "swa_context",
    )(*args)


def _ret_tables(decay_f, decay_b):
    C = RET_CHUNK
    lg_f = jnp.log(jax.nn.sigmoid(decay_f.astype(F32)))
    lg_b = jnp.log(jax.nn.sigmoid(decay_b.astype(F32)))
    idx = jnp.arange(C, dtype=F32)
    diff = idx[:, None] - idx[None, :]
    intra = (jnp.where(diff >= 0, jnp.exp(lg_f[:, None, None] * jnp.maximum(diff, 0.0)), 0.0)
             + jnp.where(diff <= 0, jnp.exp(lg_b[:, None, None] * jnp.maximum(-diff, 0.0)), 0.0))
    lanes = lambda t: jnp.repeat(t.T, RET_DK, axis=1)
    qdf = lanes(jnp.exp(lg_f[:, None] * (idx + 1.0)))
    qdb = lanes(jnp.exp(lg_b[:, None] * (C - idx)))
    kdf = lanes(jnp.exp(lg_f[:, None] * (C - 1.0 - idx)))
    kdb = lanes(jnp.exp(lg_b[:, None] * idx))
    cdf = jnp.repeat(jnp.exp(lg_f * C), RET_DV).reshape(1, -1)
    cdb = jnp.repeat(jnp.exp(lg_b * C), RET_DV).reshape(1, -1)
    return intra, qdf, qdb, kdf, kdb, cdf, cdb


def _ret_kernel(q_ref, k_ref, v_ref, g_ref, qx_ref, kx_ref, vx_ref, gx_ref,
                d_ref, qdf_ref, qdb_ref, kdf_ref, kdb_ref, cdf_ref, cdb_ref,
                *rest, ctx_out):
    if ctx_out:
        o_ref, ox_ref, sf, sb, sfx, sbx = rest
    else:
        o_ref, sf, sb, sfx, sbx = rest
        ox_ref = None
    C = RET_CHUNK
    nc = q_ref.shape[1] // C
    ncx = qx_ref.shape[1] // C
    NG = RET_HEADS // 2
    r = lax.broadcasted_iota(jnp.int32, (LANES, LANES), 0)
    cidx = lax.broadcasted_iota(jnp.int32, (LANES, LANES), 1)
    blockdiag = (r // 64) == (cidx // 64)
    lane = _lane_iota((C, LANES))
    lo = lane < 64

    def state_pass(kr, vr, stf, stb, n):
        def sums(c, _):
            c0 = pl.multiple_of(c * C, C)
            for j in range(NG):
                gs = slice(j * LANES, (j + 1) * LANES)
                kf = kr[0, pl.ds(c0, C), gs].astype(F32)
                kd = jnp.concatenate([(kf * kdf_ref[:, gs]).astype(BF16),
                                      (kf * kdb_ref[:, gs]).astype(BF16)], axis=1)
                kv = _dot_tn(kd, vr[0, pl.ds(c0, C), gs])
                stf[c + 1, j] = jnp.where(blockdiag, kv[:LANES], 0.0)
                stb[c, j] = jnp.where(blockdiag, kv[LANES:], 0.0)
            return 0
        lax.fori_loop(0, n, sums, 0, unroll=min(4, n))

        def scan(t, _):
            cf, cb = t, n - 1 - t
            for j in range(NG):
                gs = slice(j * LANES, (j + 1) * LANES)
                stf[cf + 1, j] = stf[cf, j] * cdf_ref[:, gs] + stf[cf + 1, j]
                stb[cb, j] = stb[cb + 1, j] * cdb_ref[:, gs] + stb[cb, j]
            return 0
        lax.fori_loop(0, n, scan, 0, unroll=min(4, n))

    zero = jnp.zeros((NG, LANES, LANES), F32)
    sfx[0] = zero
    sbx[ncx] = zero
    state_pass(kx_ref, vx_ref, sfx, sbx, ncx)
    sf[0] = sfx[ncx]
    sb[nc] = sbx[0]
    state_pass(k_ref, v_ref, sf, sb, nc)

    def out_pass(qr, kr, vr, gr, orf, stf, stb, n):
        def body(c, _):
            c0 = pl.multiple_of(c * C, C)
            for j in range(NG):
                gs = slice(j * LANES, (j + 1) * LANES)
                qg, kg, vg = qr[0, pl.ds(c0, C), gs], kr[0, pl.ds(c0, C), gs], vr[0, pl.ds(c0, C), gs]
                halves = []
                for half in range(2):
                    zq = jnp.zeros_like(qg)
                    qh = jnp.where(lo, qg, zq) if half == 0 else jnp.where(lo, zq, qg)
                    att = _dot_nt(qh, kg) * d_ref[2 * j + half]
                    halves.append(_dot(att.astype(BF16), vg))
                o = jnp.where(lo, halves[0], halves[1])
                qf = qg.astype(F32)
                qd = jnp.concatenate([(qf * qdf_ref[:, gs]).astype(BF16),
                                      (qf * qdb_ref[:, gs]).astype(BF16)], axis=1)
                s_cat = jnp.concatenate([stf[c, j], stb[c + 1, j]], axis=0).astype(BF16)
                o = o + _dot(qd, s_cat)
                o2 = o * o
                ms = jnp.where(lo, jnp.sum(jnp.where(lo, o2, 0.0), axis=-1, keepdims=True),
                               jnp.sum(jnp.where(lo, 0.0, o2), axis=-1, keepdims=True)) * (1.0 / RET_DV)
                gate = gr[0, pl.ds(c0, C), gs].astype(F32)
                y = o * lax.rsqrt(ms + NORM_EPS) * (gate * jax.nn.sigmoid(gate))
                orf[0, pl.ds(c0, C), gs] = y.astype(orf.dtype)
            return 0
        lax.fori_loop(0, n, body, 0, unroll=min(4, n))

    out_pass(q_ref, k_ref, v_ref, g_ref, o_ref, sf, sb, nc)
    if ctx_out:
        out_pass(qx_ref, kx_ref, vx_ref, gx_ref, ox_ref, sfx, sbx, ncx)


def _retention(q, k, v, g, qx, kx, vx, gx, tabs, ctx_out):
    B, L, W = q.shape
    Lc = qx.shape[1]
    nc, ncx = L // RET_CHUNK, Lc // RET_CHUNK
    lat = pl.BlockSpec((1, L, W), lambda b: (b, 0, 0))
    cx = pl.BlockSpec((1, Lc, W), lambda b: (b, 0, 0))
    full = lambda a: pl.BlockSpec(a.shape, lambda b: (0,) * a.ndim)
    out_shape = [jax.ShapeDtypeStruct((B, L, W), BF16)]
    out_specs = [lat]
    if ctx_out:
        out_shape.append(jax.ShapeDtypeStruct((B, Lc, W), BF16))
        out_specs.append(cx)
    res = pl.pallas_call(
        functools.partial(_ret_kernel, ctx_out=ctx_out),
        out_shape=out_shape,
        grid=(B,),
        in_specs=[lat, lat, lat, lat, cx, cx, cx, cx] + [full(t) for t in tabs],
        out_specs=out_specs,
        scratch_shapes=[pltpu.VMEM((n + 1, RET_HEADS // 2, LANES, LANES), F32) for n in (nc, nc, ncx, ncx)],
        compiler_params=_cparams(("arbitrary",)),
        name="retention",
    )(q, k, v, g, qx, kx, vx, gx, *tabs)
    return (res[0], res[1]) if ctx_out else (res[0], None)


def _swiglu_act(g, u):
    h = 0.5 * g
    return ((h + h * jnp.tanh(h)) * u).astype(BF16)


def _swiglu_chunks(x, wg, wu, wd, bounds):
    acc = None
    for c0, c1 in bounds:
        part = _dot(_swiglu_act(_dot(x, wg[:, c0:c1]), _dot(x, wu[:, c0:c1])), wd[c0:c1, :])
        acc = part if acc is None else acc + part
    return acc


def _chunk_bounds(total, size):
    return tuple((c, min(c + size, total)) for c in range(0, total, size))


def _split_hi_lo(a):
    hi = a.astype(BF16)
    return hi, (a - hi.astype(F32)).astype(BF16)


def _mix_out(oa_ref, ob_ref, oc_ref, x_ref, g1_ref, a2_ref, sh2_ref, wo_ref):
    y = (_dot(oa_ref[0], wo_ref[0:256, :]) + _dot(ob_ref[0], wo_ref[256:768, :])
         + _dot(oc_ref[0], wo_ref[768:1024, :]))
    x1 = x_ref[0] + g1_ref[0] * y
    h2 = x1 * lax.rsqrt(jnp.mean(x1 * x1, axis=-1, keepdims=True) + NORM_EPS) * a2_ref[0] + sh2_ref[0]
    return x1, h2


def _out_ffn_kernel(oa_ref, ob_ref, oc_ref, x_ref, g1_ref, a2_ref, sh2_ref, g2_ref, wo_ref,
                    wg_ref, wu_ref, wd_ref, o_ref, *, bounds):
    x1, h2 = _mix_out(oa_ref, ob_ref, oc_ref, x_ref, g1_ref, a2_ref, sh2_ref, wo_ref)
    o_ref[0] = x1 + g2_ref[0] * _swiglu_chunks(h2.astype(BF16), wg_ref, wu_ref, wd_ref, bounds)


def _out_ffn(oa, ob, oc, x, g1, a2, sh2, g2, wo, wg, wu, wd, tm):
    B, L, D = x.shape
    tm = min(tm, L)
    bm = (lambda b: b) if g1.shape[0] == B else (lambda b: 0)
    tok = lambda w: pl.BlockSpec((1, tm, w), lambda b, i: (b, i, 0))
    vec = pl.BlockSpec((1, 1, D), lambda b, i: (bm(b), 0, 0))
    wspec = lambda w: pl.BlockSpec(w.shape, lambda b, i: (0, 0), pipeline_mode=pl.Buffered(1))
    return pl.pallas_call(
        functools.partial(_out_ffn_kernel, bounds=_chunk_bounds(wg.shape[1], TF_SUB)),
        out_shape=jax.ShapeDtypeStruct((B, L, D), F32),
        grid=(B, L // tm),
        in_specs=[tok(256), tok(512), tok(256), tok(D), vec, vec, vec, vec,
                  wspec(wo), wspec(wg), wspec(wu), wspec(wd)],
        out_specs=tok(D),
        compiler_params=_cparams(("arbitrary", "arbitrary")),
        name="out_proj_dense_ffn",
    )(oa, ob, oc, x, g1, a2, sh2, g2, wo, wg, wu, wd)


def _out_proj_kernel(oa_ref, ob_ref, oc_ref, x_ref, g1_ref, a2_ref, sh2_ref, wo_ref, r_ref, tri_ref,
                     x1_ref, h2_ref, rt_ref, rtt_ref, cnt_ref, base_ref):
    @pl.when((pl.program_id(0) == 0) & (pl.program_id(1) == 0))
    def _():
        base_ref[...] = jnp.zeros_like(base_ref)

    x1, h2 = _mix_out(oa_ref, ob_ref, oc_ref, x_ref, g1_ref, a2_ref, sh2_ref, wo_ref)
    x1_ref[0] = x1
    h2_ref[0] = h2
    tm = h2.shape[0]
    h_hi, h_lo = _split_hi_lo(h2)
    prod = _dot(jnp.concatenate([h_hi, h_lo], axis=0), r_ref[...])
    logits = prod[:tm] + pltpu.roll(prod[:tm], LANES - N_EXPERTS, 1) + prod[tm:]
    lane = _lane_iota(logits.shape)
    logits = jnp.where(lane < N_EXPERTS, logits, -jnp.inf)
    v1 = jnp.max(logits, axis=-1, keepdims=True)
    i1 = jnp.min(jnp.where(logits == v1, lane, LANES), axis=-1, keepdims=True)
    rest_l = jnp.where(lane == i1, -jnp.inf, logits)
    v2 = jnp.max(rest_l, axis=-1, keepdims=True)
    i2 = jnp.min(jnp.where(rest_l == v2, lane, LANES), axis=-1, keepdims=True)
    e2 = jnp.exp(v2 - v1)
    w1 = 1.0 / (1.0 + e2)
    w2 = e2 * w1
    oh1 = jnp.where(lane == i1, 1.0, 0.0)
    oh2 = jnp.where(lane == i2, 1.0, 0.0)
    both = oh1 + oh2
    seen = _dot(tri_ref[...], both.astype(BF16)) + base_ref[...]
    rank1 = jnp.sum(oh1 * seen, axis=-1, keepdims=True)
    rank2 = jnp.sum(oh2 * seen, axis=-1, keepdims=True)
    base_ref[...] += jnp.sum(both, axis=0, keepdims=True)
    cnt_ref[...] = base_ref[...]
    vals = (i1.astype(F32), i2.astype(F32), w1, w2, rank1, rank2)
    row = jnp.zeros_like(logits)
    for k, v in enumerate(vals):
        row = jnp.where(lane == k, v, row)
    rt_ref[0] = row
    rtt_ref[0] = row.T[:ROUTE_FIELDS]


ROUTE_FIELDS = 8


def _out_proj_route(oa, ob, oc, x, g1, a2, sh2, wo, router, tm):
    B, L, D = x.shape
    tm = min(tm, L)
    nt = L // tm
    tok = lambda w: pl.BlockSpec((1, tm, w), lambda b, i: (b, i, 0))
    vec = pl.BlockSpec((1, 1, D), lambda b, i: (b, 0, 0))
    const = lambda a: pl.BlockSpec(a.shape, lambda b, i: (0, 0))
    tri = jnp.asarray(np.tril(np.ones((tm, tm), np.float32), -1), BF16)
    return pl.pallas_call(
        _out_proj_kernel,
        out_shape=[jax.ShapeDtypeStruct((B, L, D), F32), jax.ShapeDtypeStruct((B, L, D), F32),
                   jax.ShapeDtypeStruct((B, L, LANES), F32),
                   jax.ShapeDtypeStruct((B * nt, ROUTE_FIELDS, tm), F32), jax.ShapeDtypeStruct((1, LANES), F32)],
        grid=(B, nt),
        in_specs=[tok(256), tok(512), tok(256), tok(D), vec, vec, vec, const(wo), const(router), const(tri)],
        out_specs=[tok(D), tok(D), tok(LANES),
                   pl.BlockSpec((1, ROUTE_FIELDS, tm), lambda b, i: (b * nt + i, 0, 0)),
                   pl.BlockSpec((1, LANES), lambda b, i: (0, 0))],
        scratch_shapes=[pltpu.VMEM((1, LANES), F32)],
        compiler_params=_cparams(("arbitrary", "arbitrary")),
        name="out_proj_route",
    )(oa, ob, oc, x, g1, a2, sh2, wo, router, tri)


ROW_UNROLL = 16
ZERO_ROWS = 256


def _dispatch_kernel(ends_ref, pad_ref, d0_ref, d1_ref, h_ref, wg_ref, wu_ref, wd_ref,
                     o_ref, wgb_ref, wub_ref, wdb_ref, zeros_ref, sem, zsem, *, tm):
    rows = h_ref.shape[0]
    zr = zeros_ref.shape[0]
    for src, dst in ((wg_ref, wgb_ref), (wu_ref, wub_ref), (wd_ref, wdb_ref)):
        dst[...] = src[...].astype(dst.dtype)

    @pl.when(pl.program_id(0) == 0)
    def _():
        zeros_ref[...] = jnp.zeros_like(zeros_ref)

        def clear(row0, part):
            dst = o_ref.at[pl.ds(pl.multiple_of(row0, tm) + part * zr, zr)]
            return pltpu.make_async_copy(zeros_ref, dst, zsem)
        jobs = [(pad_ref[e] > 0, ends_ref[e] - tm) for e in range(N_EXPERTS)]
        total = ends_ref[N_EXPERTS - 1]
        jobs += [(total + t * tm < o_ref.shape[0], total + t * tm) for t in range(N_EXPERTS)]
        for cond, row0 in jobs:
            @pl.when(cond)
            def _():
                for part in range(tm // zr):
                    clear(row0, part).start()
        for cond, row0 in jobs:
            @pl.when(cond)
            def _():
                for part in range(tm // zr):
                    clear(row0, part).wait()

    def start(r, _):
        pltpu.make_async_copy(h_ref.at[pl.ds(r, 1)], o_ref.at[pl.ds(d0_ref[0, 0, r], 1)], sem).start(priority=0)
        pltpu.make_async_copy(h_ref.at[pl.ds(r, 1)], o_ref.at[pl.ds(d1_ref[0, 0, r], 1)], sem).start(priority=1)
        return 0

    lax.fori_loop(0, rows, start, 0, unroll=ROW_UNROLL)
    for _ in range(2):
        pltpu.make_async_copy(h_ref, o_ref.at[pl.ds(0, rows)], sem).wait()


def _dispatch(h, d0, d1, ends, padding, n_slots, tm, td, wg, wu, wd):
    N, D = h.shape
    steps = N // td
    idx_spec = pl.BlockSpec((1, 1, td), lambda i, *_: (i, 0, 0), memory_space=pltpu.SMEM)
    flat = [w.reshape(-1, w.shape[-1]) for w in (wg, wu, wd)]
    w_specs = [pl.BlockSpec((w.shape[0] // steps, w.shape[1]), lambda i, *_: (i, 0)) for w in flat]
    res = pl.pallas_call(
        functools.partial(_dispatch_kernel, tm=tm),
        out_shape=[jax.ShapeDtypeStruct((n_slots, D), h.dtype)]
        + [jax.ShapeDtypeStruct(w.shape, BF16) for w in flat],
        grid_spec=pltpu.PrefetchScalarGridSpec(
            num_scalar_prefetch=2,
            grid=(steps,),
            in_specs=[idx_spec, idx_spec, pl.BlockSpec((td, D), lambda i, *_: (i, 0))] + w_specs,
            out_specs=[pl.BlockSpec(memory_space=pl.ANY)] + w_specs,
            scratch_shapes=[pltpu.VMEM((ZERO_ROWS, D), h.dtype), pltpu.SemaphoreType.DMA,
                            pltpu.SemaphoreType.DMA]),
        compiler_params=_cparams(("arbitrary",)),
        name="moe_dispatch",
    )(ends, padding, d0.reshape(steps, 1, td), d1.reshape(steps, 1, td), h, *flat)
    return res[0], res[1].reshape(wg.shape), res[2].reshape(wu.shape), res[3].reshape(wd.shape)


def _gmm_kernel(te_ref, nu_ref, x_ref, wg_ref, wu_ref, wd_ref, o_ref, *, bounds):
    @pl.when(pl.program_id(0) < nu_ref[0])
    def _():
        o_ref[...] = _swiglu_chunks(x_ref[...].astype(BF16), wg_ref.at[0], wu_ref.at[0], wd_ref.at[0], bounds)

    @pl.when(pl.program_id(0) >= nu_ref[0])
    def _():
        o_ref[...] = jnp.zeros_like(o_ref)


def _grouped_swiglu(xs, tile_expert, n_used, wg, wu, wd, tm):
    P, D = xs.shape
    E, _, F = wg.shape
    resident = lambda shape: pl.BlockSpec((1,) + shape, lambda i, te, nu: (te[i], 0, 0),
                                          pipeline_mode=pl.Buffered(1))
    return pl.pallas_call(
        functools.partial(_gmm_kernel, bounds=_chunk_bounds(F, TF_SUB)),
        out_shape=jax.ShapeDtypeStruct((P, D), F32),
        grid_spec=pltpu.PrefetchScalarGridSpec(
            num_scalar_prefetch=2,
            grid=(P // tm,),
            in_specs=[pl.BlockSpec((tm, D), lambda i, te, nu: (jnp.minimum(i, nu[0] - 1), 0)),
                      resident((D, F)), resident((D, F)), resident((F, D))],
            out_specs=pl.BlockSpec((tm, D), lambda i, te, nu: (i, 0))),
        compiler_params=_cparams(("arbitrary",)),
        name="moe_grouped_swiglu",
    )(tile_expert, n_used, xs, wg, wu, wd)


def _combine_kernel(d0_ref, d1_ref, y_ref, x_ref, g2_ref, rt_ref, fg_ref, o_ref, buf, sem):
    rows = o_ref.shape[1]

    def start(r, _):
        pltpu.make_async_copy(y_ref.at[pl.ds(d0_ref[0, 0, r], 1)], buf.at[0, pl.ds(r, 1)], sem).start(priority=0)
        pltpu.make_async_copy(y_ref.at[pl.ds(d1_ref[0, 0, r], 1)], buf.at[1, pl.ds(r, 1)], sem).start(priority=1)
        return 0

    lax.fori_loop(0, rows, start, 0, unroll=ROW_UNROLL)
    for k in range(2):
        pltpu.make_async_copy(y_ref.at[pl.ds(0, rows)], buf.at[k], sem).wait()
    rt = rt_ref[0]
    moe = rt[:, 2:3] * buf[0] + rt[:, 3:4] * buf[1]
    x2 = x_ref[0] + g2_ref[0] * moe
    o_ref[0] = x2 * lax.rsqrt(jnp.mean(x2 * x2, axis=-1, keepdims=True) + NORM_EPS) * fg_ref[...]


def _combine(y, d0, d1, x1, g2, rt, final_g, tc):
    B, L, D = x1.shape
    tc = min(tc, L)
    nt = L // tc
    idx_spec = pl.BlockSpec((1, 1, tc), lambda b, i: (b * nt + i, 0, 0), memory_space=pltpu.SMEM)
    tok = lambda w: pl.BlockSpec((1, tc, w), lambda b, i: (b, i, 0))
    return pl.pallas_call(
        _combine_kernel,
        out_shape=jax.ShapeDtypeStruct((B, L, D), F32),
        grid=(B, nt),
        in_specs=[idx_spec, idx_spec, pl.BlockSpec(memory_space=pl.ANY), tok(D),
                  pl.BlockSpec((1, 1, D), lambda b, i: (b, 0, 0)), tok(LANES),
                  pl.BlockSpec((1, D), lambda b, i: (0, 0))],
        out_specs=tok(D),
        scratch_shapes=[pltpu.VMEM((2, tc, D), F32), pltpu.SemaphoreType.DMA],
        compiler_params=_cparams(("arbitrary", "arbitrary")),
        name="moe_combine_norm",
    )(d0.reshape(B * nt, 1, tc), d1.reshape(B * nt, 1, tc), y, x1, g2, rt, final_g.reshape(1, D))


def _routing(rtt, counts, tm):
    field = lambda k: rtt[:, k, :].reshape(-1).astype(jnp.int32)
    n = rtt.shape[0] * rtt.shape[2]
    counts = counts[0, :N_EXPERTS].astype(jnp.int32)
    padded = ((counts + tm - 1) // tm) * tm
    ends = jnp.cumsum(padded)
    offs = ends - padded

    def slot(e, rank):
        start = jnp.zeros_like(e)
        for j in range(N_EXPERTS):
            start = jnp.where(e == j, offs[j], start)
        return start + rank
    dest = (slot(field(0), field(4)), slot(field(1), field(5)))
    n_slots = 2 * n + N_EXPERTS * tm
    tile_start = jnp.arange(n_slots // tm, dtype=jnp.int32) * tm
    n_used = (ends[-1] // tm).astype(jnp.int32)
    te = jnp.sum((tile_start[:, None] >= ends[None, :]).astype(jnp.int32), axis=1)
    last = jnp.sum((ends[-1] - tm >= ends).astype(jnp.int32))
    te = jnp.where(tile_start < ends[-1], te, last).astype(jnp.int32)
    return (n_slots, te, n_used.reshape(1), dest[0], dest[1], ends.astype(jnp.int32),
            (padded - counts).astype(jnp.int32))


TM_IN = 512
TQ_MLA = 256
TM_OUT = 512
TM_FFN = 512
TM_MOE = 512
TF_SUB = 512
TD_DISPATCH = 512
TC_COMBINE = 512


def kernel(x, c, ctx, c_ctx, w_mod, b_mod, norm1_g, norm2_g, w_in, mla_q_norm, mla_w_uq, mla_kv_norm,
           mla_w_ukv, swa_sink, ret_decay_fwd, ret_decay_bwd, w_out, ffn_w_gate, ffn_w_up, ffn_w_down,
           moe_router, moe_w_gate, moe_w_up, moe_w_down, final_norm_g):
    B, L, D = x.shape
    Lc = ctx.shape[1]
    depth = w_mod.shape[0]
    xc = ctx

    cond = jnp.concatenate([c, c_ctx[None], jnp.zeros((16 - B - 1, D), F32)], axis=0)
    mod_all = _modulation(cond, w_mod, b_mod)
    tables = _rope_tables(L)

    for layer in range(depth):
        last = layer == depth - 1
        mod = mod_all[layer].reshape(16, 6, 1, D)
        sh1, sc1, g1, sh2, sc2, g2 = (mod[:B, j] for j in range(6))
        sh1x, sc1x, g1x, sh2x, sc2x, g2x = (mod[B:B + 1, j] for j in range(6))
        n1, n2 = norm1_g[layer], norm2_g[layer]

        wts = _prep_in_weights(w_in[layer], mla_q_norm[layer], mla_w_uq[layer],
                               mla_kv_norm[layer], mla_w_ukv[layer])
        lat = _in_proj(x, n1 * (1.0 + sc1), sh1, wts, tables, TM_IN)
        cx = _in_proj(xc, n1 * (1.0 + sc1x), sh1x, wts, None, TM_IN)
        qm, km, vm, sq, sk, sv, rq, rk, rv, rg = lat
        qmx, kmx, vmx, sqx, skx, svx, rqx, rkx, rvx, rgx = cx

        o_a = _mla_attention(qm, [(km, vm), (kmx, vmx)], TQ_MLA)
        sink = swa_sink[layer].astype(F32) * LOG2E
        o_b = _swa_attention(sink, sq, sk, sv, skx, svx, True)
        rtabs = _ret_tables(ret_decay_fwd[layer], ret_decay_bwd[layer])
        o_c, oc_c = _retention(rq, rk, rv, rg, rqx, rkx, rvx, rgx, rtabs, not last)
        wo = w_out[layer].astype(BF16)

        if layer % 2 == 0:
            i = layer // 2
            wg, wu, wd = (ffn_w_gate[i].astype(BF16), ffn_w_up[i].astype(BF16),
                          ffn_w_down[i].astype(BF16))
            x_next = _out_ffn(o_a, o_b, o_c, x, g1, n2 * (1.0 + sc2), sh2, g2, wo, wg, wu, wd, TM_FFN)
        else:
            i = layer // 2
            r_hi = moe_router[i].astype(BF16)
            r_lo = (moe_router[i] - r_hi.astype(F32)).astype(BF16)
            router = jnp.pad(jnp.concatenate([r_hi, r_lo], axis=1), ((0, 0), (0, LANES - 2 * N_EXPERTS)))
            x1, h2, rt, rtt, counts = _out_proj_route(o_a, o_b, o_c, x, g1, n2 * (1.0 + sc2), sh2, wo, router,
                                                      TM_OUT)
            n_slots, te, n_used, d0, d1, ends, padding = _routing(rtt, counts, TM_MOE)
            xs, ewg, ewu, ewd = _dispatch(h2.reshape(B * L, D), d0, d1, ends, padding, n_slots, TM_MOE,
                                          TD_DISPATCH, moe_w_gate[i], moe_w_up[i], moe_w_down[i])
            y = _grouped_swiglu(xs, te, n_used, ewg, ewu, ewd, TM_MOE)
            if last:
                return _combine(y, d0, d1, x1, g2, rt, final_norm_g, TC_COMBINE)
            raise NotImplementedError("expert layer is only supported as the last layer")

        if not last:
            oc_a = _mla_attention(qmx, [(kmx, vmx)], TQ_MLA)
            oc_b = _swa_attention(sink, sqx, skx, svx, skx, svx, False)
            xc = _out_ffn(oc_a, oc_b, oc_c, xc, g1x, n2 * (1.0 + sc2x), sh2x, g2x, wo, wg, wu, wd, TM_FFN)
        x = x_next
    raise NotImplementedError("trunk must end with the expert layer")
```

```python
import functools
import math

import numpy as np
import jax
import jax.numpy as jnp
from jax import lax
from jax.experimental import pallas as pl
from jax.experimental.pallas import tpu as pltpu

F32 = jnp.float32
BF16 = jnp.bfloat16

D_MODEL = 1024
DEPTH = 2
GRID_W = 64
HEAD_DIM = 64
NORM_EPS = 1e-6
ROPE_BASE = 10000.0
NEG_INF = -1e30

MLA_HEADS = 4
MLA_Q_RANK = 192
MLA_KV_RANK = 128
MLA_NOPE = 64
MLA_ROPE = 32
MLA_V = 64

SWA_Q_HEADS = 8
SWA_KV_HEADS = 2
SWA_BLOCK = 128

RET_HEADS = 4
RET_DK = 64
RET_DV = 64
RET_CHUNK = 128

D_FF = 2816
N_EXPERTS = 8
D_FF_EXPERT = 3584

LOG2E = math.log2(math.e)
MLA_ONE_LANE = (64, 0)
LANES = 128
VMEM_LIMIT = 56 * 1024 * 1024

C_SQ, C_SK, C_SV = 0, 512, 768
C_RQ, C_RK, C_RV, C_RG = 1024, 1280, 1536, 1792
C_CKV, C_EXT = 2048, 2176
IN_COLS = 2432


def _cparams(sem, vmem=VMEM_LIMIT):
    return pltpu.CompilerParams(dimension_semantics=sem, vmem_limit_bytes=vmem)


def _dot(a, b):
    return jnp.dot(a, b, preferred_element_type=F32)


def _dot_nt(a, b):
    return lax.dot_general(a, b, (((1,), (1,)), ((), ())), preferred_element_type=F32)


def _dot_tn(a, b):
    return lax.dot_general(a, b, (((0,), (0,)), ((), ())), preferred_element_type=F32)


def _lane_iota(shape):
    return lax.broadcasted_iota(jnp.int32, shape, len(shape) - 1)


def _mod_kernel(c_ref, w_ref, b_ref, o_ref):
    c = c_ref[...]
    c = c * jax.nn.sigmoid(c)
    o_ref[0] = jnp.dot(c, w_ref[0], preferred_element_type=F32,
                       precision=lax.Precision.HIGHEST) + b_ref[0]


def _modulation(cond, w_mod, b_mod):
    depth, d, n = w_mod.shape
    rows = cond.shape[0]
    tn = 1024
    return pl.pallas_call(
        _mod_kernel,
        out_shape=jax.ShapeDtypeStruct((depth, rows, n), F32),
        grid=(depth, n // tn),
        in_specs=[pl.BlockSpec((rows, d), lambda l, j: (0, 0)),
                  pl.BlockSpec((1, d, tn), lambda l, j: (l, 0, j)),
                  pl.BlockSpec((1, 1, tn), lambda l, j: (l, 0, j))],
        out_specs=pl.BlockSpec((1, rows, tn), lambda l, j: (l, 0, j)),
        compiler_params=_cparams(("arbitrary", "arbitrary")),
        name="modulation",
    )(cond, w_mod, b_mod.reshape(depth, 1, n))


def _angles(pos, dim):
    inv = (ROPE_BASE ** (-np.arange(0, dim, 2, dtype=np.float32) / dim)).astype(np.float32)
    ang = pos.astype(np.float32)[:, None] * inv[None, :]
    return np.concatenate([ang, ang], axis=-1).astype(np.float64)


def _rope_tables(length):
    t = np.arange(length)
    rows, cols = t // GRID_W, t % GRID_W
    ar, ac = _angles(rows, 32), _angles(cols, 32)
    sign32 = np.concatenate([-np.ones(16), np.ones(16)])
    cos_a = np.concatenate([np.cos(ar), np.cos(ac)], axis=-1)
    sin_a = np.concatenate([np.sin(ar) * sign32, np.sin(ac) * sign32], axis=-1)
    cos_a, sin_a = np.tile(cos_a, (1, 2)), np.tile(sin_a, (1, 2))
    at = _angles(t, 64)
    sign64 = np.concatenate([-np.ones(32), np.ones(32)])
    cos_r, sin_r = np.tile(np.cos(at), (1, 2)), np.tile(np.sin(at) * sign64, (1, 2))
    mr, mc = _angles(rows, 16), _angles(cols, 16)
    cos_m = np.ones((length, LANES))
    sin_m = np.zeros((length, LANES))
    cos_m[:, 64:96] = np.concatenate([np.cos(mr), np.cos(mc)], axis=-1)
    sin_m[:, 64:96] = np.concatenate([np.sin(mr), np.sin(mc)], axis=-1)
    return tuple(jnp.asarray(a, F32) for a in (cos_a, sin_a, cos_r, sin_r, cos_m, sin_m))


def _prep_in_weights(w_in, q_norm, w_uq, kv_norm, w_ukv):
    cuts = np.cumsum([MLA_Q_RANK, MLA_KV_RANK, MLA_ROPE, 512, 128, 128, 256, 256, 256, 256])[:-1]
    cq, ckv, kpe, sq, sk, sv, rq, rk, rv, rg = jnp.split(w_in, [int(v) for v in cuts], axis=1)
    d = w_in.shape[0]
    z64 = jnp.zeros((d, 64), F32)
    dup = lambda w: jnp.concatenate([w[:, :64], w[:, :64], w[:, 64:], w[:, 64:]], axis=1)
    low = lambda w: jnp.concatenate([w[:, :64], z64, w[:, 64:], z64], axis=1)
    w_main = jnp.concatenate(
        [sq * (HEAD_DIM ** -0.5 * LOG2E), dup(sk), low(sv), rq, rk * RET_DK ** -0.5, rv, rg, ckv,
         cq, kpe, jnp.zeros((d, 32), F32)], axis=1).astype(BF16)

    scale = (MLA_NOPE + MLA_ROPE) ** -0.5 * LOG2E
    wq = (w_uq * scale).reshape(MLA_Q_RANK, MLA_HEADS, MLA_NOPE + MLA_ROPE)
    wq = jnp.pad(wq, ((0, 64), (0, 0), (0, 32))).reshape(256, 512)
    place = np.zeros((256, 512), np.float32)
    for h in range(MLA_HEADS):
        for dd in range(MLA_ROPE):
            place[MLA_Q_RANK + dd, h * LANES + MLA_NOPE + dd] = 1.0
    wz = jnp.concatenate([wq, jnp.asarray(place)], axis=1)
    perm = np.zeros((1024, 1024), np.float32)
    for g in range(8):
        for dd in range(MLA_ROPE):
            e = dd % 16
            src = dd + 8 if e < 8 else dd - 8
            perm[g * LANES + MLA_NOPE + src, g * LANES + MLA_NOPE + dd] = -1.0 if e < 8 else 1.0
    wz_rot = wz @ jnp.asarray(perm)
    qn_ext = jnp.pad(q_norm, (0, 64)).reshape(1, 256)

    wkv = w_ukv.reshape(MLA_KV_RANK, MLA_HEADS, MLA_NOPE + MLA_V)
    kn = jnp.pad(wkv[:, :, :MLA_NOPE], ((0, 0), (0, 0), (0, 64))).reshape(MLA_KV_RANK, 512)
    vals = wkv[:, :, MLA_NOPE:]
    vv = jnp.stack([jnp.pad(vals[:, h], ((0, 0), (64, 0) if h % 2 else (0, 64))) for h in range(MLA_HEADS)],
                   axis=1).reshape(MLA_KV_RANK, 512)
    w_kv = jnp.concatenate([kn, vv], axis=1)
    return (w_main, wz.astype(BF16), wz_rot.astype(BF16), qn_ext, w_kv.astype(BF16),
            kv_norm.reshape(1, MLA_KV_RANK))


def _rope_roll(x, cos, sin_signed, half):
    lane = _lane_iota(x.shape)
    rot = jnp.where((lane % (2 * half)) < half,
                    pltpu.roll(x, LANES - half, 1), pltpu.roll(x, half, 1))
    return x * cos + rot * sin_signed


def _in_proj_kernel(*refs, rope):
    if rope:
        (x_ref, a_ref, sh_ref, w_ref, wz_ref, wzr_ref, qn_ref, wkv_ref, kvn_ref,
         ca_ref, sa_ref, cr_ref, sr_ref, cm_ref, sm_ref, *outs) = refs
    else:
        (x_ref, a_ref, sh_ref, w_ref, wz_ref, wzr_ref, qn_ref, wkv_ref, kvn_ref, *outs) = refs
    qm_ref, km_ref, vm_ref, sq_ref, sk_ref, sv_ref, rq_ref, rk_ref, rv_ref, rg_ref = outs

    x = x_ref[0]
    h = x * lax.rsqrt(jnp.mean(x * x, axis=-1, keepdims=True) + NORM_EPS) * a_ref[0] + sh_ref[0]
    p = _dot(h.astype(BF16), w_ref[...])

    lane_g = _lane_iota((x.shape[0], LANES))

    def put(ref, col, width, tables=None, half=None, ones_lane=None):
        for g in range(width // LANES):
            blk = p[:, col + g * LANES: col + (g + 1) * LANES]
            if tables is not None:
                blk = _rope_roll(blk, tables[0][...], tables[1][...], half)
            if ones_lane is not None:
                blk = jnp.where(lane_g == ones_lane, 1.0, blk)
            ref[0, :, g * LANES:(g + 1) * LANES] = blk.astype(ref.dtype)

    axial = (ca_ref, sa_ref) if rope else None
    flat = (cr_ref, sr_ref) if rope else None
    put(sq_ref, C_SQ, 512, axial, 16)
    put(sk_ref, C_SK, 256, axial, 16)
    put(sv_ref, C_SV, 256, ones_lane=SWA_ONE_LANE)
    put(rq_ref, C_RQ, 256, flat, 32)
    put(rk_ref, C_RK, 256, flat, 32)
    put(rv_ref, C_RV, 256)
    put(rg_ref, C_RG, 256)

    ext = p[:, C_EXT:C_EXT + 256]
    lane = _lane_iota(ext.shape)
    is_cq = lane < MLA_Q_RANK
    cq_sq = jnp.where(is_cq, ext * ext, 0.0)
    inv = lax.rsqrt(jnp.sum(cq_sq, axis=-1, keepdims=True) * (1.0 / MLA_Q_RANK) + NORM_EPS)
    z = jnp.where(is_cq, ext * inv * qn_ref[...], ext).astype(BF16)
    zw = _dot(z, wz_ref[...])
    ckv = p[:, C_CKV:C_CKV + MLA_KV_RANK]
    ckv = ckv * lax.rsqrt(jnp.mean(ckv * ckv, axis=-1, keepdims=True) + NORM_EPS) * kvn_ref[...]
    kv = _dot(ckv.astype(BF16), wkv_ref[...])
    if rope:
        zr = _dot(z, wzr_ref[...])
    for g in range(MLA_HEADS):
        sl = slice(g * LANES, (g + 1) * LANES)
        sk_ = slice(512 + g * LANES, 512 + (g + 1) * LANES)
        q_g, kpe_g = zw[:, sl], zw[:, sk_]
        if rope:
            q_g = q_g * cm_ref[...] + zr[:, sl] * sm_ref[...]
            kpe_g = kpe_g * cm_ref[...] + zr[:, sk_] * sm_ref[...]
        qm_ref[0, :, sl] = q_g.astype(BF16)
        km_ref[0, :, sl] = (kv[:, sl] + kpe_g).astype(BF16)
        vm_ref[0, :, sl] = jnp.where(lane_g == MLA_ONE_LANE[g % 2], 1.0, kv[:, sk_]).astype(BF16)


def _in_proj(x, a, sh, wts, tables, tm):
    B, L, D = x.shape
    w_main, wz, wzr, qn_ext, w_kv, kvn = wts
    rope = tables is not None
    tm = min(tm, L)
    bm = (lambda b: b) if a.shape[0] == B else (lambda b: 0)
    const = lambda i, b: (0, 0)
    in_specs = [pl.BlockSpec((1, tm, D), lambda i, b: (b, i, 0)),
                pl.BlockSpec((1, 1, D), lambda i, b: (bm(b), 0, 0)),
                pl.BlockSpec((1, 1, D), lambda i, b: (bm(b), 0, 0)),
                pl.BlockSpec(w_main.shape, const), pl.BlockSpec(wz.shape, const),
                pl.BlockSpec(wzr.shape, const), pl.BlockSpec(qn_ext.shape, const),
                pl.BlockSpec(w_kv.shape, const), pl.BlockSpec(kvn.shape, const)]
    args = [x, a, sh, w_main, wz, wzr, qn_ext, w_kv, kvn]
    if rope:
        in_specs += [pl.BlockSpec((tm, LANES), lambda i, b: (i, 0))] * 6
        args += list(tables)
    widths = (512, 512, 512, 512, 256, 256, 256, 256, 256, 256)
    return pl.pallas_call(
        functools.partial(_in_proj_kernel, rope=rope),
        out_shape=[jax.ShapeDtypeStruct((B, L, w), BF16) for w in widths],
        grid=(L // tm, B),
        in_specs=in_specs,
        out_specs=[pl.BlockSpec((1, tm, w), lambda i, b: (b, i, 0)) for w in widths],
        compiler_params=_cparams(("arbitrary", "arbitrary")),
        name="in_proj_rope" if rope else "in_proj_ctx",
    )(*args)


def _mla_kernel(q_ref, *refs, n_sets):
    if n_sets > 1:
        *kv_refs, o_ref, k_all, v_all = refs

        @pl.when(pl.program_id(1) == 0)
        def _():
            row = 0
            for k_ref, v_ref in zip(kv_refs[0::2], kv_refs[1::2]):
                n = k_ref.shape[1]
                k_all[row:row + n, :] = k_ref[0]
                v_all[row:row + n, :] = v_ref[0]
                row += n
        keys, values = (lambda sl: k_all[:, sl]), (lambda sl: v_all[:, sl])
    else:
        k_ref, v_ref, o_ref = refs
        keys, values = (lambda sl: k_ref[0, :, sl]), (lambda sl: v_ref[0, :, sl])
    lane = _lane_iota((q_ref.shape[1], LANES))
    outs = []
    for h in range(MLA_HEADS):
        sl = slice(h * LANES, (h + 1) * LANES)
        s = _dot_nt(q_ref[0, :, sl], keys(sl))
        p = jnp.exp2(s - jnp.max(s, axis=-1, keepdims=True))
        o = _dot(p.astype(BF16), values(sl))
        one = MLA_ONE_LANE[h % 2]
        outs.append(o * (1.0 / o[:, one:one + 1]))
    for g in range(MLA_HEADS // 2):
        o_ref[0, :, g * LANES:(g + 1) * LANES] = jnp.where(
            lane < 64, outs[2 * g], outs[2 * g + 1]).astype(o_ref.dtype)


def _mla_attention(qm, kvs, tq):
    B, L, _ = qm.shape
    tq = min(tq, L)
    flat = [a for kv in kvs for a in kv]
    lk = sum(k.shape[1] for k, _ in kvs)
    scratch = [pltpu.VMEM((lk, 512), BF16)] * 2 if len(kvs) > 1 else []
    return pl.pallas_call(
        functools.partial(_mla_kernel, n_sets=len(kvs)),
        out_shape=jax.ShapeDtypeStruct((B, L, MLA_HEADS * MLA_V), BF16),
        grid=(B, L // tq),
        in_specs=[pl.BlockSpec((1, tq, 512), lambda b, i: (b, i, 0))]
        + [pl.BlockSpec((1,) + a.shape[1:], lambda b, i: (b, 0, 0)) for a in flat],
        out_specs=pl.BlockSpec((1, tq, MLA_HEADS * MLA_V), lambda b, i: (b, i, 0)),
        scratch_shapes=scratch,
        compiler_params=_cparams(("arbitrary", "arbitrary")),
        name="mla_attention",
    )(qm, *flat)


SWA_ONE_LANE = 64
SWA_TQ = 2 * SWA_BLOCK
SWA_BAND = SWA_TQ + 2 * SWA_BLOCK


def _swa_bias():
    W = SWA_BLOCK
    G = SWA_Q_HEADS // SWA_KV_HEADS
    qq = np.arange(G * SWA_TQ)[:, None] % SWA_TQ
    kk = np.arange(SWA_BAND)[None, :]
    masks = [np.abs(kk - shift - qq) <= W for shift in (0, W, 2 * W)]
    return jnp.asarray(np.where(np.stack(masks), 0.0, NEG_INF), F32)


def _swa_kernel(sink_ref, q_ref, k_ref, v_ref, kc_ref, vc_ref, *rest, banded):
    W = SWA_BLOCK
    i = pl.program_id(1)
    if banded:
        bias_ref, o_ref = rest
        start = jnp.clip(i * SWA_TQ - W, 0, k_ref.shape[1] - SWA_BAND)
        start = pl.multiple_of(start, W)
        k_all = jnp.concatenate([k_ref[0, pl.ds(start, SWA_BAND), :], kc_ref[0]], axis=0)
        v_all = jnp.concatenate([v_ref[0, pl.ds(start, SWA_BAND), :], vc_ref[0]], axis=0)
    else:
        (o_ref,) = rest
        k_all, v_all = kc_ref[0], vc_ref[0]
    tq = q_ref.shape[1]
    G = SWA_Q_HEADS // SWA_KV_HEADS
    row = lax.broadcasted_iota(jnp.int32, (G * tq, 1), 0)
    lo = _lane_iota((tq, LANES)) < 64
    for g in range(SWA_KV_HEADS):
        gs = slice(g * LANES, (g + 1) * LANES)
        k_g, v_g = k_all[:, gs], v_all[:, gs]
        pieces = []
        sink = jnp.zeros((G * tq, 1), F32)
        for jj in range(G // 2):
            j = g * (G // 2) + jj
            q_pair = q_ref[0, :, j * LANES:(j + 1) * LANES]
            zq = jnp.zeros_like(q_pair)
            pieces += [jnp.where(lo, q_pair, zq), jnp.where(lo, zq, q_pair)]
        for hh in range(G):
            sink = jnp.where(row // tq == hh, sink_ref[g * G + hh], sink)
        s = _dot_nt(jnp.concatenate(pieces, axis=0), k_g)
        if banded:
            s = jnp.concatenate([s[:, :SWA_BAND] + bias_ref[0], s[:, SWA_BAND:]], axis=1)
        m = jnp.maximum(jnp.max(s, axis=-1, keepdims=True), sink)
        o = _dot(jnp.exp2(s - m).astype(BF16), v_g)
        o = o * (1.0 / (o[:, SWA_ONE_LANE:SWA_ONE_LANE + 1] + jnp.exp2(sink - m)))
        for jj in range(G // 2):
            j = g * (G // 2) + jj
            even, odd = o[2 * jj * tq:(2 * jj + 1) * tq], o[(2 * jj + 1) * tq:(2 * jj + 2) * tq]
            o_ref[0, :, j * LANES:(j + 1) * LANES] = jnp.where(lo, even, pltpu.roll(odd, 64, 1)).astype(o_ref.dtype)


def _swa_attention(sink, q, k, v, kc, vc, banded):
    B, L, _ = q.shape
    Lc = kc.shape[1]
    tq = SWA_TQ if banded else L
    nb = L // tq
    Lkv = k.shape[1]
    in_specs = [pl.BlockSpec(memory_space=pltpu.SMEM),
                pl.BlockSpec((1, tq, 512), lambda b, i: (b, i, 0)),
                pl.BlockSpec((1, Lkv, 256), lambda b, i: (b, 0, 0)),
                pl.BlockSpec((1, Lkv, 256), lambda b, i: (b, 0, 0)),
                pl.BlockSpec((1, Lc, 256), lambda b, i: (b, 0, 0)),
                pl.BlockSpec((1, Lc, 256), lambda b, i: (b, 0, 0))]
    args = [sink, q, k, v, kc, vc]
    if banded:
        assert nb >= 2 and L >= SWA_BAND, "band masks assume distinct first and last query tiles"
        bias = _swa_bias()
        in_specs.append(pl.BlockSpec((1,) + bias.shape[1:],
                                     lambda b, i: (jnp.where(i == 0, 0, jnp.where(i == nb - 1, 2, 1)), 0, 0)))
        args.append(bias)
    return pl.pallas_call(
        functools.partial(_swa_kernel, banded=banded),
        out_shape=jax.ShapeDtypeStruct((B, L, 512), BF16),
        grid=(B, nb),
        in_specs=in_specs,
        out_specs=pl.BlockSpec((1, tq, 512), lambda b, i: (b, i, 0)),
        compiler_params=_cparams(("arbitrary", "arbitrary")),
        name="swa_banded" if banded else "swa_context",
    )(*args)


def _ret_tables(decay_f, decay_b):
    C = RET_CHUNK
    lg_f = jnp.log(jax.nn.sigmoid(decay_f.astype(F32)))
    lg_b = jnp.log(jax.nn.sigmoid(decay_b.astype(F32)))
    idx = jnp.arange(C, dtype=F32)
    diff = idx[:, None] - idx[None, :]
    intra = (jnp.where(diff >= 0, jnp.exp(lg_f[:, None, None] * jnp.maximum(diff, 0.0)), 0.0)
             + jnp.where(diff <= 0, jnp.exp(lg_b[:, None, None] * jnp.maximum(-diff, 0.0)), 0.0))
    lanes = lambda t: jnp.repeat(t.T, RET_DK, axis=1)
    qdf = lanes(jnp.exp(lg_f[:, None] * (idx + 1.0)))
    qdb = lanes(jnp.exp(lg_b[:, None] * (C - idx)))
    kdf = lanes(jnp.exp(lg_f[:, None] * (C - 1.0 - idx)))
    kdb = lanes(jnp.exp(lg_b[:, None] * idx))
    cdf = jnp.repeat(jnp.exp(lg_f * C), RET_DV).reshape(1, -1)
    cdb = jnp.repeat(jnp.exp(lg_b * C), RET_DV).reshape(1, -1)
    return intra, qdf, qdb, kdf, kdb, cdf, cdb


def _ret_kernel(q_ref, k_ref, v_ref, g_ref, qx_ref, kx_ref, vx_ref, gx_ref,
                d_ref, qdf_ref, qdb_ref, kdf_ref, kdb_ref, cdf_ref, cdb_ref,
                *rest, ctx_out):
    if ctx_out:
        o_ref, ox_ref, sf, sb, sfx, sbx = rest
    else:
        o_ref, sf, sb, sfx, sbx = rest
        ox_ref = None
    C = RET_CHUNK
    nc = q_ref.shape[1] // C
    ncx = qx_ref.shape[1] // C
    NG = RET_HEADS // 2
    r = lax.broadcasted_iota(jnp.int32, (LANES, LANES), 0)
    cidx = lax.broadcasted_iota(jnp.int32, (LANES, LANES), 1)
    blockdiag = (r // 64) == (cidx // 64)
    lane = _lane_iota((C, LANES))
    lo = lane < 64

    def state_pass(kr, vr, stf, stb, n):
        def sums(c, _):
            c0 = pl.multiple_of(c * C, C)
            for j in range(NG):
                gs = slice(j * LANES, (j + 1) * LANES)
                kf = kr[0, pl.ds(c0, C), gs].astype(F32)
                kd = jnp.concatenate([(kf * kdf_ref[:, gs]).astype(BF16),
                                      (kf * kdb_ref[:, gs]).astype(BF16)], axis=1)
                kv = _dot_tn(kd, vr[0, pl.ds(c0, C), gs])
                stf[c + 1, j] = jnp.where(blockdiag, kv[:LANES], 0.0)
                stb[c, j] = jnp.where(blockdiag, kv[LANES:], 0.0)
            return 0
        lax.fori_loop(0, n, sums, 0, unroll=min(4, n))

        def scan(t, _):
            cf, cb = t, n - 1 - t
            for j in range(NG):
                gs = slice(j * LANES, (j + 1) * LANES)
                stf[cf + 1, j] = stf[cf, j] * cdf_ref[:, gs] + stf[cf + 1, j]
                stb[cb, j] = stb[cb + 1, j] * cdb_ref[:, gs] + stb[cb, j]
            return 0
        lax.fori_loop(0, n, scan, 0, unroll=min(4, n))

    zero = jnp.zeros((NG, LANES, LANES), F32)
    sfx[0] = zero
    sbx[ncx] = zero
    state_pass(kx_ref, vx_ref, sfx, sbx, ncx)
    sf[0] = sfx[ncx]
    sb[nc] = sbx[0]
    state_pass(k_ref, v_ref, sf, sb, nc)

    def out_pass(qr, kr, vr, gr, orf, stf, stb, n):
        def body(c, _):
            c0 = pl.multiple_of(c * C, C)
            for j in range(NG):
                gs = slice(j * LANES, (j + 1) * LANES)
                qg, kg, vg = qr[0, pl.ds(c0, C), gs], kr[0, pl.ds(c0, C), gs], vr[0, pl.ds(c0, C), gs]
                halves = []
                for half in range(2):
                    zq = jnp.zeros_like(qg)
                    qh = jnp.where(lo, qg, zq) if half == 0 else jnp.where(lo, zq, qg)
                    att = _dot_nt(qh, kg) * d_ref[2 * j + half]
                    halves.append(_dot(att.astype(BF16), vg))
                o = jnp.where(lo, halves[0], halves[1])
                qf = qg.astype(F32)
                qd = jnp.concatenate([(qf * qdf_ref[:, gs]).astype(BF16),
                                      (qf * qdb_ref[:, gs]).astype(BF16)], axis=1)
                s_cat = jnp.concatenate([stf[c, j], stb[c + 1, j]], axis=0).astype(BF16)
                o = o + _dot(qd, s_cat)
                o2 = o * o
                ms = jnp.where(lo, jnp.sum(jnp.where(lo, o2, 0.0), axis=-1, keepdims=True),
                               jnp.sum(jnp.where(lo, 0.0, o2), axis=-1, keepdims=True)) * (1.0 / RET_DV)
                gate = gr[0, pl.ds(c0, C), gs].astype(F32)
                y = o * lax.rsqrt(ms + NORM_EPS) * (gate * jax.nn.sigmoid(gate))
                orf[0, pl.ds(c0, C), gs] = y.astype(orf.dtype)
            return 0
        lax.fori_loop(0, n, body, 0, unroll=min(4, n))

    out_pass(q_ref, k_ref, v_ref, g_ref, o_ref, sf, sb, nc)
    if ctx_out:
        out_pass(qx_ref, kx_ref, vx_ref, gx_ref, ox_ref, sfx, sbx, ncx)


def _retention(q, k, v, g, qx, kx, vx, gx, tabs, ctx_out):
    B, L, W = q.shape
    Lc = qx.shape[1]
    nc, ncx = L // RET_CHUNK, Lc // RET_CHUNK
    lat = pl.BlockSpec((1, L, W), lambda b: (b, 0, 0))
    cx = pl.BlockSpec((1, Lc, W), lambda b: (b, 0, 0))
    full = lambda a: pl.BlockSpec(a.shape, lambda b: (0,) * a.ndim)
    out_shape = [jax.ShapeDtypeStruct((B, L, W), BF16)]
    out_specs = [lat]
    if ctx_out:
        out_shape.append(jax.ShapeDtypeStruct((B, Lc, W), BF16))
        out_specs.append(cx)
    res = pl.pallas_call(
        functools.partial(_ret_kernel, ctx_out=ctx_out),
        out_shape=out_shape,
        grid=(B,),
        in_specs=[lat, lat, lat, lat, cx, cx, cx, cx] + [full(t) for t in tabs],
        out_specs=out_specs,
        scratch_shapes=[pltpu.VMEM((n + 1, RET_HEADS // 2, LANES, LANES), F32) for n in (nc, nc, ncx, ncx)],
        compiler_params=_cparams(("arbitrary",)),
        name="retention",
    )(q, k, v, g, qx, kx, vx, gx, *tabs)
    return (res[0], res[1]) if ctx_out else (res[0], None)


def _swiglu_act(g, u):
    h = 0.5 * g
    return ((h + h * jnp.tanh(h)) * u).astype(BF16)


def _swiglu_chunks(x, wg, wu, wd, bounds):
    acc = None
    for c0, c1 in bounds:
        part = _dot(_swiglu_act(_dot(x, wg[:, c0:c1]), _dot(x, wu[:, c0:c1])), wd[c0:c1, :])
        acc = part if acc is None else acc + part
    return acc


def _chunk_bounds(total, size):
    return tuple((c, min(c + size, total)) for c in range(0, total, size))


def _split_hi_lo(a):
    hi = a.astype(BF16)
    return hi, (a - hi.astype(F32)).astype(BF16)


def _mix_out(oa_ref, ob_ref, oc_ref, x_ref, g1_ref, a2_ref, sh2_ref, wo_ref):
    y = (_dot(oa_ref[0], wo_ref[0:256, :]) + _dot(ob_ref[0], wo_ref[256:768, :])
         + _dot(oc_ref[0], wo_ref[768:1024, :]))
    x1 = x_ref[0] + g1_ref[0] * y
    h2 = x1 * lax.rsqrt(jnp.mean(x1 * x1, axis=-1, keepdims=True) + NORM_EPS) * a2_ref[0] + sh2_ref[0]
    return x1, h2


def _out_ffn_kernel(oa_ref, ob_ref, oc_ref, x_ref, g1_ref, a2_ref, sh2_ref, g2_ref, wo_ref,
                    wg_ref, wu_ref, wd_ref, o_ref, *, bounds):
    x1, h2 = _mix_out(oa_ref, ob_ref, oc_ref, x_ref, g1_ref, a2_ref, sh2_ref, wo_ref)
    o_ref[0] = x1 + g2_ref[0] * _swiglu_chunks(h2.astype(BF16), wg_ref, wu_ref, wd_ref, bounds)


def _out_ffn(oa, ob, oc, x, g1, a2, sh2, g2, wo, wg, wu, wd, tm):
    B, L, D = x.shape
    tm = min(tm, L)
    bm = (lambda b: b) if g1.shape[0] == B else (lambda b: 0)
    tok = lambda w: pl.BlockSpec((1, tm, w), lambda b, i: (b, i, 0))
    vec = pl.BlockSpec((1, 1, D), lambda b, i: (bm(b), 0, 0))
    wspec = lambda w: pl.BlockSpec(w.shape, lambda b, i: (0, 0), pipeline_mode=pl.Buffered(1))
    return pl.pallas_call(
        functools.partial(_out_ffn_kernel, bounds=_chunk_bounds(wg.shape[1], TF_SUB)),
        out_shape=jax.ShapeDtypeStruct((B, L, D), F32),
        grid=(B, L // tm),
        in_specs=[tok(256), tok(512), tok(256), tok(D), vec, vec, vec, vec,
                  wspec(wo), wspec(wg), wspec(wu), wspec(wd)],
        out_specs=tok(D),
        compiler_params=_cparams(("arbitrary", "arbitrary")),
        name="out_proj_dense_ffn",
    )(oa, ob, oc, x, g1, a2, sh2, g2, wo, wg, wu, wd)


def _out_proj_kernel(oa_ref, ob_ref, oc_ref, x_ref, g1_ref, a2_ref, sh2_ref, wo_ref, r_ref, tri_ref,
                     x1_ref, h2_ref, rt_ref, rtt_ref, cnt_ref, base_ref):
    @pl.when((pl.program_id(0) == 0) & (pl.program_id(1) == 0))
    def _():
        base_ref[...] = jnp.zeros_like(base_ref)

    x1, h2 = _mix_out(oa_ref, ob_ref, oc_ref, x_ref, g1_ref, a2_ref, sh2_ref, wo_ref)
    x1_ref[0] = x1
    h2_ref[0] = h2
    tm = h2.shape[0]
    h_hi, h_lo = _split_hi_lo(h2)
    prod = _dot(jnp.concatenate([h_hi, h_lo], axis=0), r_ref[...])
    logits = prod[:tm] + pltpu.roll(prod[:tm], LANES - N_EXPERTS, 1) + prod[tm:]
    lane = _lane_iota(logits.shape)
    logits = jnp.where(lane < N_EXPERTS, logits, -jnp.inf)
    v1 = jnp.max(logits, axis=-1, keepdims=True)
    i1 = jnp.min(jnp.where(logits == v1, lane, LANES), axis=-1, keepdims=True)
    rest_l = jnp.where(lane == i1, -jnp.inf, logits)
    v2 = jnp.max(rest_l, axis=-1, keepdims=True)
    i2 = jnp.min(jnp.where(rest_l == v2, lane, LANES), axis=-1, keepdims=True)
    e2 = jnp.exp(v2 - v1)
    w1 = 1.0 / (1.0 + e2)
    w2 = e2 * w1
    oh1 = jnp.where(lane == i1, 1.0, 0.0)
    oh2 = jnp.where(lane == i2, 1.0, 0.0)
    both = oh1 + oh2
    seen = _dot(tri_ref[...], both.astype(BF16)) + base_ref[...]
    rank1 = jnp.sum(oh1 * seen, axis=-1, keepdims=True)
    rank2 = jnp.sum(oh2 * seen, axis=-1, keepdims=True)
    base_ref[...] += jnp.sum(both, axis=0, keepdims=True)
    cnt_ref[...] = base_ref[...]
    vals = (i1.astype(F32), i2.astype(F32), w1, w2, rank1, rank2)
    row = jnp.zeros_like(logits)
    for k, v in enumerate(vals):
        row = jnp.where(lane == k, v, row)
    rt_ref[0] = row
    rtt_ref[0] = row.T[:ROUTE_FIELDS]


ROUTE_FIELDS = 8


def _out_proj_route(oa, ob, oc, x, g1, a2, sh2, wo, router, tm):
    B, L, D = x.shape
    tm = min(tm, L)
    nt = L // tm
    tok = lambda w: pl.BlockSpec((1, tm, w), lambda b, i: (b, i, 0))
    vec = pl.BlockSpec((1, 1, D), lambda b, i: (b, 0, 0))
    const = lambda a: pl.BlockSpec(a.shape, lambda b, i: (0, 0))
    tri = jnp.asarray(np.tril(np.ones((tm, tm), np.float32), -1), BF16)
    return pl.pallas_call(
        _out_proj_kernel,
        out_shape=[jax.ShapeDtypeStruct((B, L, D), F32), jax.ShapeDtypeStruct((B, L, D), F32),
                   jax.ShapeDtypeStruct((B, L, LANES), F32),
                   jax.ShapeDtypeStruct((B * nt, ROUTE_FIELDS, tm), F32), jax.ShapeDtypeStruct((1, LANES), F32)],
        grid=(B, nt),
        in_specs=[tok(256), tok(512), tok(256), tok(D), vec, vec, vec, const(wo), const(router), const(tri)],
        out_specs=[tok(D), tok(D), tok(LANES),
                   pl.BlockSpec((1, ROUTE_FIELDS, tm), lambda b, i: (b * nt + i, 0, 0)),
                   pl.BlockSpec((1, LANES), lambda b, i: (0, 0))],
        scratch_shapes=[pltpu.VMEM((1, LANES), F32)],
        compiler_params=_cparams(("arbitrary", "arbitrary")),
        name="out_proj_route",
    )(oa, ob, oc, x, g1, a2, sh2, wo, router, tri)


ROW_UNROLL = 16
ZERO_ROWS = 256


def _dispatch_kernel(ends_ref, pad_ref, d0_ref, d1_ref, h_ref, wg_ref, wu_ref, wd_ref,
                     o_ref, wgb_ref, wub_ref, wdb_ref, zeros_ref, sem, zsem, *, tm):
    rows = h_ref.shape[0]
    zr = zeros_ref.shape[0]
    for src, dst in ((wg_ref, wgb_ref), (wu_ref, wub_ref), (wd_ref, wdb_ref)):
        dst[...] = src[...].astype(dst.dtype)

    @pl.when(pl.program_id(0) == 0)
    def _():
        zeros_ref[...] = jnp.zeros_like(zeros_ref)

        def clear(row0, part):
            dst = o_ref.at[pl.ds(pl.multiple_of(row0, tm) + part * zr, zr)]
            return pltpu.make_async_copy(zeros_ref, dst, zsem)
        jobs = [(pad_ref[e] > 0, ends_ref[e] - tm) for e in range(N_EXPERTS)]
        total = ends_ref[N_EXPERTS - 1]
        jobs += [(total + t * tm < o_ref.shape[0], total + t * tm) for t in range(N_EXPERTS)]
        for cond, row0 in jobs:
            @pl.when(cond)
            def _():
                for part in range(tm // zr):
                    clear(row0, part).start()
        for cond, row0 in jobs:
            @pl.when(cond)
            def _():
                for part in range(tm // zr):
                    clear(row0, part).wait()

    def start(r, _):
        pltpu.make_async_copy(h_ref.at[pl.ds(r, 1)], o_ref.at[pl.ds(d0_ref[0, 0, r], 1)], sem).start(priority=0)
        pltpu.make_async_copy(h_ref.at[pl.ds(r, 1)], o_ref.at[pl.ds(d1_ref[0, 0, r], 1)], sem).start(priority=1)
        return 0

    lax.fori_loop(0, rows, start, 0, unroll=ROW_UNROLL)
    for _ in range(2):
        pltpu.make_async_copy(h_ref, o_ref.at[pl.ds(0, rows)], sem).wait()


def _dispatch(h, d0, d1, ends, padding, n_slots, tm, td, wg, wu, wd):
    N, D = h.shape
    steps = N // td
    idx_spec = pl.BlockSpec((1, 1, td), lambda i, *_: (i, 0, 0), memory_space=pltpu.SMEM)
    flat = [w.reshape(-1, w.shape[-1]) for w in (wg, wu, wd)]
    w_specs = [pl.BlockSpec((w.shape[0] // steps, w.shape[1]), lambda i, *_: (i, 0)) for w in flat]
    res = pl.pallas_call(
        functools.partial(_dispatch_kernel, tm=tm),
        out_shape=[jax.ShapeDtypeStruct((n_slots, D), h.dtype)]
        + [jax.ShapeDtypeStruct(w.shape, BF16) for w in flat],
        grid_spec=pltpu.PrefetchScalarGridSpec(
            num_scalar_prefetch=2,
            grid=(steps,),
            in_specs=[idx_spec, idx_spec, pl.BlockSpec((td, D), lambda i, *_: (i, 0))] + w_specs,
            out_specs=[pl.BlockSpec(memory_space=pl.ANY)] + w_specs,
            scratch_shapes=[pltpu.VMEM((ZERO_ROWS, D), h.dtype), pltpu.SemaphoreType.DMA,
                            pltpu.SemaphoreType.DMA]),
        compiler_params=_cparams(("arbitrary",)),
        name="moe_dispatch",
    )(ends, padding, d0.reshape(steps, 1, td), d1.reshape(steps, 1, td), h, *flat)
    return res[0], res[1].reshape(wg.shape), res[2].reshape(wu.shape), res[3].reshape(wd.shape)


def _gmm_kernel(te_ref, nu_ref, x_ref, wg_ref, wu_ref, wd_ref, o_ref, *, bounds):
    @pl.when(pl.program_id(0) < nu_ref[0])
    def _():
        o_ref[...] = _swiglu_chunks(x_ref[...].astype(BF16), wg_ref.at[0], wu_ref.at[0], wd_ref.at[0], bounds)

    @pl.when(pl.program_id(0) >= nu_ref[0])
    def _():
        o_ref[...] = jnp.zeros_like(o_ref)


def _grouped_swiglu(xs, tile_expert, n_used, wg, wu, wd, tm):
    P, D = xs.shape
    E, _, F = wg.shape
    resident = lambda shape: pl.BlockSpec((1,) + shape, lambda i, te, nu: (te[i], 0, 0),
                                          pipeline_mode=pl.Buffered(1))
    return pl.pallas_call(
        functools.partial(_gmm_kernel, bounds=_chunk_bounds(F, TF_SUB)),
        out_shape=jax.ShapeDtypeStruct((P, D), F32),
        grid_spec=pltpu.PrefetchScalarGridSpec(
            num_scalar_prefetch=2,
            grid=(P // tm,),
            in_specs=[pl.BlockSpec((tm, D), lambda i, te, nu: (jnp.minimum(i, nu[0] - 1), 0)),
                      resident((D, F)), resident((D, F)), resident((F, D))],
            out_specs=pl.BlockSpec((tm, D), lambda i, te, nu: (i, 0))),
        compiler_params=_cparams(("arbitrary",)),
        name="moe_grouped_swiglu",
    )(tile_expert, n_used, xs, wg, wu, wd)


def _combine_kernel(d0_ref, d1_ref, y_ref, x_ref, g2_ref, rt_ref, fg_ref, o_ref, buf, sem):
    rows = o_ref.shape[1]

    def start(r, _):
        pltpu.make_async_copy(y_ref.at[pl.ds(d0_ref[0, 0, r], 1)], buf.at[0, pl.ds(r, 1)], sem).start(priority=0)
        pltpu.make_async_copy(y_ref.at[pl.ds(d1_ref[0, 0, r], 1)], buf.at[1, pl.ds(r, 1)], sem).start(priority=1)
        return 0

    lax.fori_loop(0, rows, start, 0, unroll=ROW_UNROLL)
    for k in range(2):
        pltpu.make_async_copy(y_ref.at[pl.ds(0, rows)], buf.at[k], sem).wait()
    rt = rt_ref[0]
    moe = rt[:, 2:3] * buf[0] + rt[:, 3:4] * buf[1]
    x2 = x_ref[0] + g2_ref[0] * moe
    o_ref[0] = x2 * lax.rsqrt(jnp.mean(x2 * x2, axis=-1, keepdims=True) + NORM_EPS) * fg_ref[...]


def _combine(y, d0, d1, x1, g2, rt, final_g, tc):
    B, L, D = x1.shape
    tc = min(tc, L)
    nt = L // tc
    idx_spec = pl.BlockSpec((1, 1, tc), lambda b, i: (b * nt + i, 0, 0), memory_space=pltpu.SMEM)
    tok = lambda w: pl.BlockSpec((1, tc, w), lambda b, i: (b, i, 0))
    return pl.pallas_call(
        _combine_kernel,
        out_shape=jax.ShapeDtypeStruct((B, L, D), F32),
        grid=(B, nt),
        in_specs=[idx_spec, idx_spec, pl.BlockSpec(memory_space=pl.ANY), tok(D),
                  pl.BlockSpec((1, 1, D), lambda b, i: (b, 0, 0)), tok(LANES),
                  pl.BlockSpec((1, D), lambda b, i: (0, 0))],
        out_specs=tok(D),
        scratch_shapes=[pltpu.VMEM((2, tc, D), F32), pltpu.SemaphoreType.DMA],
        compiler_params=_cparams(("arbitrary", "arbitrary")),
        name="moe_combine_norm",
    )(d0.reshape(B * nt, 1, tc), d1.reshape(B * nt, 1, tc), y, x1, g2, rt, final_g.reshape(1, D))


def _routing(rtt, counts, tm):
    field = lambda k: rtt[:, k, :].reshape(-1).astype(jnp.int32)
    n = rtt.shape[0] * rtt.shape[2]
    counts = counts[0, :N_EXPERTS].astype(jnp.int32)
    padded = ((counts + tm - 1) // tm) * tm
    ends = jnp.cumsum(padded)
    offs = ends - padded

    def slot(e, rank):
        start = jnp.zeros_like(e)
        for j in range(N_EXPERTS):
            start = jnp.where(e == j, offs[j], start)
        return start + rank
    dest = (slot(field(0), field(4)), slot(field(1), field(5)))
    n_slots = 2 * n + N_EXPERTS * tm
    tile_start = jnp.arange(n_slots // tm, dtype=jnp.int32) * tm
    n_used = (ends[-1] // tm).astype(jnp.int32)
    te = jnp.sum((tile_start[:, None] >= ends[None, :]).astype(jnp.int32), axis=1)
    last = jnp.sum((ends[-1] - tm >= ends).astype(jnp.int32))
    te = jnp.where(tile_start < ends[-1], te, last).astype(jnp.int32)
    return (n_slots, te, n_used.reshape(1), dest[0], dest[1], ends.astype(jnp.int32),
            (padded - counts).astype(jnp.int32))


TM_IN = 512
TQ_MLA = 256
TM_OUT = 512
TM_FFN = 512
TM_MOE = 512
TF_SUB = 512
TD_DISPATCH = 512
TC_COMBINE = 512


def kernel(x, c, ctx, c_ctx, w_mod, b_mod, norm1_g, norm2_g, w_in, mla_q_norm, mla_w_uq, mla_kv_norm,
           mla_w_ukv, swa_sink, ret_decay_fwd, ret_decay_bwd, w_out, ffn_w_gate, ffn_w_up, ffn_w_down,
           moe_router, moe_w_gate, moe_w_up, moe_w_down, final_norm_g):
    B, L, D = x.shape
    Lc = ctx.shape[1]
    depth = w_mod.shape[0]
    xc = ctx

    cond = jnp.concatenate([c, c_ctx[None], jnp.zeros((16 - B - 1, D), F32)], axis=0)
    mod_all = _modulation(cond, w_mod, b_mod)
    tables = _rope_tables(L)

    for layer in range(depth):
        last = layer == depth - 1
        mod = mod_all[layer].reshape(16, 6, 1, D)
        sh1, sc1, g1, sh2, sc2, g2 = (mod[:B, j] for j in range(6))
        sh1x, sc1x, g1x, sh2x, sc2x, g2x = (mod[B:B + 1, j] for j in range(6))
        n1, n2 = norm1_g[layer], norm2_g[layer]

        wts = _prep_in_weights(w_in[layer], mla_q_norm[layer], mla_w_uq[layer],
                               mla_kv_norm[layer], mla_w_ukv[layer])
        lat = _in_proj(x, n1 * (1.0 + sc1), sh1, wts, tables, TM_IN)
        cx = _in_proj(xc, n1 * (1.0 + sc1x), sh1x, wts, None, TM_IN)
        qm, km, vm, sq, sk, sv, rq, rk, rv, rg = lat
        qmx, kmx, vmx, sqx, skx, svx, rqx, rkx, rvx, rgx = cx

        o_a = _mla_attention(qm, [(km, vm), (kmx, vmx)], TQ_MLA)
        sink = swa_sink[layer].astype(F32) * LOG2E
        o_b = _swa_attention(sink, sq, sk, sv, skx, svx, True)
        rtabs = _ret_tables(ret_decay_fwd[layer], ret_decay_bwd[layer])
        o_c, oc_c = _retention(rq, rk, rv, rg, rqx, rkx, rvx, rgx, rtabs, not last)
        wo = w_out[layer].astype(BF16)

        if layer % 2 == 0:
            i = layer // 2
            wg, wu, wd = (ffn_w_gate[i].astype(BF16), ffn_w_up[i].astype(BF16),
                          ffn_w_down[i].astype(BF16))
            x_next = _out_ffn(o_a, o_b, o_c, x, g1, n2 * (1.0 + sc2), sh2, g2, wo, wg, wu, wd, TM_FFN)
        else:
            i = layer // 2
            r_hi = moe_router[i].astype(BF16)
            r_lo = (moe_router[i] - r_hi.astype(F32)).astype(BF16)
            router = jnp.pad(jnp.concatenate([r_hi, r_lo], axis=1), ((0, 0), (0, LANES - 2 * N_EXPERTS)))
            x1, h2, rt, rtt, counts = _out_proj_route(o_a, o_b, o_c, x, g1, n2 * (1.0 + sc2), sh2, wo, router,
                                                      TM_OUT)
            n_slots, te, n_used, d0, d1, ends, padding = _routing(rtt, counts, TM_MOE)
            xs, ewg, ewu, ewd = _dispatch(h2.reshape(B * L, D), d0, d1, ends, padding, n_slots, TM_MOE,
                                          TD_DISPATCH, moe_w_gate[i], moe_w_up[i], moe_w_down[i])
            y = _grouped_swiglu(xs, te, n_used, ewg, ewu, ewd, TM_MOE)
            if last:
                return _combine(y, d0, d1, x1, g2, rt, final_norm_g, TC_COMBINE)
            raise NotImplementedError("expert layer is only supported as the last layer")

        if not last:
            oc_a = _mla_attention(qmx, [(kmx, vmx)], TQ_MLA)
            oc_b = _swa_attention(sink, sqx, skx, svx, skx, svx, False)
            xc = _out_ffn(oc_a, oc_b, oc_c, xc, g1x, n2 * (1.0 + sc2x), sh2x, g2x, wo, wg, wu, wd, TM_FFN)
        x = x_next
    raise NotImplementedError("trunk must end with the expert layer")
```

```python
import functools
import math

import numpy as np
import jax
import jax.numpy as jnp
from jax import lax
from jax.experimental import pallas as pl
from jax.experimental.pallas import tpu as pltpu

F32 = jnp.float32
BF16 = jnp.bfloat16

D_MODEL = 1024
DEPTH = 2
GRID_W = 64
HEAD_DIM = 64
NORM_EPS = 1e-6
ROPE_BASE = 10000.0
NEG_INF = -1e30

MLA_HEADS = 4
MLA_Q_RANK = 192
MLA_KV_RANK = 128
MLA_NOPE = 64
MLA_ROPE = 32
MLA_V = 64

SWA_Q_HEADS = 8
SWA_KV_HEADS = 2
SWA_BLOCK = 128

RET_HEADS = 4
RET_DK = 64
RET_DV = 64
RET_CHUNK = 128

D_FF = 2816
N_EXPERTS = 8
D_FF_EXPERT = 3584

LOG2E = math.log2(math.e)
MLA_ONE_LANE = (64, 0)
LANES = 128
VMEM_LIMIT = 56 * 1024 * 1024

C_SQ, C_SK, C_SV = 0, 512, 768
C_RQ, C_RK, C_RV, C_RG = 1024, 1280, 1536, 1792
C_CKV, C_EXT = 2048, 2176
IN_COLS = 2432


def _cparams(sem, vmem=VMEM_LIMIT):
    return pltpu.CompilerParams(dimension_semantics=sem, vmem_limit_bytes=vmem)


def _dot(a, b):
    return jnp.dot(a, b, preferred_element_type=F32)


def _dot_nt(a, b):
    return lax.dot_general(a, b, (((1,), (1,)), ((), ())), preferred_element_type=F32)


def _dot_tn(a, b):
    return lax.dot_general(a, b, (((0,), (0,)), ((), ())), preferred_element_type=F32)


def _lane_iota(shape):
    return lax.broadcasted_iota(jnp.int32, shape, len(shape) - 1)


def _mod_kernel(c_ref, w_ref, b_ref, o_ref):
    c = c_ref[...]
    c = c * jax.nn.sigmoid(c)
    o_ref[0] = jnp.dot(c, w_ref[0], preferred_element_type=F32,
                       precision=lax.Precision.HIGHEST) + b_ref[0]


def _modulation(cond, w_mod, b_mod):
    depth, d, n = w_mod.shape
    rows = cond.shape[0]
    tn = 1024
    return pl.pallas_call(
        _mod_kernel,
        out_shape=jax.ShapeDtypeStruct((depth, rows, n), F32),
        grid=(depth, n // tn),
        in_specs=[pl.BlockSpec((rows, d), lambda l, j: (0, 0)),
                  pl.BlockSpec((1, d, tn), lambda l, j: (l, 0, j)),
                  pl.BlockSpec((1, 1, tn), lambda l, j: (l, 0, j))],
        out_specs=pl.BlockSpec((1, rows, tn), lambda l, j: (l, 0, j)),
        compiler_params=_cparams(("arbitrary", "arbitrary")),
        name="modulation",
    )(cond, w_mod, b_mod.reshape(depth, 1, n))


def _angles(pos, dim):
    inv = (ROPE_BASE ** (-np.arange(0, dim, 2, dtype=np.float32) / dim)).astype(np.float32)
    ang = pos.astype(np.float32)[:, None] * inv[None, :]
    return np.concatenate([ang, ang], axis=-1).astype(np.float64)


def _rope_tables(length):
    t = np.arange(length)
    rows, cols = t // GRID_W, t % GRID_W
    ar, ac = _angles(rows, 32), _angles(cols, 32)
    sign32 = np.concatenate([-np.ones(16), np.ones(16)])
    cos_a = np.concatenate([np.cos(ar), np.cos(ac)], axis=-1)
    sin_a = np.concatenate([np.sin(ar) * sign32, np.sin(ac) * sign32], axis=-1)
    cos_a, sin_a = np.tile(cos_a, (1, 2)), np.tile(sin_a, (1, 2))
    at = _angles(t, 64)
    sign64 = np.concatenate([-np.ones(32), np.ones(32)])
    cos_r, sin_r = np.tile(np.cos(at), (1, 2)), np.tile(np.sin(at) * sign64, (1, 2))
    mr, mc = _angles(rows, 16), _angles(cols, 16)
    cos_m = np.ones((length, LANES))
    sin_m = np.zeros((length, LANES))
    cos_m[:, 64:96] = np.concatenate([np.cos(mr), np.cos(mc)], axis=-1)
    sin_m[:, 64:96] = np.concatenate([np.sin(mr), np.sin(mc)], axis=-1)
    return tuple(jnp.asarray(a, F32) for a in (cos_a, sin_a, cos_r, sin_r, cos_m, sin_m))


def _prep_in_weights(w_in, q_norm, w_uq, kv_norm, w_ukv):
    cuts = np.cumsum([MLA_Q_RANK, MLA_KV_RANK, MLA_ROPE, 512, 128, 128, 256, 256, 256, 256])[:-1]
    cq, ckv, kpe, sq, sk, sv, rq, rk, rv, rg = jnp.split(w_in, [int(v) for v in cuts], axis=1)
    d = w_in.shape[0]
    z64 = jnp.zeros((d, 64), F32)
    dup = lambda w: jnp.concatenate([w[:, :64], w[:, :64], w[:, 64:], w[:, 64:]], axis=1)
    low = lambda w: jnp.concatenate([w[:, :64], z64, w[:, 64:], z64], axis=1)
    w_main = jnp.concatenate(
        [sq * (HEAD_DIM ** -0.5 * LOG2E), dup(sk), low(sv), rq, rk * RET_DK ** -0.5, rv, rg, ckv,
         cq, kpe, jnp.zeros((d, 32), F32)], axis=1).astype(BF16)

    scale = (MLA_NOPE + MLA_ROPE) ** -0.5 * LOG2E
    wq = (w_uq * scale).reshape(MLA_Q_RANK, MLA_HEADS, MLA_NOPE + MLA_ROPE)
    wq = jnp.pad(wq, ((0, 64), (0, 0), (0, 32))).reshape(256, 512)
    place = np.zeros((256, 512), np.float32)
    for h in range(MLA_HEADS):
        for dd in range(MLA_ROPE):
            place[MLA_Q_RANK + dd, h * LANES + MLA_NOPE + dd] = 1.0
    wz = jnp.concatenate([wq, jnp.asarray(place)], axis=1)
    src = np.arange(1024)
    sign = np.zeros(1024, np.float32)
    for g in range(8):
        for dd in range(MLA_ROPE):
            col = g * LANES + MLA_NOPE + dd
            src[col] = col + 8 if dd % 16 < 8 else col - 8
            sign[col] = -1.0 if dd % 16 < 8 else 1.0
    wz_rot = wz[:, src] * jnp.asarray(sign)
    qn_ext = jnp.pad(q_norm, (0, 64)).reshape(1, 256)

    wkv = w_ukv.reshape(MLA_KV_RANK, MLA_HEADS, MLA_NOPE + MLA_V)
    kn = jnp.pad(wkv[:, :, :MLA_NOPE], ((0, 0), (0, 0), (0, 64))).reshape(MLA_KV_RANK, 512)
    vals = wkv[:, :, MLA_NOPE:]
    vv = jnp.stack([jnp.pad(vals[:, h], ((0, 0), (64, 0) if h % 2 else (0, 64))) for h in range(MLA_HEADS)],
                   axis=1).reshape(MLA_KV_RANK, 512)
    w_kv = jnp.concatenate([kn, vv], axis=1)
    return (w_main, wz.astype(BF16), wz_rot.astype(BF16), qn_ext, w_kv.astype(BF16),
            kv_norm.reshape(1, MLA_KV_RANK))


def _rope_roll(x, cos, sin_signed, half):
    lane = _lane_iota(x.shape)
    rot = jnp.where((lane % (2 * half)) < half,
                    pltpu.roll(x, LANES - half, 1), pltpu.roll(x, half, 1))
    return x * cos + rot * sin_signed


def _in_proj_kernel(*refs, rope):
    if rope:
        (x_ref, a_ref, sh_ref, w_ref, wz_ref, wzr_ref, qn_ref, wkv_ref, kvn_ref,
         ca_ref, sa_ref, cr_ref, sr_ref, cm_ref, sm_ref, *outs) = refs
    else:
        (x_ref, a_ref, sh_ref, w_ref, wz_ref, wzr_ref, qn_ref, wkv_ref, kvn_ref, *outs) = refs
    qm_ref, km_ref, vm_ref, sq_ref, sk_ref, sv_ref, rq_ref, rk_ref, rv_ref, rg_ref = outs

    x = x_ref[0]
    h = x * lax.rsqrt(jnp.mean(x * x, axis=-1, keepdims=True) + NORM_EPS) * a_ref[0] + sh_ref[0]
    p = _dot(h.astype(BF16), w_ref[...])

    lane_g = _lane_iota((x.shape[0], LANES))

    def put(ref, col, width, tables=None, half=None, ones_lane=None):
        for g in range(width // LANES):
            blk = p[:, col + g * LANES: col + (g + 1) * LANES]
            if tables is not None:
                blk = _rope_roll(blk, tables[0][...], tables[1][...], half)
            if ones_lane is not None:
                blk = jnp.where(lane_g == ones_lane, 1.0, blk)
            ref[0, :, g * LANES:(g + 1) * LANES] = blk.astype(ref.dtype)

    axial = (ca_ref, sa_ref) if rope else None
    flat = (cr_ref, sr_ref) if rope else None
    put(sq_ref, C_SQ, 512, axial, 16)
    put(sk_ref, C_SK, 256, axial, 16)
    put(sv_ref, C_SV, 256, ones_lane=SWA_ONE_LANE)
    put(rq_ref, C_RQ, 256, flat, 32)
    put(rk_ref, C_RK, 256, flat, 32)
    put(rv_ref, C_RV, 256)
    put(rg_ref, C_RG, 256)

    ext = p[:, C_EXT:C_EXT + 256]
    lane = _lane_iota(ext.shape)
    is_cq = lane < MLA_Q_RANK
    cq_sq = jnp.where(is_cq, ext * ext, 0.0)
    inv = lax.rsqrt(jnp.sum(cq_sq, axis=-1, keepdims=True) * (1.0 / MLA_Q_RANK) + NORM_EPS)
    z = jnp.where(is_cq, ext * inv * qn_ref[...], ext).astype(BF16)
    zw = _dot(z, wz_ref[...])
    ckv = p[:, C_CKV:C_CKV + MLA_KV_RANK]
    ckv = ckv * lax.rsqrt(jnp.mean(ckv * ckv, axis=-1, keepdims=True) + NORM_EPS) * kvn_ref[...]
    kv = _dot(ckv.astype(BF16), wkv_ref[...])
    if rope:
        zr = _dot(z, wzr_ref[...])
    for g in range(MLA_HEADS):
        sl = slice(g * LANES, (g + 1) * LANES)
        sk_ = slice(512 + g * LANES, 512 + (g + 1) * LANES)
        q_g, kpe_g = zw[:, sl], zw[:, sk_]
        if rope:
            q_g = q_g * cm_ref[...] + zr[:, sl] * sm_ref[...]
            kpe_g = kpe_g * cm_ref[...] + zr[:, sk_] * sm_ref[...]
        qm_ref[0, :, sl] = q_g.astype(BF16)
        km_ref[0, :, sl] = (kv[:, sl] + kpe_g).astype(BF16)
        vm_ref[0, :, sl] = jnp.where(lane_g == MLA_ONE_LANE[g % 2], 1.0, kv[:, sk_]).astype(BF16)


def _in_proj(x, a, sh, wts, tables, tm):
    B, L, D = x.shape
    w_main, wz, wzr, qn_ext, w_kv, kvn = wts
    rope = tables is not None
    tm = min(tm, L)
    bm = (lambda b: b) if a.shape[0] == B else (lambda b: 0)
    const = lambda i, b: (0, 0)
    in_specs = [pl.BlockSpec((1, tm, D), lambda i, b: (b, i, 0)),
                pl.BlockSpec((1, 1, D), lambda i, b: (bm(b), 0, 0)),
                pl.BlockSpec((1, 1, D), lambda i, b: (bm(b), 0, 0)),
                pl.BlockSpec(w_main.shape, const), pl.BlockSpec(wz.shape, const),
                pl.BlockSpec(wzr.shape, const), pl.BlockSpec(qn_ext.shape, const),
                pl.BlockSpec(w_kv.shape, const), pl.BlockSpec(kvn.shape, const)]
    args = [x, a, sh, w_main, wz, wzr, qn_ext, w_kv, kvn]
    if rope:
        in_specs += [pl.BlockSpec((tm, LANES), lambda i, b: (i, 0))] * 6
        args += list(tables)
    widths = (512, 512, 512, 512, 256, 256, 256, 256, 256, 256)
    return pl.pallas_call(
        functools.partial(_in_proj_kernel, rope=rope),
        out_shape=[jax.ShapeDtypeStruct((B, L, w), BF16) for w in widths],
        grid=(L // tm, B),
        in_specs=in_specs,
        out_specs=[pl.BlockSpec((1, tm, w), lambda i, b: (b, i, 0)) for w in widths],
        compiler_params=_cparams(("arbitrary", "arbitrary")),
        name="in_proj_rope" if rope else "in_proj_ctx",
    )(*args)


def _mla_kernel(q_ref, *refs, n_sets):
    if n_sets > 1:
        *kv_refs, o_ref, k_all, v_all = refs

        @pl.when(pl.program_id(1) == 0)
        def _():
            row = 0
            for k_ref, v_ref in zip(kv_refs[0::2], kv_refs[1::2]):
                n = k_ref.shape[1]
                k_all[row:row + n, :] = k_ref[0]
                v_all[row:row + n, :] = v_ref[0]
                row += n
        keys, values = (lambda sl: k_all[:, sl]), (lambda sl: v_all[:, sl])
    else:
        k_ref, v_ref, o_ref = refs
        keys, values = (lambda sl: k_ref[0, :, sl]), (lambda sl: v_ref[0, :, sl])
    lane = _lane_iota((q_ref.shape[1], LANES))
    outs = []
    for h in range(MLA_HEADS):
        sl = slice(h * LANES, (h + 1) * LANES)
        s = _dot_nt(q_ref[0, :, sl], keys(sl))
        p = jnp.exp2(s - jnp.max(s, axis=-1, keepdims=True))
        o = _dot(p.astype(BF16), values(sl))
        one = MLA_ONE_LANE[h % 2]
        outs.append(o * (1.0 / o[:, one:one + 1]))
    for g in range(MLA_HEADS // 2):
        o_ref[0, :, g * LANES:(g + 1) * LANES] = jnp.where(
            lane < 64, outs[2 * g], outs[2 * g + 1]).astype(o_ref.dtype)


def _mla_attention(qm, kvs, tq):
    B, L, _ = qm.shape
    tq = min(tq, L)
    flat = [a for kv in kvs for a in kv]
    lk = sum(k.shape[1] for k, _ in kvs)
    scratch = [pltpu.VMEM((lk, 512), BF16)] * 2 if len(kvs) > 1 else []
    return pl.pallas_call(
        functools.partial(_mla_kernel, n_sets=len(kvs)),
        out_shape=jax.ShapeDtypeStruct((B, L, MLA_HEADS * MLA_V), BF16),
        grid=(B, L // tq),
        in_specs=[pl.BlockSpec((1, tq, 512), lambda b, i: (b, i, 0))]
        + [pl.BlockSpec((1,) + a.shape[1:], lambda b, i: (b, 0, 0)) for a in flat],
        out_specs=pl.BlockSpec((1, tq, MLA_HEADS * MLA_V), lambda b, i: (b, i, 0)),
        scratch_shapes=scratch,
        compiler_params=_cparams(("arbitrary", "arbitrary")),
        name="mla_attention",
    )(qm, *flat)


SWA_ONE_LANE = 64
SWA_TQ = 2 * SWA_BLOCK
SWA_BAND = SWA_TQ + 2 * SWA_BLOCK


def _swa_bias():
    W = SWA_BLOCK
    qq = np.arange(2 * SWA_TQ)[:, None] % SWA_TQ
    kk = np.arange(SWA_BAND)[None, :]
    masks = [np.abs(kk - shift - qq) <= W for shift in (0, W, 2 * W)]
    return jnp.asarray(np.where(np.stack(masks), 0.0, NEG_INF), F32)


def _swa_kernel(sink_ref, q_ref, k_ref, v_ref, kc_ref, vc_ref, *rest, banded):
    W = SWA_BLOCK
    i = pl.program_id(1)
    if banded:
        bias_ref, o_ref = rest
        start = jnp.clip(i * SWA_TQ - W, 0, k_ref.shape[1] - SWA_BAND)
        start = pl.multiple_of(start, W)
        k_all = jnp.concatenate([k_ref[0, pl.ds(start, SWA_BAND), :], kc_ref[0]], axis=0)
        v_all = jnp.concatenate([v_ref[0, pl.ds(start, SWA_BAND), :], vc_ref[0]], axis=0)
    else:
        (o_ref,) = rest
        k_all, v_all = kc_ref[0], vc_ref[0]
    tq = q_ref.shape[1]
    pairs_per_kv = SWA_Q_HEADS // SWA_KV_HEADS // 2
    upper = lax.broadcasted_iota(jnp.int32, (2 * tq, 1), 0) >= tq
    lo = _lane_iota((tq, LANES)) < 64
    for j in range(SWA_Q_HEADS // 2):
        gs = slice((j // pairs_per_kv) * LANES, (j // pairs_per_kv + 1) * LANES)
        q_pair = q_ref[0, :, j * LANES:(j + 1) * LANES]
        zq = jnp.zeros_like(q_pair)
        q2 = jnp.concatenate([jnp.where(lo, q_pair, zq), jnp.where(lo, zq, q_pair)], axis=0)
        sink = jnp.where(upper, sink_ref[2 * j + 1], sink_ref[2 * j])
        s = _dot_nt(q2, k_all[:, gs])
        if banded:
            s = jnp.concatenate([s[:, :SWA_BAND] + bias_ref[0], s[:, SWA_BAND:]], axis=1)
        m = jnp.maximum(jnp.max(s, axis=-1, keepdims=True), sink)
        o = _dot(jnp.exp2(s - m).astype(BF16), v_all[:, gs])
        o = o * (1.0 / (o[:, SWA_ONE_LANE:SWA_ONE_LANE + 1] + jnp.exp2(sink - m)))
        o_ref[0, :, j * LANES:(j + 1) * LANES] = jnp.where(lo, o[:tq], pltpu.roll(o[tq:], 64, 1)).astype(o_ref.dtype)


def _swa_attention(sink, q, k, v, kc, vc, banded):
    B, L, _ = q.shape
    Lc = kc.shape[1]
    tq = SWA_TQ if banded else L
    nb = L // tq
    Lkv = k.shape[1]
    in_specs = [pl.BlockSpec(memory_space=pltpu.SMEM),
                pl.BlockSpec((1, tq, 512), lambda b, i: (b, i, 0)),
                pl.BlockSpec((1, Lkv, 256), lambda b, i: (b, 0, 0)),
                pl.BlockSpec((1, Lkv, 256), lambda b, i: (b, 0, 0)),
                pl.BlockSpec((1, Lc, 256), lambda b, i: (b, 0, 0)),
                pl.BlockSpec((1, Lc, 256), lambda b, i: (b, 0, 0))]
    args = [sink, q, k, v, kc, vc]
    if banded:
        assert nb >= 2 and L >= SWA_BAND, "band masks assume distinct first and last query tiles"
        bias = _swa_bias()
        in_specs.append(pl.BlockSpec((1,) + bias.shape[1:],
                                     lambda b, i: (jnp.where(i == 0, 0, jnp.where(i == nb - 1, 2, 1)), 0, 0)))
        args.append(bias)
    return pl.pallas_call(
        functools.partial(_swa_kernel, banded=banded),
        out_shape=jax.ShapeDtypeStruct((B, L, 512), BF16),
        grid=(B, nb),
        in_specs=in_specs,
        out_specs=pl.BlockSpec((1, tq, 512), lambda b, i: (b, i, 0)),
        compiler_params=_cparams(("arbitrary", "arbitrary")),
        name="swa_banded" if banded else "swa_context",
    )(*args)


def _ret_tables(decay_f, decay_b):
    C = RET_CHUNK
    lg_f = jnp.log(jax.nn.sigmoid(decay_f.astype(F32)))
    lg_b = jnp.log(jax.nn.sigmoid(decay_b.astype(F32)))
    idx = jnp.arange(C, dtype=F32)
    diff = idx[:, None] - idx[None, :]
    intra = (jnp.where(diff >= 0, jnp.exp(lg_f[:, None, None] * jnp.maximum(diff, 0.0)), 0.0)
             + jnp.where(diff <= 0, jnp.exp(lg_b[:, None, None] * jnp.maximum(-diff, 0.0)), 0.0))
    lanes = lambda t: jnp.repeat(t.T, RET_DK, axis=1)
    qdf = lanes(jnp.exp(lg_f[:, None] * (idx + 1.0)))
    qdb = lanes(jnp.exp(lg_b[:, None] * (C - idx)))
    kdf = lanes(jnp.exp(lg_f[:, None] * (C - 1.0 - idx)))
    kdb = lanes(jnp.exp(lg_b[:, None] * idx))
    cdf = jnp.repeat(jnp.exp(lg_f * C), RET_DV).reshape(1, -1)
    cdb = jnp.repeat(jnp.exp(lg_b * C), RET_DV).reshape(1, -1)
    return intra, qdf, qdb, kdf, kdb, cdf, cdb


def _ret_kernel(q_ref, k_ref, v_ref, g_ref, qx_ref, kx_ref, vx_ref, gx_ref,
                d_ref, qdf_ref, qdb_ref, kdf_ref, kdb_ref, cdf_ref, cdb_ref,
                *rest, ctx_out):
    if ctx_out:
        o_ref, ox_ref, sf, sb, sfx, sbx = rest
    else:
        o_ref, sf, sb, sfx, sbx = rest
        ox_ref = None
    C = RET_CHUNK
    nc = q_ref.shape[1] // C
    ncx = qx_ref.shape[1] // C
    NG = RET_HEADS // 2
    r = lax.broadcasted_iota(jnp.int32, (LANES, LANES), 0)
    cidx = lax.broadcasted_iota(jnp.int32, (LANES, LANES), 1)
    blockdiag = (r // 64) == (cidx // 64)
    lane = _lane_iota((C, LANES))
    lo = lane < 64

    def state_pass(kr, vr, stf, stb, n):
        def sums(c, _):
            c0 = pl.multiple_of(c * C, C)
            for j in range(NG):
                gs = slice(j * LANES, (j + 1) * LANES)
                kf = kr[0, pl.ds(c0, C), gs].astype(F32)
                kd = jnp.concatenate([(kf * kdf_ref[:, gs]).astype(BF16),
                                      (kf * kdb_ref[:, gs]).astype(BF16)], axis=1)
                kv = _dot_tn(kd, vr[0, pl.ds(c0, C), gs])
                stf[c + 1, j] = jnp.where(blockdiag, kv[:LANES], 0.0)
                stb[c, j] = jnp.where(blockdiag, kv[LANES:], 0.0)
            return 0
        lax.fori_loop(0, n, sums, 0, unroll=min(4, n))

        def scan(t, _):
            cf, cb = t, n - 1 - t
            for j in range(NG):
                gs = slice(j * LANES, (j + 1) * LANES)
                stf[cf + 1, j] = stf[cf, j] * cdf_ref[:, gs] + stf[cf + 1, j]
                stb[cb, j] = stb[cb + 1, j] * cdb_ref[:, gs] + stb[cb, j]
            return 0
        lax.fori_loop(0, n, scan, 0, unroll=min(4, n))

    zero = jnp.zeros((NG, LANES, LANES), F32)
    sfx[0] = zero
    sbx[ncx] = zero
    state_pass(kx_ref, vx_ref, sfx, sbx, ncx)
    sf[0] = sfx[ncx]
    sb[nc] = sbx[0]
    state_pass(k_ref, v_ref, sf, sb, nc)

    def out_pass(qr, kr, vr, gr, orf, stf, stb, n):
        def body(c, _):
            c0 = pl.multiple_of(c * C, C)
            for j in range(NG):
                gs = slice(j * LANES, (j + 1) * LANES)
                qg, kg, vg = qr[0, pl.ds(c0, C), gs], kr[0, pl.ds(c0, C), gs], vr[0, pl.ds(c0, C), gs]
                halves = []
                for half in range(2):
                    zq = jnp.zeros_like(qg)
                    qh = jnp.where(lo, qg, zq) if half == 0 else jnp.where(lo, zq, qg)
                    att = _dot_nt(qh, kg) * d_ref[2 * j + half]
                    halves.append(_dot(att.astype(BF16), vg))
                o = jnp.where(lo, halves[0], halves[1])
                qf = qg.astype(F32)
                qd = jnp.concatenate([(qf * qdf_ref[:, gs]).astype(BF16),
                                      (qf * qdb_ref[:, gs]).astype(BF16)], axis=1)
                s_cat = jnp.concatenate([stf[c, j], stb[c + 1, j]], axis=0).astype(BF16)
                o = o + _dot(qd, s_cat)
                o2 = o * o
                ms = jnp.where(lo, jnp.sum(jnp.where(lo, o2, 0.0), axis=-1, keepdims=True),
                               jnp.sum(jnp.where(lo, 0.0, o2), axis=-1, keepdims=True)) * (1.0 / RET_DV)
                gate = gr[0, pl.ds(c0, C), gs].astype(F32)
                y = o * lax.rsqrt(ms + NORM_EPS) * (gate * jax.nn.sigmoid(gate))
                orf[0, pl.ds(c0, C), gs] = y.astype(orf.dtype)
            return 0
        lax.fori_loop(0, n, body, 0, unroll=min(4, n))

    out_pass(q_ref, k_ref, v_ref, g_ref, o_ref, sf, sb, nc)
    if ctx_out:
        out_pass(qx_ref, kx_ref, vx_ref, gx_ref, ox_ref, sfx, sbx, ncx)


def _retention(q, k, v, g, qx, kx, vx, gx, tabs, ctx_out):
    B, L, W = q.shape
    Lc = qx.shape[1]
    nc, ncx = L // RET_CHUNK, Lc // RET_CHUNK
    lat = pl.BlockSpec((1, L, W), lambda b: (b, 0, 0))
    cx = pl.BlockSpec((1, Lc, W), lambda b: (b, 0, 0))
    full = lambda a: pl.BlockSpec(a.shape, lambda b: (0,) * a.ndim)
    out_shape = [jax.ShapeDtypeStruct((B, L, W), BF16)]
    out_specs = [lat]
    if ctx_out:
        out_shape.append(jax.ShapeDtypeStruct((B, Lc, W), BF16))
        out_specs.append(cx)
    res = pl.pallas_call(
        functools.partial(_ret_kernel, ctx_out=ctx_out),
        out_shape=out_shape,
        grid=(B,),
        in_specs=[lat, lat, lat, lat, cx, cx, cx, cx] + [full(t) for t in tabs],
        out_specs=out_specs,
        scratch_shapes=[pltpu.VMEM((n + 1, RET_HEADS // 2, LANES, LANES), F32) for n in (nc, nc, ncx, ncx)],
        compiler_params=_cparams(("arbitrary",)),
        name="retention",
    )(q, k, v, g, qx, kx, vx, gx, *tabs)
    return (res[0], res[1]) if ctx_out else (res[0], None)


def _swiglu_act(g, u):
    h = 0.5 * g
    return ((h + h * jnp.tanh(h)) * u).astype(BF16)


def _swiglu_chunks(x, wg, wu, wd, bounds):
    acc = None
    for c0, c1 in bounds:
        part = _dot(_swiglu_act(_dot(x, wg[:, c0:c1]), _dot(x, wu[:, c0:c1])), wd[c0:c1, :])
        acc = part if acc is None else acc + part
    return acc


def _chunk_bounds(total, size):
    return tuple((c, min(c + size, total)) for c in range(0, total, size))


def _split_hi_lo(a):
    hi = a.astype(BF16)
    return hi, (a - hi.astype(F32)).astype(BF16)


def _mix_out(oa_ref, ob_ref, oc_ref, x_ref, g1_ref, a2_ref, sh2_ref, wo_ref):
    y = (_dot(oa_ref[0], wo_ref[0:256, :]) + _dot(ob_ref[0], wo_ref[256:768, :])
         + _dot(oc_ref[0], wo_ref[768:1024, :]))
    x1 = x_ref[0] + g1_ref[0] * y
    h2 = x1 * lax.rsqrt(jnp.mean(x1 * x1, axis=-1, keepdims=True) + NORM_EPS) * a2_ref[0] + sh2_ref[0]
    return x1, h2


def _out_ffn_kernel(oa_ref, ob_ref, oc_ref, x_ref, g1_ref, a2_ref, sh2_ref, g2_ref, wo_ref,
                    wg_ref, wu_ref, wd_ref, o_ref, *, bounds):
    x1, h2 = _mix_out(oa_ref, ob_ref, oc_ref, x_ref, g1_ref, a2_ref, sh2_ref, wo_ref)
    o_ref[0] = x1 + g2_ref[0] * _swiglu_chunks(h2.astype(BF16), wg_ref, wu_ref, wd_ref, bounds)


def _out_ffn(oa, ob, oc, x, g1, a2, sh2, g2, wo, wg, wu, wd, tm):
    B, L, D = x.shape
    tm = min(tm, L)
    bm = (lambda b: b) if g1.shape[0] == B else (lambda b: 0)
    tok = lambda w: pl.BlockSpec((1, tm, w), lambda b, i: (b, i, 0))
    vec = pl.BlockSpec((1, 1, D), lambda b, i: (bm(b), 0, 0))
    wspec = lambda w: pl.BlockSpec(w.shape, lambda b, i: (0, 0), pipeline_mode=pl.Buffered(1))
    return pl.pallas_call(
        functools.partial(_out_ffn_kernel, bounds=_chunk_bounds(wg.shape[1], TF_SUB)),
        out_shape=jax.ShapeDtypeStruct((B, L, D), F32),
        grid=(B, L // tm),
        in_specs=[tok(256), tok(512), tok(256), tok(D), vec, vec, vec, vec,
                  wspec(wo), wspec(wg), wspec(wu), wspec(wd)],
        out_specs=tok(D),
        compiler_params=_cparams(("arbitrary", "arbitrary")),
        name="out_proj_dense_ffn",
    )(oa, ob, oc, x, g1, a2, sh2, g2, wo, wg, wu, wd)


def _out_proj_kernel(oa_ref, ob_ref, oc_ref, x_ref, g1_ref, a2_ref, sh2_ref, wo_ref, r_ref, tri_ref,
                     x1_ref, h2_ref, rt_ref, rtt_ref, cnt_ref, base_ref):
    @pl.when((pl.program_id(0) == 0) & (pl.program_id(1) == 0))
    def _():
        base_ref[...] = jnp.zeros_like(base_ref)

    x1, h2 = _mix_out(oa_ref, ob_ref, oc_ref, x_ref, g1_ref, a2_ref, sh2_ref, wo_ref)
    x1_ref[0] = x1
    h2_ref[0] = h2
    tm = h2.shape[0]
    h_hi, h_lo = _split_hi_lo(h2)
    prod = _dot(jnp.concatenate([h_hi, h_lo], axis=0), r_ref[...])
    logits = prod[:tm] + pltpu.roll(prod[:tm], LANES - N_EXPERTS, 1) + prod[tm:]
    lane = _lane_iota(logits.shape)
    logits = jnp.where(lane < N_EXPERTS, logits, -jnp.inf)
    v1 = jnp.max(logits, axis=-1, keepdims=True)
    i1 = jnp.min(jnp.where(logits == v1, lane, LANES), axis=-1, keepdims=True)
    rest_l = jnp.where(lane == i1, -jnp.inf, logits)
    v2 = jnp.max(rest_l, axis=-1, keepdims=True)
    i2 = jnp.min(jnp.where(rest_l == v2, lane, LANES), axis=-1, keepdims=True)
    e2 = jnp.exp(v2 - v1)
    w1 = 1.0 / (1.0 + e2)
    w2 = e2 * w1
    oh1 = jnp.where(lane == i1, 1.0, 0.0)
    oh2 = jnp.where(lane == i2, 1.0, 0.0)
    both = oh1 + oh2
    seen = _dot(tri_ref[...], both.astype(BF16)) + base_ref[...]
    rank1 = jnp.sum(oh1 * seen, axis=-1, keepdims=True)
    rank2 = jnp.sum(oh2 * seen, axis=-1, keepdims=True)
    base_ref[...] += jnp.sum(both, axis=0, keepdims=True)
    cnt_ref[...] = base_ref[...]
    vals = (i1.astype(F32), i2.astype(F32), w1, w2, rank1, rank2)
    row = jnp.zeros_like(logits)
    for k, v in enumerate(vals):
        row = jnp.where(lane == k, v, row)
    rt_ref[0] = row
    rtt_ref[0] = row.T[:ROUTE_FIELDS]


ROUTE_FIELDS = 8


def _out_proj_route(oa, ob, oc, x, g1, a2, sh2, wo, router, tm):
    B, L, D = x.shape
    tm = min(tm, L)
    nt = L // tm
    tok = lambda w: pl.BlockSpec((1, tm, w), lambda b, i: (b, i, 0))
    vec = pl.BlockSpec((1, 1, D), lambda b, i: (b, 0, 0))
    const = lambda a: pl.BlockSpec(a.shape, lambda b, i: (0, 0))
    tri = jnp.asarray(np.tril(np.ones((tm, tm), np.float32), -1), BF16)
    return pl.pallas_call(
        _out_proj_kernel,
        out_shape=[jax.ShapeDtypeStruct((B, L, D), F32), jax.ShapeDtypeStruct((B, L, D), F32),
                   jax.ShapeDtypeStruct((B, L, LANES), F32),
                   jax.ShapeDtypeStruct((B * nt, ROUTE_FIELDS, tm), F32), jax.ShapeDtypeStruct((1, LANES), F32)],
        grid=(B, nt),
        in_specs=[tok(256), tok(512), tok(256), tok(D), vec, vec, vec, const(wo), const(router), const(tri)],
        out_specs=[tok(D), tok(D), tok(LANES),
                   pl.BlockSpec((1, ROUTE_FIELDS, tm), lambda b, i: (b * nt + i, 0, 0)),
                   pl.BlockSpec((1, LANES), lambda b, i: (0, 0))],
        scratch_shapes=[pltpu.VMEM((1, LANES), F32)],
        compiler_params=_cparams(("arbitrary", "arbitrary")),
        name="out_proj_route",
    )(oa, ob, oc, x, g1, a2, sh2, wo, router, tri)


ROW_UNROLL = 16
ZERO_ROWS = 256


def _dispatch_kernel(ends_ref, pad_ref, d0_ref, d1_ref, h_ref, wg_ref, wu_ref, wd_ref,
                     o_ref, wgb_ref, wub_ref, wdb_ref, zeros_ref, sem, zsem, *, tm):
    rows = h_ref.shape[0]
    zr = zeros_ref.shape[0]
    for src, dst in ((wg_ref, wgb_ref), (wu_ref, wub_ref), (wd_ref, wdb_ref)):
        dst[...] = src[...].astype(dst.dtype)

    @pl.when(pl.program_id(0) == 0)
    def _():
        zeros_ref[...] = jnp.zeros_like(zeros_ref)

        def clear(row0, part):
            dst = o_ref.at[pl.ds(pl.multiple_of(row0, tm) + part * zr, zr)]
            return pltpu.make_async_copy(zeros_ref, dst, zsem)
        jobs = [(pad_ref[e] > 0, ends_ref[e] - tm) for e in range(N_EXPERTS)]
        total = ends_ref[N_EXPERTS - 1]
        jobs += [(total + t * tm < o_ref.shape[0], total + t * tm) for t in range(N_EXPERTS)]
        for cond, row0 in jobs:
            @pl.when(cond)
            def _():
                for part in range(tm // zr):
                    clear(row0, part).start()
        for cond, row0 in jobs:
            @pl.when(cond)
            def _():
                for part in range(tm // zr):
                    clear(row0, part).wait()

    def start(r, _):
        pltpu.make_async_copy(h_ref.at[pl.ds(r, 1)], o_ref.at[pl.ds(d0_ref[0, 0, r], 1)], sem).start(priority=0)
        pltpu.make_async_copy(h_ref.at[pl.ds(r, 1)], o_ref.at[pl.ds(d1_ref[0, 0, r], 1)], sem).start(priority=1)
        return 0

    lax.fori_loop(0, rows, start, 0, unroll=ROW_UNROLL)
    for _ in range(2):
        pltpu.make_async_copy(h_ref, o_ref.at[pl.ds(0, rows)], sem).wait()


def _dispatch(h, d0, d1, ends, padding, n_slots, tm, td, wg, wu, wd):
    N, D = h.shape
    steps = N // td
    idx_spec = pl.BlockSpec((1, 1, td), lambda i, *_: (i, 0, 0), memory_space=pltpu.SMEM)
    flat = [w.reshape(-1, w.shape[-1]) for w in (wg, wu, wd)]
    w_specs = [pl.BlockSpec((w.shape[0] // steps, w.shape[1]), lambda i, *_: (i, 0)) for w in flat]
    res = pl.pallas_call(
        functools.partial(_dispatch_kernel, tm=tm),
        out_shape=[jax.ShapeDtypeStruct((n_slots, D), h.dtype)]
        + [jax.ShapeDtypeStruct(w.shape, BF16) for w in flat],
        grid_spec=pltpu.PrefetchScalarGridSpec(
            num_scalar_prefetch=2,
            grid=(steps,),
            in_specs=[idx_spec, idx_spec, pl.BlockSpec((td, D), lambda i, *_: (i, 0))] + w_specs,
            out_specs=[pl.BlockSpec(memory_space=pl.ANY)] + w_specs,
            scratch_shapes=[pltpu.VMEM((ZERO_ROWS, D), h.dtype), pltpu.SemaphoreType.DMA,
                            pltpu.SemaphoreType.DMA]),
        compiler_params=_cparams(("arbitrary",)),
        name="moe_dispatch",
    )(ends, padding, d0.reshape(steps, 1, td), d1.reshape(steps, 1, td), h, *flat)
    return res[0], res[1].reshape(wg.shape), res[2].reshape(wu.shape), res[3].reshape(wd.shape)


def _gmm_kernel(te_ref, nu_ref, x_ref, wg_ref, wu_ref, wd_ref, o_ref, *, bounds):
    @pl.when(pl.program_id(0) < nu_ref[0])
    def _():
        o_ref[...] = _swiglu_chunks(x_ref[...].astype(BF16), wg_ref.at[0], wu_ref.at[0], wd_ref.at[0], bounds)

    @pl.when(pl.program_id(0) >= nu_ref[0])
    def _():
        o_ref[...] = jnp.zeros_like(o_ref)


def _grouped_swiglu(xs, tile_expert, n_used, wg, wu, wd, tm):
    P, D = xs.shape
    E, _, F = wg.shape
    resident = lambda shape: pl.BlockSpec((1,) + shape, lambda i, te, nu: (te[i], 0, 0),
                                          pipeline_mode=pl.Buffered(1))
    return pl.pallas_call(
        functools.partial(_gmm_kernel, bounds=_chunk_bounds(F, TF_SUB)),
        out_shape=jax.ShapeDtypeStruct((P, D), F32),
        grid_spec=pltpu.PrefetchScalarGridSpec(
            num_scalar_prefetch=2,
            grid=(P // tm,),
            in_specs=[pl.BlockSpec((tm, D), lambda i, te, nu: (jnp.minimum(i, nu[0] - 1), 0)),
                      resident((D, F)), resident((D, F)), resident((F, D))],
            out_specs=pl.BlockSpec((tm, D), lambda i, te, nu: (i, 0))),
        compiler_params=_cparams(("arbitrary",)),
        name="moe_grouped_swiglu",
    )(tile_expert, n_used, xs, wg, wu, wd)


def _combine_kernel(d0_ref, d1_ref, y_ref, x_ref, g2_ref, rt_ref, fg_ref, o_ref, buf, sem):
    rows = o_ref.shape[1]

    def start(r, _):
        pltpu.make_async_copy(y_ref.at[pl.ds(d0_ref[0, 0, r], 1)], buf.at[0, pl.ds(r, 1)], sem).start(priority=0)
        pltpu.make_async_copy(y_ref.at[pl.ds(d1_ref[0, 0, r], 1)], buf.at[1, pl.ds(r, 1)], sem).start(priority=1)
        return 0

    lax.fori_loop(0, rows, start, 0, unroll=ROW_UNROLL)
    for k in range(2):
        pltpu.make_async_copy(y_ref.at[pl.ds(0, rows)], buf.at[k], sem).wait()
    rt = rt_ref[0]
    moe = rt[:, 2:3] * buf[0] + rt[:, 3:4] * buf[1]
    x2 = x_ref[0] + g2_ref[0] * moe
    o_ref[0] = x2 * lax.rsqrt(jnp.mean(x2 * x2, axis=-1, keepdims=True) + NORM_EPS) * fg_ref[...]


def _combine(y, d0, d1, x1, g2, rt, final_g, tc):
    B, L, D = x1.shape
    tc = min(tc, L)
    nt = L // tc
    idx_spec = pl.BlockSpec((1, 1, tc), lambda b, i: (b * nt + i, 0, 0), memory_space=pltpu.SMEM)
    tok = lambda w: pl.BlockSpec((1, tc, w), lambda b, i: (b, i, 0))
    return pl.pallas_call(
        _combine_kernel,
        out_shape=jax.ShapeDtypeStruct((B, L, D), F32),
        grid=(B, nt),
        in_specs=[idx_spec, idx_spec, pl.BlockSpec(memory_space=pl.ANY), tok(D),
                  pl.BlockSpec((1, 1, D), lambda b, i: (b, 0, 0)), tok(LANES),
                  pl.BlockSpec((1, D), lambda b, i: (0, 0))],
        out_specs=tok(D),
        scratch_shapes=[pltpu.VMEM((2, tc, D), F32), pltpu.SemaphoreType.DMA],
        compiler_params=_cparams(("arbitrary", "arbitrary")),
        name="moe_combine_norm",
    )(d0.reshape(B * nt, 1, tc), d1.reshape(B * nt, 1, tc), y, x1, g2, rt, final_g.reshape(1, D))


def _routing(rtt, counts, tm):
    field = lambda k: rtt[:, k, :].reshape(-1).astype(jnp.int32)
    n = rtt.shape[0] * rtt.shape[2]
    counts = counts[0, :N_EXPERTS].astype(jnp.int32)
    padded = ((counts + tm - 1) // tm) * tm
    ends = jnp.cumsum(padded)
    offs = ends - padded

    def slot(e, rank):
        start = jnp.zeros_like(e)
        for j in range(N_EXPERTS):
            start = jnp.where(e == j, offs[j], start)
        return start + rank
    dest = (slot(field(0), field(4)), slot(field(1), field(5)))
    n_slots = 2 * n + N_EXPERTS * tm
    tile_start = jnp.arange(n_slots // tm, dtype=jnp.int32) * tm
    n_used = (ends[-1] // tm).astype(jnp.int32)
    te = jnp.sum((tile_start[:, None] >= ends[None, :]).astype(jnp.int32), axis=1)
    last = jnp.sum((ends[-1] - tm >= ends).astype(jnp.int32))
    te = jnp.where(tile_start < ends[-1], te, last).astype(jnp.int32)
    return (n_slots, te, n_used.reshape(1), dest[0], dest[1], ends.astype(jnp.int32),
            (padded - counts).astype(jnp.int32))


TM_IN = 512
TQ_MLA = 256
TM_OUT = 512
TM_FFN = 512
TM_MOE = 512
TF_SUB = 512
TD_DISPATCH = 512
TC_COMBINE = 512


def kernel(x, c, ctx, c_ctx, w_mod, b_mod, norm1_g, norm2_g, w_in, mla_q_norm, mla_w_uq, mla_kv_norm,
           mla_w_ukv, swa_sink, ret_decay_fwd, ret_decay_bwd, w_out, ffn_w_gate, ffn_w_up, ffn_w_down,
           moe_router, moe_w_gate, moe_w_up, moe_w_down, final_norm_g):
    B, L, D = x.shape
    Lc = ctx.shape[1]
    depth = w_mod.shape[0]
    xc = ctx

    cond = jnp.concatenate([c, c_ctx[None], jnp.zeros((16 - B - 1, D), F32)], axis=0)
    mod_all = _modulation(cond, w_mod, b_mod)
    tables = _rope_tables(L)

    for layer in range(depth):
        last = layer == depth - 1
        mod = mod_all[layer].reshape(16, 6, 1, D)
        sh1, sc1, g1, sh2, sc2, g2 = (mod[:B, j] for j in range(6))
        sh1x, sc1x, g1x, sh2x, sc2x, g2x = (mod[B:B + 1, j] for j in range(6))
        n1, n2 = norm1_g[layer], norm2_g[layer]

        wts = _prep_in_weights(w_in[layer], mla_q_norm[layer], mla_w_uq[layer],
                               mla_kv_norm[layer], mla_w_ukv[layer])
        lat = _in_proj(x, n1 * (1.0 + sc1), sh1, wts, tables, TM_IN)
        cx = _in_proj(xc, n1 * (1.0 + sc1x), sh1x, wts, None, TM_IN)
        qm, km, vm, sq, sk, sv, rq, rk, rv, rg = lat
        qmx, kmx, vmx, sqx, skx, svx, rqx, rkx, rvx, rgx = cx

        o_a = _mla_attention(qm, [(km, vm), (kmx, vmx)], TQ_MLA)
        sink = swa_sink[layer].astype(F32) * LOG2E
        o_b = _swa_attention(sink, sq, sk, sv, skx, svx, True)
        rtabs = _ret_tables(ret_decay_fwd[layer], ret_decay_bwd[layer])
        o_c, oc_c = _retention(rq, rk, rv, rg, rqx, rkx, rvx, rgx, rtabs, not last)
        wo = w_out[layer].astype(BF16)

        if layer % 2 == 0:
            i = layer // 2
            wg, wu, wd = (ffn_w_gate[i].astype(BF16), ffn_w_up[i].astype(BF16),
                          ffn_w_down[i].astype(BF16))
            x_next = _out_ffn(o_a, o_b, o_c, x, g1, n2 * (1.0 + sc2), sh2, g2, wo, wg, wu, wd, TM_FFN)
        else:
            i = layer // 2
            r_hi = moe_router[i].astype(BF16)
            r_lo = (moe_router[i] - r_hi.astype(F32)).astype(BF16)
            router = jnp.pad(jnp.concatenate([r_hi, r_lo], axis=1), ((0, 0), (0, LANES - 2 * N_EXPERTS)))
            x1, h2, rt, rtt, counts = _out_proj_route(o_a, o_b, o_c, x, g1, n2 * (1.0 + sc2), sh2, wo, router,
                                                      TM_OUT)
            n_slots, te, n_used, d0, d1, ends, padding = _routing(rtt, counts, TM_MOE)
            xs, ewg, ewu, ewd = _dispatch(h2.reshape(B * L, D), d0, d1, ends, padding, n_slots, TM_MOE,
                                          TD_DISPATCH, moe_w_gate[i], moe_w_up[i], moe_w_down[i])
            y = _grouped_swiglu(xs, te, n_used, ewg, ewu, ewd, TM_MOE)
            if last:
                return _combine(y, d0, d1, x1, g2, rt, final_norm_g, TC_COMBINE)
            raise NotImplementedError("expert layer is only supported as the last layer")

        if not last:
            oc_a = _mla_attention(qmx, [(kmx, vmx)], TQ_MLA)
            oc_b = _swa_attention(sink, sqx, skx, svx, skx, svx, False)
            xc = _out_ffn(oc_a, oc_b, oc_c, xc, g1x, n2 * (1.0 + sc2x), sh2x, g2x, wo, wg, wu, wd, TM_FFN)
        x = x_next
    raise NotImplementedError("trunk must end with the expert layer")
```

```python
import functools
import math

import numpy as np
import jax
import jax.numpy as jnp
from jax import lax
from jax.experimental import pallas as pl
from jax.experimental.pallas import tpu as pltpu

F32 = jnp.float32
BF16 = jnp.bfloat16

D_MODEL = 1024
DEPTH = 2
GRID_W = 64
HEAD_DIM = 64
NORM_EPS = 1e-6
ROPE_BASE = 10000.0
NEG_INF = -1e30

MLA_HEADS = 4
MLA_Q_RANK = 192
MLA_KV_RANK = 128
MLA_NOPE = 64
MLA_ROPE = 32
MLA_V = 64

SWA_Q_HEADS = 8
SWA_KV_HEADS = 2
SWA_BLOCK = 128

RET_HEADS = 4
RET_DK = 64
RET_DV = 64
RET_CHUNK = 128

D_FF = 2816
N_EXPERTS = 8
D_FF_EXPERT = 3584

LOG2E = math.log2(math.e)
MLA_ONE_LANE = (64, 0)
LANES = 128
VMEM_LIMIT = 56 * 1024 * 1024

C_SQ, C_SK, C_SV = 0, 512, 768
C_RQ, C_RK, C_RV, C_RG = 1024, 1280, 1536, 1792
C_CKV, C_EXT = 2048, 2176
IN_COLS = 2432


def _cparams(sem, vmem=VMEM_LIMIT):
    return pltpu.CompilerParams(dimension_semantics=sem, vmem_limit_bytes=vmem)


def _dot(a, b):
    return jnp.dot(a, b, preferred_element_type=F32)


def _dot_nt(a, b):
    return lax.dot_general(a, b, (((1,), (1,)), ((), ())), preferred_element_type=F32)


def _dot_tn(a, b):
    return lax.dot_general(a, b, (((0,), (0,)), ((), ())), preferred_element_type=F32)


def _lane_iota(shape):
    return lax.broadcasted_iota(jnp.int32, shape, len(shape) - 1)


def _mod_kernel(c_ref, w_ref, b_ref, o_ref):
    c = c_ref[...]
    c = c * jax.nn.sigmoid(c)
    o_ref[0] = jnp.dot(c, w_ref[0], preferred_element_type=F32,
                       precision=lax.Precision.HIGHEST) + b_ref[0]


def _modulation(cond, w_mod, b_mod):
    depth, d, n = w_mod.shape
    rows = cond.shape[0]
    tn = 1024
    return pl.pallas_call(
        _mod_kernel,
        out_shape=jax.ShapeDtypeStruct((depth, rows, n), F32),
        grid=(depth, n // tn),
        in_specs=[pl.BlockSpec((rows, d), lambda l, j: (0, 0)),
                  pl.BlockSpec((1, d, tn), lambda l, j: (l, 0, j)),
                  pl.BlockSpec((1, 1, tn), lambda l, j: (l, 0, j))],
        out_specs=pl.BlockSpec((1, rows, tn), lambda l, j: (l, 0, j)),
        compiler_params=_cparams(("arbitrary", "arbitrary")),
        name="modulation",
    )(cond, w_mod, b_mod.reshape(depth, 1, n))


def _angles(pos, dim):
    inv = (ROPE_BASE ** (-np.arange(0, dim, 2, dtype=np.float32) / dim)).astype(np.float32)
    ang = pos.astype(np.float32)[:, None] * inv[None, :]
    return np.concatenate([ang, ang], axis=-1).astype(np.float64)


def _rope_tables(length):
    t = np.arange(length)
    rows, cols = t // GRID_W, t % GRID_W
    ar, ac = _angles(rows, 32), _angles(cols, 32)
    sign32 = np.concatenate([-np.ones(16), np.ones(16)])
    cos_a = np.concatenate([np.cos(ar), np.cos(ac)], axis=-1)
    sin_a = np.concatenate([np.sin(ar) * sign32, np.sin(ac) * sign32], axis=-1)
    cos_a, sin_a = np.tile(cos_a, (1, 2)), np.tile(sin_a, (1, 2))
    at = _angles(t, 64)
    sign64 = np.concatenate([-np.ones(32), np.ones(32)])
    cos_r, sin_r = np.tile(np.cos(at), (1, 2)), np.tile(np.sin(at) * sign64, (1, 2))
    mr, mc = _angles(rows, 16), _angles(cols, 16)
    cos_m = np.ones((length, LANES))
    sin_m = np.zeros((length, LANES))
    cos_m[:, 64:96] = np.concatenate([np.cos(mr), np.cos(mc)], axis=-1)
    sin_m[:, 64:96] = np.concatenate([np.sin(mr), np.sin(mc)], axis=-1)
    return tuple(jnp.asarray(a, F32) for a in (cos_a, sin_a, cos_r, sin_r, cos_m, sin_m))


def _prep_in_weights(w_in, q_norm, w_uq, kv_norm, w_ukv):
    cuts = np.cumsum([MLA_Q_RANK, MLA_KV_RANK, MLA_ROPE, 512, 128, 128, 256, 256, 256, 256])[:-1]
    cq, ckv, kpe, sq, sk, sv, rq, rk, rv, rg = jnp.split(w_in, [int(v) for v in cuts], axis=1)
    d = w_in.shape[0]
    z64 = jnp.zeros((d, 64), F32)
    dup = lambda w: jnp.concatenate([w[:, :64], w[:, :64], w[:, 64:], w[:, 64:]], axis=1)
    low = lambda w: jnp.concatenate([w[:, :64], z64, w[:, 64:], z64], axis=1)
    w_main = jnp.concatenate(
        [sq * (HEAD_DIM ** -0.5 * LOG2E), dup(sk), low(sv), rq, rk * RET_DK ** -0.5, rv, rg, ckv,
         cq, kpe, jnp.zeros((d, 32), F32)], axis=1).astype(BF16)

    scale = (MLA_NOPE + MLA_ROPE) ** -0.5 * LOG2E
    wq = (w_uq * scale).reshape(MLA_Q_RANK, MLA_HEADS, MLA_NOPE + MLA_ROPE)
    wq = jnp.pad(wq, ((0, 64), (0, 0), (0, 32))).reshape(256, 512)
    place = np.zeros((256, 512), np.float32)
    for h in range(MLA_HEADS):
        for dd in range(MLA_ROPE):
            place[MLA_Q_RANK + dd, h * LANES + MLA_NOPE + dd] = 1.0
    wz = jnp.concatenate([wq, jnp.asarray(place)], axis=1)
    src = np.arange(1024)
    sign = np.zeros(1024, np.float32)
    for g in range(8):
        for dd in range(MLA_ROPE):
            col = g * LANES + MLA_NOPE + dd
            src[col] = col + 8 if dd % 16 < 8 else col - 8
            sign[col] = -1.0 if dd % 16 < 8 else 1.0
    wz_rot = wz[:, src] * jnp.asarray(sign)
    qn_ext = jnp.pad(q_norm, (0, 64)).reshape(1, 256)

    wkv = w_ukv.reshape(MLA_KV_RANK, MLA_HEADS, MLA_NOPE + MLA_V)
    kn = jnp.pad(wkv[:, :, :MLA_NOPE], ((0, 0), (0, 0), (0, 64))).reshape(MLA_KV_RANK, 512)
    vals = wkv[:, :, MLA_NOPE:]
    vv = jnp.stack([jnp.pad(vals[:, h], ((0, 0), (64, 0) if h % 2 else (0, 64))) for h in range(MLA_HEADS)],
                   axis=1).reshape(MLA_KV_RANK, 512)
    w_kv = jnp.concatenate([kn, vv], axis=1)
    return (w_main, wz.astype(BF16), wz_rot.astype(BF16), qn_ext, w_kv.astype(BF16),
            kv_norm.reshape(1, MLA_KV_RANK))


def _rope_roll(x, cos, sin_signed, half):
    lane = _lane_iota(x.shape)
    rot = jnp.where((lane % (2 * half)) < half,
                    pltpu.roll(x, LANES - half, 1), pltpu.roll(x, half, 1))
    return x * cos + rot * sin_signed


def _in_proj_kernel(*refs, rope):
    if rope:
        (x_ref, a_ref, sh_ref, w_ref, wz_ref, wzr_ref, qn_ref, wkv_ref, kvn_ref,
         ca_ref, sa_ref, cr_ref, sr_ref, cm_ref, sm_ref, *outs) = refs
    else:
        (x_ref, a_ref, sh_ref, w_ref, wz_ref, wzr_ref, qn_ref, wkv_ref, kvn_ref, *outs) = refs
    qm_ref, km_ref, vm_ref, sq_ref, sk_ref, sv_ref, rq_ref, rk_ref, rv_ref, rg_ref = outs

    x = x_ref[0]
    h = x * lax.rsqrt(jnp.mean(x * x, axis=-1, keepdims=True) + NORM_EPS) * a_ref[0] + sh_ref[0]
    p = _dot(h.astype(BF16), w_ref[...])

    lane_g = _lane_iota((x.shape[0], LANES))

    def put(ref, col, width, tables=None, half=None, ones_lane=None):
        for g in range(width // LANES):
            blk = p[:, col + g * LANES: col + (g + 1) * LANES]
            if tables is not None:
                blk = _rope_roll(blk, tables[0][...], tables[1][...], half)
            if ones_lane is not None:
                blk = jnp.where(lane_g == ones_lane, 1.0, blk)
            ref[0, :, g * LANES:(g + 1) * LANES] = blk.astype(ref.dtype)

    axial = (ca_ref, sa_ref) if rope else None
    flat = (cr_ref, sr_ref) if rope else None
    put(sq_ref, C_SQ, 512, axial, 16)
    put(sk_ref, C_SK, 256, axial, 16)
    put(sv_ref, C_SV, 256, ones_lane=SWA_ONE_LANE)
    put(rq_ref, C_RQ, 256, flat, 32)
    put(rk_ref, C_RK, 256, flat, 32)
    put(rv_ref, C_RV, 256)
    put(rg_ref, C_RG, 256)

    ext = p[:, C_EXT:C_EXT + 256]
    lane = _lane_iota(ext.shape)
    is_cq = lane < MLA_Q_RANK
    cq_sq = jnp.where(is_cq, ext * ext, 0.0)
    inv = lax.rsqrt(jnp.sum(cq_sq, axis=-1, keepdims=True) * (1.0 / MLA_Q_RANK) + NORM_EPS)
    z = jnp.where(is_cq, ext * inv * qn_ref[...], ext).astype(BF16)
    zw = _dot(z, wz_ref[...])
    ckv = p[:, C_CKV:C_CKV + MLA_KV_RANK]
    ckv = ckv * lax.rsqrt(jnp.mean(ckv * ckv, axis=-1, keepdims=True) + NORM_EPS) * kvn_ref[...]
    kv = _dot(ckv.astype(BF16), wkv_ref[...])
    if rope:
        zr = _dot(z, wzr_ref[...])
    for g in range(MLA_HEADS):
        sl = slice(g * LANES, (g + 1) * LANES)
        sk_ = slice(512 + g * LANES, 512 + (g + 1) * LANES)
        q_g, kpe_g = zw[:, sl], zw[:, sk_]
        if rope:
            q_g = q_g * cm_ref[...] + zr[:, sl] * sm_ref[...]
            kpe_g = kpe_g * cm_ref[...] + zr[:, sk_] * sm_ref[...]
        qm_ref[0, :, sl] = q_g.astype(BF16)
        km_ref[0, :, sl] = (kv[:, sl] + kpe_g).astype(BF16)
        vm_ref[0, :, sl] = jnp.where(lane_g == MLA_ONE_LANE[g % 2], 1.0, kv[:, sk_]).astype(BF16)


def _in_proj(x, a, sh, wts, tables, tm):
    B, L, D = x.shape
    w_main, wz, wzr, qn_ext, w_kv, kvn = wts
    rope = tables is not None
    tm = min(tm, L)
    bm = (lambda b: b) if a.shape[0] == B else (lambda b: 0)
    const = lambda i, b: (0, 0)
    in_specs = [pl.BlockSpec((1, tm, D), lambda i, b: (b, i, 0)),
                pl.BlockSpec((1, 1, D), lambda i, b: (bm(b), 0, 0)),
                pl.BlockSpec((1, 1, D), lambda i, b: (bm(b), 0, 0)),
                pl.BlockSpec(w_main.shape, const), pl.BlockSpec(wz.shape, const),
                pl.BlockSpec(wzr.shape, const), pl.BlockSpec(qn_ext.shape, const),
                pl.BlockSpec(w_kv.shape, const), pl.BlockSpec(kvn.shape, const)]
    args = [x, a, sh, w_main, wz, wzr, qn_ext, w_kv, kvn]
    if rope:
        in_specs += [pl.BlockSpec((tm, LANES), lambda i, b: (i, 0))] * 6
        args += list(tables)
    widths = (512, 512, 512, 512, 256, 256, 256, 256, 256, 256)
    return pl.pallas_call(
        functools.partial(_in_proj_kernel, rope=rope),
        out_shape=[jax.ShapeDtypeStruct((B, L, w), BF16) for w in widths],
        grid=(L // tm, B),
        in_specs=in_specs,
        out_specs=[pl.BlockSpec((1, tm, w), lambda i, b: (b, i, 0)) for w in widths],
        compiler_params=_cparams(("arbitrary", "arbitrary")),
        name="in_proj_rope" if rope else "in_proj_ctx",
    )(*args)


def _mla_kernel(q_ref, *refs, n_sets):
    if n_sets > 1:
        *kv_refs, o_ref, k_all, v_all = refs

        @pl.when(pl.program_id(1) == 0)
        def _():
            row = 0
            for k_ref, v_ref in zip(kv_refs[0::2], kv_refs[1::2]):
                n = k_ref.shape[1]
                k_all[row:row + n, :] = k_ref[0]
                v_all[row:row + n, :] = v_ref[0]
                row += n
        keys, values = (lambda sl: k_all[:, sl]), (lambda sl: v_all[:, sl])
    else:
        k_ref, v_ref, o_ref = refs
        keys, values = (lambda sl: k_ref[0, :, sl]), (lambda sl: v_ref[0, :, sl])
    lane = _lane_iota((q_ref.shape[1], LANES))
    outs = []
    for h in range(MLA_HEADS):
        sl = slice(h * LANES, (h + 1) * LANES)
        s = _dot_nt(q_ref[0, :, sl], keys(sl))
        p = jnp.exp2(s - jnp.max(s, axis=-1, keepdims=True))
        o = _dot(p.astype(BF16), values(sl))
        one = MLA_ONE_LANE[h % 2]
        outs.append(o * (1.0 / o[:, one:one + 1]))
    for g in range(MLA_HEADS // 2):
        o_ref[0, :, g * LANES:(g + 1) * LANES] = jnp.where(
            lane < 64, outs[2 * g], outs[2 * g + 1]).astype(o_ref.dtype)


def _mla_attention(qm, kvs, tq):
    B, L, _ = qm.shape
    tq = min(tq, L)
    flat = [a for kv in kvs for a in kv]
    lk = sum(k.shape[1] for k, _ in kvs)
    scratch = [pltpu.VMEM((lk, 512), BF16)] * 2 if len(kvs) > 1 else []
    return pl.pallas_call(
        functools.partial(_mla_kernel, n_sets=len(kvs)),
        out_shape=jax.ShapeDtypeStruct((B, L, MLA_HEADS * MLA_V), BF16),
        grid=(B, L // tq),
        in_specs=[pl.BlockSpec((1, tq, 512), lambda b, i: (b, i, 0))]
        + [pl.BlockSpec((1,) + a.shape[1:], lambda b, i: (b, 0, 0)) for a in flat],
        out_specs=pl.BlockSpec((1, tq, MLA_HEADS * MLA_V), lambda b, i: (b, i, 0)),
        scratch_shapes=scratch,
        compiler_params=_cparams(("arbitrary", "arbitrary")),
        name="mla_attention",
    )(qm, *flat)


SWA_ONE_LANE = 64
SWA_TQ = 2 * SWA_BLOCK
SWA_BAND = SWA_TQ + 2 * SWA_BLOCK


def _swa_bias():
    W = SWA_BLOCK
    qq = np.arange(SWA_TQ)[:, None]
    kk = np.arange(SWA_BAND)[None, :]
    masks = [np.abs(kk - shift - qq) <= W for shift in (0, W, 2 * W)]
    return jnp.asarray(np.where(np.stack(masks), 0.0, NEG_INF), F32)


def _swa_kernel(sink_ref, q_ref, k_ref, v_ref, kc_ref, vc_ref, *rest, banded):
    W = SWA_BLOCK
    i = pl.program_id(1)
    if banded:
        bias_ref, o_ref = rest
        start = jnp.clip(i * SWA_TQ - W, 0, k_ref.shape[1] - SWA_BAND)
        start = pl.multiple_of(start, W)
        k_all = jnp.concatenate([k_ref[0, pl.ds(start, SWA_BAND), :], kc_ref[0]], axis=0)
        v_all = jnp.concatenate([v_ref[0, pl.ds(start, SWA_BAND), :], vc_ref[0]], axis=0)
    else:
        (o_ref,) = rest
        k_all, v_all = kc_ref[0], vc_ref[0]
    tq = q_ref.shape[1]
    pairs_per_kv = SWA_Q_HEADS // SWA_KV_HEADS // 2
    lo = _lane_iota((tq, LANES)) < 64
    for j in range(SWA_Q_HEADS // 2):
        gs = slice((j // pairs_per_kv) * LANES, (j // pairs_per_kv + 1) * LANES)
        q_pair = q_ref[0, :, j * LANES:(j + 1) * LANES]
        zq = jnp.zeros_like(q_pair)
        heads = []
        for half, q_h in enumerate((jnp.where(lo, q_pair, zq), jnp.where(lo, zq, q_pair))):
            sink = sink_ref[2 * j + half]
            s = _dot_nt(q_h, k_all[:, gs])
            if banded:
                s = jnp.concatenate([s[:, :SWA_BAND] + bias_ref[0], s[:, SWA_BAND:]], axis=1)
            m = jnp.maximum(jnp.max(s, axis=-1, keepdims=True), sink)
            o = _dot(jnp.exp2(s - m).astype(BF16), v_all[:, gs])
            heads.append(o * (1.0 / (o[:, SWA_ONE_LANE:SWA_ONE_LANE + 1] + jnp.exp2(sink - m))))
        o_ref[0, :, j * LANES:(j + 1) * LANES] = jnp.where(
            lo, heads[0], pltpu.roll(heads[1], 64, 1)).astype(o_ref.dtype)


def _swa_attention(sink, q, k, v, kc, vc, banded):
    B, L, _ = q.shape
    Lc = kc.shape[1]
    tq = SWA_TQ if banded else L
    nb = L // tq
    Lkv = k.shape[1]
    in_specs = [pl.BlockSpec(memory_space=pltpu.SMEM),
                pl.BlockSpec((1, tq, 512), lambda b, i: (b, i, 0)),
                pl.BlockSpec((1, Lkv, 256), lambda b, i: (b, 0, 0)),
                pl.BlockSpec((1, Lkv, 256), lambda b, i: (b, 0, 0)),
                pl.BlockSpec((1, Lc, 256), lambda b, i: (b, 0, 0)),
                pl.BlockSpec((1, Lc, 256), lambda b, i: (b, 0, 0))]
    args = [sink, q, k, v, kc, vc]
    if banded:
        assert nb >= 2 and L >= SWA_BAND, "band masks assume distinct first and last query tiles"
        bias = _swa_bias()
        in_specs.append(pl.BlockSpec((1,) + bias.shape[1:],
                                     lambda b, i: (jnp.where(i == 0, 0, jnp.where(i == nb - 1, 2, 1)), 0, 0)))
        args.append(bias)
    return pl.pallas_call(
        functools.partial(_swa_kernel, banded=banded),
        out_shape=jax.ShapeDtypeStruct((B, L, 512), BF16),
        grid=(B, nb),
        in_specs=in_specs,
        out_specs=pl.BlockSpec((1, tq, 512), lambda b, i: (b, i, 0)),
        compiler_params=_cparams(("arbitrary", "arbitrary")),
        name="swa_banded" if banded else "swa_context",
    )(*args)


def _ret_tables(decay_f, decay_b):
    C = RET_CHUNK
    lg_f = jnp.log(jax.nn.sigmoid(decay_f.astype(F32)))
    lg_b = jnp.log(jax.nn.sigmoid(decay_b.astype(F32)))
    idx = jnp.arange(C, dtype=F32)
    diff = idx[:, None] - idx[None, :]
    intra = (jnp.where(diff >= 0, jnp.exp(lg_f[:, None, None] * jnp.maximum(diff, 0.0)), 0.0)
             + jnp.where(diff <= 0, jnp.exp(lg_b[:, None, None] * jnp.maximum(-diff, 0.0)), 0.0))
    lanes = lambda t: jnp.repeat(t.T, RET_DK, axis=1)
    qdf = lanes(jnp.exp(lg_f[:, None] * (idx + 1.0)))
    qdb = lanes(jnp.exp(lg_b[:, None] * (C - idx)))
    kdf = lanes(jnp.exp(lg_f[:, None] * (C - 1.0 - idx)))
    kdb = lanes(jnp.exp(lg_b[:, None] * idx))
    cdf = jnp.repeat(jnp.exp(lg_f * C), RET_DV).reshape(1, -1)
    cdb = jnp.repeat(jnp.exp(lg_b * C), RET_DV).reshape(1, -1)
    return intra, qdf, qdb, kdf, kdb, cdf, cdb


def _ret_kernel(q_ref, k_ref, v_ref, g_ref, qx_ref, kx_ref, vx_ref, gx_ref,
                d_ref, qdf_ref, qdb_ref, kdf_ref, kdb_ref, cdf_ref, cdb_ref,
                *rest, ctx_out):
    if ctx_out:
        o_ref, ox_ref, sf, sb, sfx, sbx = rest
    else:
        o_ref, sf, sb, sfx, sbx = rest
        ox_ref = None
    C = RET_CHUNK
    nc = q_ref.shape[1] // C
    ncx = qx_ref.shape[1] // C
    NG = RET_HEADS // 2
    r = lax.broadcasted_iota(jnp.int32, (LANES, LANES), 0)
    cidx = lax.broadcasted_iota(jnp.int32, (LANES, LANES), 1)
    blockdiag = (r // 64) == (cidx // 64)
    lane = _lane_iota((C, LANES))
    lo = lane < 64

    def state_pass(kr, vr, stf, stb, n):
        def sums(c, _):
            c0 = pl.multiple_of(c * C, C)
            for j in range(NG):
                gs = slice(j * LANES, (j + 1) * LANES)
                kf = kr[0, pl.ds(c0, C), gs].astype(F32)
                kd = jnp.concatenate([(kf * kdf_ref[:, gs]).astype(BF16),
                                      (kf * kdb_ref[:, gs]).astype(BF16)], axis=1)
                kv = _dot_tn(kd, vr[0, pl.ds(c0, C), gs])
                stf[c + 1, j] = jnp.where(blockdiag, kv[:LANES], 0.0)
                stb[c, j] = jnp.where(blockdiag, kv[LANES:], 0.0)
            return 0
        lax.fori_loop(0, n, sums, 0, unroll=min(4, n))

        def scan(t, _):
            cf, cb = t, n - 1 - t
            for j in range(NG):
                gs = slice(j * LANES, (j + 1) * LANES)
                stf[cf + 1, j] = stf[cf, j] * cdf_ref[:, gs] + stf[cf + 1, j]
                stb[cb, j] = stb[cb + 1, j] * cdb_ref[:, gs] + stb[cb, j]
            return 0
        lax.fori_loop(0, n, scan, 0, unroll=min(4, n))

    zero = jnp.zeros((NG, LANES, LANES), F32)
    sfx[0] = zero
    sbx[ncx] = zero
    state_pass(kx_ref, vx_ref, sfx, sbx, ncx)
    sf[0] = sfx[ncx]
    sb[nc] = sbx[0]
    state_pass(k_ref, v_ref, sf, sb, nc)

    def out_pass(qr, kr, vr, gr, orf, stf, stb, n):
        def body(c, _):
            c0 = pl.multiple_of(c * C, C)
            for j in range(NG):
                gs = slice(j * LANES, (j + 1) * LANES)
                qg, kg, vg = qr[0, pl.ds(c0, C), gs], kr[0, pl.ds(c0, C), gs], vr[0, pl.ds(c0, C), gs]
                halves = []
                for half in range(2):
                    zq = jnp.zeros_like(qg)
                    qh = jnp.where(lo, qg, zq) if half == 0 else jnp.where(lo, zq, qg)
                    att = _dot_nt(qh, kg) * d_ref[2 * j + half]
                    halves.append(_dot(att.astype(BF16), vg))
                o = jnp.where(lo, halves[0], halves[1])
                qf = qg.astype(F32)
                qd = jnp.concatenate([(qf * qdf_ref[:, gs]).astype(BF16),
                                      (qf * qdb_ref[:, gs]).astype(BF16)], axis=1)
                s_cat = jnp.concatenate([stf[c, j], stb[c + 1, j]], axis=0).astype(BF16)
                o = o + _dot(qd, s_cat)
                o2 = o * o
                ms = jnp.where(lo, jnp.sum(jnp.where(lo, o2, 0.0), axis=-1, keepdims=True),
                               jnp.sum(jnp.where(lo, 0.0, o2), axis=-1, keepdims=True)) * (1.0 / RET_DV)
                gate = gr[0, pl.ds(c0, C), gs].astype(F32)
                y = o * lax.rsqrt(ms + NORM_EPS) * (gate * jax.nn.sigmoid(gate))
                orf[0, pl.ds(c0, C), gs] = y.astype(orf.dtype)
            return 0
        lax.fori_loop(0, n, body, 0, unroll=min(4, n))

    out_pass(q_ref, k_ref, v_ref, g_ref, o_ref, sf, sb, nc)
    if ctx_out:
        out_pass(qx_ref, kx_ref, vx_ref, gx_ref, ox_ref, sfx, sbx, ncx)


def _retention(q, k, v, g, qx, kx, vx, gx, tabs, ctx_out):
    B, L, W = q.shape
    Lc = qx.shape[1]
    nc, ncx = L // RET_CHUNK, Lc // RET_CHUNK
    lat = pl.BlockSpec((1, L, W), lambda b: (b, 0, 0))
    cx = pl.BlockSpec((1, Lc, W), lambda b: (b, 0, 0))
    full = lambda a: pl.BlockSpec(a.shape, lambda b: (0,) * a.ndim)
    out_shape = [jax.ShapeDtypeStruct((B, L, W), BF16)]
    out_specs = [lat]
    if ctx_out:
        out_shape.append(jax.ShapeDtypeStruct((B, Lc, W), BF16))
        out_specs.append(cx)
    res = pl.pallas_call(
        functools.partial(_ret_kernel, ctx_out=ctx_out),
        out_shape=out_shape,
        grid=(B,),
        in_specs=[lat, lat, lat, lat, cx, cx, cx, cx] + [full(t) for t in tabs],
        out_specs=out_specs,
        scratch_shapes=[pltpu.VMEM((n + 1, RET_HEADS // 2, LANES, LANES), F32) for n in (nc, nc, ncx, ncx)],
        compiler_params=_cparams(("arbitrary",)),
        name="retention",
    )(q, k, v, g, qx, kx, vx, gx, *tabs)
    return (res[0], res[1]) if ctx_out else (res[0], None)


def _swiglu_act(g, u):
    h = 0.5 * g
    return ((h + h * jnp.tanh(h)) * u).astype(BF16)


def _swiglu_chunks(x, wg, wu, wd, bounds):
    acc = None
    for c0, c1 in bounds:
        part = _dot(_swiglu_act(_dot(x, wg[:, c0:c1]), _dot(x, wu[:, c0:c1])), wd[c0:c1, :])
        acc = part if acc is None else acc + part
    return acc


def _chunk_bounds(total, size):
    return tuple((c, min(c + size, total)) for c in range(0, total, size))


def _split_hi_lo(a):
    hi = a.astype(BF16)
    return hi, (a - hi.astype(F32)).astype(BF16)


def _mix_out(oa_ref, ob_ref, oc_ref, x_ref, g1_ref, a2_ref, sh2_ref, wo_ref):
    y = (_dot(oa_ref[0], wo_ref[0:256, :]) + _dot(ob_ref[0], wo_ref[256:768, :])
         + _dot(oc_ref[0], wo_ref[768:1024, :]))
    x1 = x_ref[0] + g1_ref[0] * y
    h2 = x1 * lax.rsqrt(jnp.mean(x1 * x1, axis=-1, keepdims=True) + NORM_EPS) * a2_ref[0] + sh2_ref[0]
    return x1, h2


def _out_ffn_kernel(oa_ref, ob_ref, oc_ref, x_ref, g1_ref, a2_ref, sh2_ref, g2_ref, wo_ref,
                    wg_ref, wu_ref, wd_ref, o_ref, *, bounds):
    x1, h2 = _mix_out(oa_ref, ob_ref, oc_ref, x_ref, g1_ref, a2_ref, sh2_ref, wo_ref)
    o_ref[0] = x1 + g2_ref[0] * _swiglu_chunks(h2.astype(BF16), wg_ref, wu_ref, wd_ref, bounds)


def _out_ffn(oa, ob, oc, x, g1, a2, sh2, g2, wo, wg, wu, wd, tm):
    B, L, D = x.shape
    tm = min(tm, L)
    bm = (lambda b: b) if g1.shape[0] == B else (lambda b: 0)
    tok = lambda w: pl.BlockSpec((1, tm, w), lambda b, i: (b, i, 0))
    vec = pl.BlockSpec((1, 1, D), lambda b, i: (bm(b), 0, 0))
    wspec = lambda w: pl.BlockSpec(w.shape, lambda b, i: (0, 0), pipeline_mode=pl.Buffered(1))
    return pl.pallas_call(
        functools.partial(_out_ffn_kernel, bounds=_chunk_bounds(wg.shape[1], TF_SUB)),
        out_shape=jax.ShapeDtypeStruct((B, L, D), F32),
        grid=(B, L // tm),
        in_specs=[tok(256), tok(512), tok(256), tok(D), vec, vec, vec, vec,
                  wspec(wo), wspec(wg), wspec(wu), wspec(wd)],
        out_specs=tok(D),
        compiler_params=_cparams(("arbitrary", "arbitrary")),
        name="out_proj_dense_ffn",
    )(oa, ob, oc, x, g1, a2, sh2, g2, wo, wg, wu, wd)


def _out_proj_kernel(oa_ref, ob_ref, oc_ref, x_ref, g1_ref, a2_ref, sh2_ref, wo_ref, r_ref, tri_ref,
                     x1_ref, h2_ref, rt_ref, rtt_ref, cnt_ref, base_ref):
    @pl.when((pl.program_id(0) == 0) & (pl.program_id(1) == 0))
    def _():
        base_ref[...] = jnp.zeros_like(base_ref)

    x1, h2 = _mix_out(oa_ref, ob_ref, oc_ref, x_ref, g1_ref, a2_ref, sh2_ref, wo_ref)
    x1_ref[0] = x1
    h2_ref[0] = h2
    tm = h2.shape[0]
    h_hi, h_lo = _split_hi_lo(h2)
    prod = _dot(jnp.concatenate([h_hi, h_lo], axis=0), r_ref[...])
    logits = prod[:tm] + pltpu.roll(prod[:tm], LANES - N_EXPERTS, 1) + prod[tm:]
    lane = _lane_iota(logits.shape)
    logits = jnp.where(lane < N_EXPERTS, logits, -jnp.inf)
    v1 = jnp.max(logits, axis=-1, keepdims=True)
    i1 = jnp.min(jnp.where(logits == v1, lane, LANES), axis=-1, keepdims=True)
    rest_l = jnp.where(lane == i1, -jnp.inf, logits)
    v2 = jnp.max(rest_l, axis=-1, keepdims=True)
    i2 = jnp.min(jnp.where(rest_l == v2, lane, LANES), axis=-1, keepdims=True)
    e2 = jnp.exp(v2 - v1)
    w1 = 1.0 / (1.0 + e2)
    w2 = e2 * w1
    oh1 = jnp.where(lane == i1, 1.0, 0.0)
    oh2 = jnp.where(lane == i2, 1.0, 0.0)
    both = oh1 + oh2
    seen = _dot(tri_ref[...], both.astype(BF16)) + base_ref[...]
    rank1 = jnp.sum(oh1 * seen, axis=-1, keepdims=True)
    rank2 = jnp.sum(oh2 * seen, axis=-1, keepdims=True)
    base_ref[...] += jnp.sum(both, axis=0, keepdims=True)
    cnt_ref[...] = base_ref[...]
    vals = (i1.astype(F32), i2.astype(F32), w1, w2, rank1, rank2)
    row = jnp.zeros_like(logits)
    for k, v in enumerate(vals):
        row = jnp.where(lane == k, v, row)
    rt_ref[0] = row
    rtt_ref[0] = row.T[:ROUTE_FIELDS]


ROUTE_FIELDS = 8


def _out_proj_route(oa, ob, oc, x, g1, a2, sh2, wo, router, tm):
    B, L, D = x.shape
    tm = min(tm, L)
    nt = L // tm
    tok = lambda w: pl.BlockSpec((1, tm, w), lambda b, i: (b, i, 0))
    vec = pl.BlockSpec((1, 1, D), lambda b, i: (b, 0, 0))
    const = lambda a: pl.BlockSpec(a.shape, lambda b, i: (0, 0))
    tri = jnp.asarray(np.tril(np.ones((tm, tm), np.float32), -1), BF16)
    return pl.pallas_call(
        _out_proj_kernel,
        out_shape=[jax.ShapeDtypeStruct((B, L, D), F32), jax.ShapeDtypeStruct((B, L, D), F32),
                   jax.ShapeDtypeStruct((B, L, LANES), F32),
                   jax.ShapeDtypeStruct((B * nt, ROUTE_FIELDS, tm), F32), jax.ShapeDtypeStruct((1, LANES), F32)],
        grid=(B, nt),
        in_specs=[tok(256), tok(512), tok(256), tok(D), vec, vec, vec, const(wo), const(router), const(tri)],
        out_specs=[tok(D), tok(D), tok(LANES),
                   pl.BlockSpec((1, ROUTE_FIELDS, tm), lambda b, i: (b * nt + i, 0, 0)),
                   pl.BlockSpec((1, LANES), lambda b, i: (0, 0))],
        scratch_shapes=[pltpu.VMEM((1, LANES), F32)],
        compiler_params=_cparams(("arbitrary", "arbitrary")),
        name="out_proj_route",
    )(oa, ob, oc, x, g1, a2, sh2, wo, router, tri)


ROW_UNROLL = 16
ZERO_ROWS = 256


def _dispatch_kernel(ends_ref, pad_ref, d0_ref, d1_ref, h_ref, wg_ref, wu_ref, wd_ref,
                     o_ref, wgb_ref, wub_ref, wdb_ref, zeros_ref, sem, zsem, *, tm):
    rows = h_ref.shape[0]
    zr = zeros_ref.shape[0]
    for src, dst in ((wg_ref, wgb_ref), (wu_ref, wub_ref), (wd_ref, wdb_ref)):
        dst[...] = src[...].astype(dst.dtype)

    @pl.when(pl.program_id(0) == 0)
    def _():
        zeros_ref[...] = jnp.zeros_like(zeros_ref)

        def clear(row0, part):
            dst = o_ref.at[pl.ds(pl.multiple_of(row0, tm) + part * zr, zr)]
            return pltpu.make_async_copy(zeros_ref, dst, zsem)
        jobs = [(pad_ref[e] > 0, ends_ref[e] - tm) for e in range(N_EXPERTS)]
        total = ends_ref[N_EXPERTS - 1]
        jobs += [(total + t * tm < o_ref.shape[0], total + t * tm) for t in range(N_EXPERTS)]
        for cond, row0 in jobs:
            @pl.when(cond)
            def _():
                for part in range(tm // zr):
                    clear(row0, part).start()
        for cond, row0 in jobs:
            @pl.when(cond)
            def _():
                for part in range(tm // zr):
                    clear(row0, part).wait()

    def start(r, _):
        pltpu.make_async_copy(h_ref.at[pl.ds(r, 1)], o_ref.at[pl.ds(d0_ref[0, 0, r], 1)], sem).start(priority=0)
        pltpu.make_async_copy(h_ref.at[pl.ds(r, 1)], o_ref.at[pl.ds(d1_ref[0, 0, r], 1)], sem).start(priority=1)
        return 0

    lax.fori_loop(0, rows, start, 0, unroll=ROW_UNROLL)
    for _ in range(2):
        pltpu.make_async_copy(h_ref, o_ref.at[pl.ds(0, rows)], sem).wait()


def _dispatch(h, d0, d1, ends, padding, n_slots, tm, td, wg, wu, wd):
    N, D = h.shape
    steps = N // td
    idx_spec = pl.BlockSpec((1, 1, td), lambda i, *_: (i, 0, 0), memory_space=pltpu.SMEM)
    flat = [w.reshape(-1, w.shape[-1]) for w in (wg, wu, wd)]
    w_specs = [pl.BlockSpec((w.shape[0] // steps, w.shape[1]), lambda i, *_: (i, 0)) for w in flat]
    res = pl.pallas_call(
        functools.partial(_dispatch_kernel, tm=tm),
        out_shape=[jax.ShapeDtypeStruct((n_slots, D), h.dtype)]
        + [jax.ShapeDtypeStruct(w.shape, BF16) for w in flat],
        grid_spec=pltpu.PrefetchScalarGridSpec(
            num_scalar_prefetch=2,
            grid=(steps,),
            in_specs=[idx_spec, idx_spec, pl.BlockSpec((td, D), lambda i, *_: (i, 0))] + w_specs,
            out_specs=[pl.BlockSpec(memory_space=pl.ANY)] + w_specs,
            scratch_shapes=[pltpu.VMEM((ZERO_ROWS, D), h.dtype), pltpu.SemaphoreType.DMA,
                            pltpu.SemaphoreType.DMA]),
        compiler_params=_cparams(("arbitrary",)),
        name="moe_dispatch",
    )(ends, padding, d0.reshape(steps, 1, td), d1.reshape(steps, 1, td), h, *flat)
    return res[0], res[1].reshape(wg.shape), res[2].reshape(wu.shape), res[3].reshape(wd.shape)


def _gmm_kernel(te_ref, nu_ref, x_ref, wg_ref, wu_ref, wd_ref, o_ref, *, bounds):
    @pl.when(pl.program_id(0) < nu_ref[0])
    def _():
        o_ref[...] = _swiglu_chunks(x_ref[...].astype(BF16), wg_ref.at[0], wu_ref.at[0], wd_ref.at[0], bounds)

    @pl.when(pl.program_id(0) >= nu_ref[0])
    def _():
        o_ref[...] = jnp.zeros_like(o_ref)


def _grouped_swiglu(xs, tile_expert, n_used, wg, wu, wd, tm):
    P, D = xs.shape
    E, _, F = wg.shape
    resident = lambda shape: pl.BlockSpec((1,) + shape, lambda i, te, nu: (te[i], 0, 0),
                                          pipeline_mode=pl.Buffered(1))
    return pl.pallas_call(
        functools.partial(_gmm_kernel, bounds=_chunk_bounds(F, TF_SUB)),
        out_shape=jax.ShapeDtypeStruct((P, D), F32),
        grid_spec=pltpu.PrefetchScalarGridSpec(
            num_scalar_prefetch=2,
            grid=(P // tm,),
            in_specs=[pl.BlockSpec((tm, D), lambda i, te, nu: (jnp.minimum(i, nu[0] - 1), 0)),
                      resident((D, F)), resident((D, F)), resident((F, D))],
            out_specs=pl.BlockSpec((tm, D), lambda i, te, nu: (i, 0))),
        compiler_params=_cparams(("arbitrary",)),
        name="moe_grouped_swiglu",
    )(tile_expert, n_used, xs, wg, wu, wd)


def _combine_kernel(d0_ref, d1_ref, y_ref, x_ref, g2_ref, rt_ref, fg_ref, o_ref, buf, sem):
    rows = o_ref.shape[1]

    def start(r, _):
        pltpu.make_async_copy(y_ref.at[pl.ds(d0_ref[0, 0, r], 1)], buf.at[0, pl.ds(r, 1)], sem).start(priority=0)
        pltpu.make_async_copy(y_ref.at[pl.ds(d1_ref[0, 0, r], 1)], buf.at[1, pl.ds(r, 1)], sem).start(priority=1)
        return 0

    lax.fori_loop(0, rows, start, 0, unroll=ROW_UNROLL)
    for k in range(2):
        pltpu.make_async_copy(y_ref.at[pl.ds(0, rows)], buf.at[k], sem).wait()
    rt = rt_ref[0]
    moe = rt[:, 2:3] * buf[0] + rt[:, 3:4] * buf[1]
    x2 = x_ref[0] + g2_ref[0] * moe
    o_ref[0] = x2 * lax.rsqrt(jnp.mean(x2 * x2, axis=-1, keepdims=True) + NORM_EPS) * fg_ref[...]


def _combine(y, d0, d1, x1, g2, rt, final_g, tc):
    B, L, D = x1.shape
    tc = min(tc, L)
    nt = L // tc
    idx_spec = pl.BlockSpec((1, 1, tc), lambda b, i: (b * nt + i, 0, 0), memory_space=pltpu.SMEM)
    tok = lambda w: pl.BlockSpec((1, tc, w), lambda b, i: (b, i, 0))
    return pl.pallas_call(
        _combine_kernel,
        out_shape=jax.ShapeDtypeStruct((B, L, D), F32),
        grid=(B, nt),
        in_specs=[idx_spec, idx_spec, pl.BlockSpec(memory_space=pl.ANY), tok(D),
                  pl.BlockSpec((1, 1, D), lambda b, i: (b, 0, 0)), tok(LANES),
                  pl.BlockSpec((1, D), lambda b, i: (0, 0))],
        out_specs=tok(D),
        scratch_shapes=[pltpu.VMEM((2, tc, D), F32), pltpu.SemaphoreType.DMA],
        compiler_params=_cparams(("arbitrary", "arbitrary")),
        name="moe_combine_norm",
    )(d0.reshape(B * nt, 1, tc), d1.reshape(B * nt, 1, tc), y, x1, g2, rt, final_g.reshape(1, D))


def _routing(rtt, counts, tm):
    field = lambda k: rtt[:, k, :].reshape(-1).astype(jnp.int32)
    n = rtt.shape[0] * rtt.shape[2]
    counts = counts[0, :N_EXPERTS].astype(jnp.int32)
    padded = ((counts + tm - 1) // tm) * tm
    ends = jnp.cumsum(padded)
    offs = ends - padded

    def slot(e, rank):
        start = jnp.zeros_like(e)
        for j in range(N_EXPERTS):
            start = jnp.where(e == j, offs[j], start)
        return start + rank
    dest = (slot(field(0), field(4)), slot(field(1), field(5)))
    n_slots = 2 * n + N_EXPERTS * tm
    tile_start = jnp.arange(n_slots // tm, dtype=jnp.int32) * tm
    n_used = (ends[-1] // tm).astype(jnp.int32)
    te = jnp.sum((tile_start[:, None] >= ends[None, :]).astype(jnp.int32), axis=1)
    last = jnp.sum((ends[-1] - tm >= ends).astype(jnp.int32))
    te = jnp.where(tile_start < ends[-1], te, last).astype(jnp.int32)
    return (n_slots, te, n_used.reshape(1), dest[0], dest[1], ends.astype(jnp.int32),
            (padded - counts).astype(jnp.int32))


TM_IN = 512
TQ_MLA = 256
TM_OUT = 512
TM_FFN = 512
TM_MOE = 512
TF_SUB = 512
TD_DISPATCH = 512
TC_COMBINE = 512


def kernel(x, c, ctx, c_ctx, w_mod, b_mod, norm1_g, norm2_g, w_in, mla_q_norm, mla_w_uq, mla_kv_norm,
           mla_w_ukv, swa_sink, ret_decay_fwd, ret_decay_bwd, w_out, ffn_w_gate, ffn_w_up, ffn_w_down,
           moe_router, moe_w_gate, moe_w_up, moe_w_down, final_norm_g):
    B, L, D = x.shape
    Lc = ctx.shape[1]
    depth = w_mod.shape[0]
    xc = ctx

    cond = jnp.concatenate([c, c_ctx[None], jnp.zeros((16 - B - 1, D), F32)], axis=0)
    mod_all = _modulation(cond, w_mod, b_mod)
    tables = _rope_tables(L)

    for layer in range(depth):
        last = layer == depth - 1
        mod = mod_all[layer].reshape(16, 6, 1, D)
        sh1, sc1, g1, sh2, sc2, g2 = (mod[:B, j] for j in range(6))
        sh1x, sc1x, g1x, sh2x, sc2x, g2x = (mod[B:B + 1, j] for j in range(6))
        n1, n2 = norm1_g[layer], norm2_g[layer]

        wts = _prep_in_weights(w_in[layer], mla_q_norm[layer], mla_w_uq[layer],
                               mla_kv_norm[layer], mla_w_ukv[layer])
        lat = _in_proj(x, n1 * (1.0 + sc1), sh1, wts, tables, TM_IN)
        cx = _in_proj(xc, n1 * (1.0 + sc1x), sh1x, wts, None, TM_IN)
        qm, km, vm, sq, sk, sv, rq, rk, rv, rg = lat
        qmx, kmx, vmx, sqx, skx, svx, rqx, rkx, rvx, rgx = cx

        o_a = _mla_attention(qm, [(km, vm), (kmx, vmx)], TQ_MLA)
        sink = swa_sink[layer].astype(F32) * LOG2E
        o_b = _swa_attention(sink, sq, sk, sv, skx, svx, True)
        rtabs = _ret_tables(ret_decay_fwd[layer], ret_decay_bwd[layer])
        o_c, oc_c = _retention(rq, rk, rv, rg, rqx, rkx, rvx, rgx, rtabs, not last)
        wo = w_out[layer].astype(BF16)

        if layer % 2 == 0:
            i = layer // 2
            wg, wu, wd = (ffn_w_gate[i].astype(BF16), ffn_w_up[i].astype(BF16),
                          ffn_w_down[i].astype(BF16))
            x_next = _out_ffn(o_a, o_b, o_c, x, g1, n2 * (1.0 + sc2), sh2, g2, wo, wg, wu, wd, TM_FFN)
        else:
            i = layer // 2
            r_hi = moe_router[i].astype(BF16)
            r_lo = (moe_router[i] - r_hi.astype(F32)).astype(BF16)
            router = jnp.pad(jnp.concatenate([r_hi, r_lo], axis=1), ((0, 0), (0, LANES - 2 * N_EXPERTS)))
            x1, h2, rt, rtt, counts = _out_proj_route(o_a, o_b, o_c, x, g1, n2 * (1.0 + sc2), sh2, wo, router,
                                                      TM_OUT)
            n_slots, te, n_used, d0, d1, ends, padding = _routing(rtt, counts, TM_MOE)
            xs, ewg, ewu, ewd = _dispatch(h2.reshape(B * L, D), d0, d1, ends, padding, n_slots, TM_MOE,
                                          TD_DISPATCH, moe_w_gate[i], moe_w_up[i], moe_w_down[i])
            y = _grouped_swiglu(xs, te, n_used, ewg, ewu, ewd, TM_MOE)
            if last:
                return _combine(y, d0, d1, x1, g2, rt, final_norm_g, TC_COMBINE)
            raise NotImplementedError("expert layer is only supported as the last layer")

        if not last:
            oc_a = _mla_attention(qmx, [(kmx, vmx)], TQ_MLA)
            oc_b = _swa_attention(sink, sqx, skx, svx, skx, svx, False)
            xc = _out_ffn(oc_a, oc_b, oc_c, xc, g1x, n2 * (1.0 + sc2x), sh2x, g2x, wo, wg, wu, wd, TM_FFN)
        x = x_next
    raise NotImplementedError("trunk must end with the expert layer")
```

```python
import functools
import math

import numpy as np
import jax
import jax.numpy as jnp
from jax import lax
from jax.experimental import pallas as pl
from jax.experimental.pallas import tpu as pltpu

F32 = jnp.float32
BF16 = jnp.bfloat16

D_MODEL = 1024
DEPTH = 2
GRID_W = 64
HEAD_DIM = 64
NORM_EPS = 1e-6
ROPE_BASE = 10000.0
NEG_INF = -1e30

MLA_HEADS = 4
MLA_Q_RANK = 192
MLA_KV_RANK = 128
MLA_NOPE = 64
MLA_ROPE = 32
MLA_V = 64

SWA_Q_HEADS = 8
SWA_KV_HEADS = 2
SWA_BLOCK = 128

RET_HEADS = 4
RET_DK = 64
RET_DV = 64
RET_CHUNK = 128

D_FF = 2816
N_EXPERTS = 8
D_FF_EXPERT = 3584

LOG2E = math.log2(math.e)
MLA_ONE_LANE = (64, 0)
LANES = 128
VMEM_LIMIT = 56 * 1024 * 1024

C_SQ, C_SK, C_SV = 0, 512, 768
C_RQ, C_RK, C_RV, C_RG = 1024, 1280, 1536, 1792
C_CKV, C_EXT = 2048, 2176
IN_COLS = 2432


def _cparams(sem, vmem=VMEM_LIMIT):
    return pltpu.CompilerParams(dimension_semantics=sem, vmem_limit_bytes=vmem)


def _dot(a, b):
    return jnp.dot(a, b, preferred_element_type=F32)


def _dot_nt(a, b):
    return lax.dot_general(a, b, (((1,), (1,)), ((), ())), preferred_element_type=F32)


def _dot_tn(a, b):
    return lax.dot_general(a, b, (((0,), (0,)), ((), ())), preferred_element_type=F32)


def _lane_iota(shape):
    return lax.broadcasted_iota(jnp.int32, shape, len(shape) - 1)


def _mod_kernel(c_ref, w_ref, b_ref, o_ref):
    c = c_ref[...]
    c = c * jax.nn.sigmoid(c)
    o_ref[0] = jnp.dot(c, w_ref[0], preferred_element_type=F32,
                       precision=lax.Precision.HIGHEST) + b_ref[0]


def _modulation(cond, w_mod, b_mod):
    depth, d, n = w_mod.shape
    rows = cond.shape[0]
    tn = 2048
    return pl.pallas_call(
        _mod_kernel,
        out_shape=jax.ShapeDtypeStruct((depth, rows, n), F32),
        grid=(depth, n // tn),
        in_specs=[pl.BlockSpec((rows, d), lambda l, j: (0, 0)),
                  pl.BlockSpec((1, d, tn), lambda l, j: (l, 0, j)),
                  pl.BlockSpec((1, 1, tn), lambda l, j: (l, 0, j))],
        out_specs=pl.BlockSpec((1, rows, tn), lambda l, j: (l, 0, j)),
        compiler_params=_cparams(("arbitrary", "arbitrary")),
        name="modulation",
    )(cond, w_mod, b_mod.reshape(depth, 1, n))


def _angles(pos, dim):
    inv = (ROPE_BASE ** (-np.arange(0, dim, 2, dtype=np.float32) / dim)).astype(np.float32)
    ang = pos.astype(np.float32)[:, None] * inv[None, :]
    return np.concatenate([ang, ang], axis=-1).astype(np.float64)


def _rope_tables(length):
    t = np.arange(length)
    rows, cols = t // GRID_W, t % GRID_W
    ar, ac = _angles(rows, 32), _angles(cols, 32)
    sign32 = np.concatenate([-np.ones(16), np.ones(16)])
    cos_a = np.concatenate([np.cos(ar), np.cos(ac)], axis=-1)
    sin_a = np.concatenate([np.sin(ar) * sign32, np.sin(ac) * sign32], axis=-1)
    cos_a, sin_a = np.tile(cos_a, (1, 2)), np.tile(sin_a, (1, 2))
    at = _angles(t, 64)
    sign64 = np.concatenate([-np.ones(32), np.ones(32)])
    cos_r, sin_r = np.tile(np.cos(at), (1, 2)), np.tile(np.sin(at) * sign64, (1, 2))
    mr, mc = _angles(rows, 16), _angles(cols, 16)
    cos_m = np.ones((length, LANES))
    sin_m = np.zeros((length, LANES))
    cos_m[:, 64:96] = np.concatenate([np.cos(mr), np.cos(mc)], axis=-1)
    sin_m[:, 64:96] = np.concatenate([np.sin(mr), np.sin(mc)], axis=-1)
    return tuple(jnp.asarray(a, F32) for a in (cos_a, sin_a, cos_r, sin_r, cos_m, sin_m))


def _prep_in_weights(w_in, q_norm, w_uq, kv_norm, w_ukv):
    cuts = np.cumsum([MLA_Q_RANK, MLA_KV_RANK, MLA_ROPE, 512, 128, 128, 256, 256, 256, 256])[:-1]
    cq, ckv, kpe, sq, sk, sv, rq, rk, rv, rg = jnp.split(w_in, [int(v) for v in cuts], axis=1)
    d = w_in.shape[0]
    z64 = jnp.zeros((d, 64), F32)
    dup = lambda w: jnp.concatenate([w[:, :64], w[:, :64], w[:, 64:], w[:, 64:]], axis=1)
    low = lambda w: jnp.concatenate([w[:, :64], z64, w[:, 64:], z64], axis=1)
    w_main = jnp.concatenate(
        [sq * (HEAD_DIM ** -0.5 * LOG2E), dup(sk), low(sv), rq, rk * RET_DK ** -0.5, rv, rg, ckv,
         cq, kpe, jnp.zeros((d, 32), F32)], axis=1).astype(BF16)

    scale = (MLA_NOPE + MLA_ROPE) ** -0.5 * LOG2E
    wq = (w_uq * scale).reshape(MLA_Q_RANK, MLA_HEADS, MLA_NOPE + MLA_ROPE)
    wq = jnp.pad(wq, ((0, 64), (0, 0), (0, 32))).reshape(256, 512)
    place = np.zeros((256, 512), np.float32)
    for h in range(MLA_HEADS):
        for dd in range(MLA_ROPE):
            place[MLA_Q_RANK + dd, h * LANES + MLA_NOPE + dd] = 1.0
    wz = jnp.concatenate([wq, jnp.asarray(place)], axis=1)
    src = np.arange(1024)
    sign = np.zeros(1024, np.float32)
    for g in range(8):
        for dd in range(MLA_ROPE):
            col = g * LANES + MLA_NOPE + dd
            src[col] = col + 8 if dd % 16 < 8 else col - 8
            sign[col] = -1.0 if dd % 16 < 8 else 1.0
    wz_rot = wz[:, src] * jnp.asarray(sign)
    qn_ext = jnp.pad(q_norm, (0, 64)).reshape(1, 256)

    wkv = w_ukv.reshape(MLA_KV_RANK, MLA_HEADS, MLA_NOPE + MLA_V)
    kn = jnp.pad(wkv[:, :, :MLA_NOPE], ((0, 0), (0, 0), (0, 64))).reshape(MLA_KV_RANK, 512)
    vals = wkv[:, :, MLA_NOPE:]
    vv = jnp.stack([jnp.pad(vals[:, h], ((0, 0), (64, 0) if h % 2 else (0, 64))) for h in range(MLA_HEADS)],
                   axis=1).reshape(MLA_KV_RANK, 512)
    w_kv = jnp.concatenate([kn, vv], axis=1)
    return (w_main, wz.astype(BF16), wz_rot.astype(BF16), qn_ext, w_kv.astype(BF16),
            kv_norm.reshape(1, MLA_KV_RANK))


def _rope_roll(x, cos, sin_signed, half):
    lane = _lane_iota(x.shape)
    rot = jnp.where((lane % (2 * half)) < half,
                    pltpu.roll(x, LANES - half, 1), pltpu.roll(x, half, 1))
    return x * cos + rot * sin_signed


def _in_proj_kernel(*refs, rope):
    if rope:
        (x_ref, a_ref, sh_ref, w_ref, wz_ref, wzr_ref, qn_ref, wkv_ref, kvn_ref,
         ca_ref, sa_ref, cr_ref, sr_ref, cm_ref, sm_ref, *outs) = refs
    else:
        (x_ref, a_ref, sh_ref, w_ref, wz_ref, wzr_ref, qn_ref, wkv_ref, kvn_ref, *outs) = refs
    qm_ref, km_ref, vm_ref, sq_ref, sk_ref, sv_ref, rq_ref, rk_ref, rv_ref, rg_ref = outs

    x = x_ref[0]
    h = x * lax.rsqrt(jnp.mean(x * x, axis=-1, keepdims=True) + NORM_EPS) * a_ref[0] + sh_ref[0]
    p = _dot(h.astype(BF16), w_ref[...])

    lane_g = _lane_iota((x.shape[0], LANES))

    def put(ref, col, width, tables=None, half=None, ones_lane=None):
        for g in range(width // LANES):
            blk = p[:, col + g * LANES: col + (g + 1) * LANES]
            if tables is not None:
                blk = _rope_roll(blk, tables[0][...], tables[1][...], half)
            if ones_lane is not None:
                blk = jnp.where(lane_g == ones_lane, 1.0, blk)
            ref[0, :, g * LANES:(g + 1) * LANES] = blk.astype(ref.dtype)

    axial = (ca_ref, sa_ref) if rope else None
    flat = (cr_ref, sr_ref) if rope else None
    put(sq_ref, C_SQ, 512, axial, 16)
    put(sk_ref, C_SK, 256, axial, 16)
    put(sv_ref, C_SV, 256, ones_lane=SWA_ONE_LANE)
    put(rq_ref, C_RQ, 256, flat, 32)
    put(rk_ref, C_RK, 256, flat, 32)
    put(rv_ref, C_RV, 256)
    put(rg_ref, C_RG, 256)

    ext = p[:, C_EXT:C_EXT + 256]
    lane = _lane_iota(ext.shape)
    is_cq = lane < MLA_Q_RANK
    cq_sq = jnp.where(is_cq, ext * ext, 0.0)
    inv = lax.rsqrt(jnp.sum(cq_sq, axis=-1, keepdims=True) * (1.0 / MLA_Q_RANK) + NORM_EPS)
    z = jnp.where(is_cq, ext * inv * qn_ref[...], ext).astype(BF16)
    zw = _dot(z, wz_ref[...])
    ckv = p[:, C_CKV:C_CKV + MLA_KV_RANK]
    ckv = ckv * lax.rsqrt(jnp.mean(ckv * ckv, axis=-1, keepdims=True) + NORM_EPS) * kvn_ref[...]
    kv = _dot(ckv.astype(BF16), wkv_ref[...])
    if rope:
        zr = _dot(z, wzr_ref[...])
    for g in range(MLA_HEADS):
        sl = slice(g * LANES, (g + 1) * LANES)
        sk_ = slice(512 + g * LANES, 512 + (g + 1) * LANES)
        q_g, kpe_g = zw[:, sl], zw[:, sk_]
        if rope:
            q_g = q_g * cm_ref[...] + zr[:, sl] * sm_ref[...]
            kpe_g = kpe_g * cm_ref[...] + zr[:, sk_] * sm_ref[...]
        qm_ref[0, :, sl] = q_g.astype(BF16)
        km_ref[0, :, sl] = (kv[:, sl] + kpe_g).astype(BF16)
        vm_ref[0, :, sl] = jnp.where(lane_g == MLA_ONE_LANE[g % 2], 1.0, kv[:, sk_]).astype(BF16)


def _in_proj(x, a, sh, wts, tables, tm):
    B, L, D = x.shape
    w_main, wz, wzr, qn_ext, w_kv, kvn = wts
    rope = tables is not None
    tm = min(tm, L)
    bm = (lambda b: b) if a.shape[0] == B else (lambda b: 0)
    const = lambda i, b: (0, 0)
    in_specs = [pl.BlockSpec((1, tm, D), lambda i, b: (b, i, 0)),
                pl.BlockSpec((1, 1, D), lambda i, b: (bm(b), 0, 0)),
                pl.BlockSpec((1, 1, D), lambda i, b: (bm(b), 0, 0)),
                pl.BlockSpec(w_main.shape, const), pl.BlockSpec(wz.shape, const),
                pl.BlockSpec(wzr.shape, const), pl.BlockSpec(qn_ext.shape, const),
                pl.BlockSpec(w_kv.shape, const), pl.BlockSpec(kvn.shape, const)]
    args = [x, a, sh, w_main, wz, wzr, qn_ext, w_kv, kvn]
    if rope:
        in_specs += [pl.BlockSpec((tm, LANES), lambda i, b: (i, 0))] * 6
        args += list(tables)
    widths = (512, 512, 512, 512, 256, 256, 256, 256, 256, 256)
    return pl.pallas_call(
        functools.partial(_in_proj_kernel, rope=rope),
        out_shape=[jax.ShapeDtypeStruct((B, L, w), BF16) for w in widths],
        grid=(L // tm, B),
        in_specs=in_specs,
        out_specs=[pl.BlockSpec((1, tm, w), lambda i, b: (b, i, 0)) for w in widths],
        compiler_params=_cparams(("arbitrary", "arbitrary")),
        name="in_proj_rope" if rope else "in_proj_ctx",
    )(*args)


def _mla_kernel(q_ref, *refs, n_sets):
    if n_sets > 1:
        *kv_refs, o_ref, k_all, v_all = refs

        @pl.when(pl.program_id(1) == 0)
        def _():
            row = 0
            for k_ref, v_ref in zip(kv_refs[0::2], kv_refs[1::2]):
                n = k_ref.shape[1]
                k_all[row:row + n, :] = k_ref[0]
                v_all[row:row + n, :] = v_ref[0]
                row += n
        keys, values = (lambda sl: k_all[:, sl]), (lambda sl: v_all[:, sl])
    else:
        k_ref, v_ref, o_ref = refs
        keys, values = (lambda sl: k_ref[0, :, sl]), (lambda sl: v_ref[0, :, sl])
    lane = _lane_iota((q_ref.shape[1], LANES))
    outs = []
    for h in range(MLA_HEADS):
        sl = slice(h * LANES, (h + 1) * LANES)
        s = _dot_nt(q_ref[0, :, sl], keys(sl))
        p = jnp.exp2(s - jnp.max(s, axis=-1, keepdims=True))
        o = _dot(p.astype(BF16), values(sl))
        one = MLA_ONE_LANE[h % 2]
        outs.append(o * (1.0 / o[:, one:one + 1]))
    for g in range(MLA_HEADS // 2):
        o_ref[0, :, g * LANES:(g + 1) * LANES] = jnp.where(
            lane < 64, outs[2 * g], outs[2 * g + 1]).astype(o_ref.dtype)


def _mla_attention(qm, kvs, tq):
    B, L, _ = qm.shape
    tq = min(tq, L)
    flat = [a for kv in kvs for a in kv]
    lk = sum(k.shape[1] for k, _ in kvs)
    scratch = [pltpu.VMEM((lk, 512), BF16)] * 2 if len(kvs) > 1 else []
    return pl.pallas_call(
        functools.partial(_mla_kernel, n_sets=len(kvs)),
        out_shape=jax.ShapeDtypeStruct((B, L, MLA_HEADS * MLA_V), BF16),
        grid=(B, L // tq),
        in_specs=[pl.BlockSpec((1, tq, 512), lambda b, i: (b, i, 0))]
        + [pl.BlockSpec((1,) + a.shape[1:], lambda b, i: (b, 0, 0)) for a in flat],
        out_specs=pl.BlockSpec((1, tq, MLA_HEADS * MLA_V), lambda b, i: (b, i, 0)),
        scratch_shapes=scratch,
        compiler_params=_cparams(("arbitrary", "arbitrary")),
        name="mla_attention",
    )(qm, *flat)


SWA_ONE_LANE = 64
SWA_TQ = 2 * SWA_BLOCK
SWA_BAND = SWA_TQ + 2 * SWA_BLOCK


def _swa_bias():
    W = SWA_BLOCK
    qq = np.arange(2 * SWA_TQ)[:, None] % SWA_TQ
    kk = np.arange(SWA_BAND)[None, :]
    masks = [np.abs(kk - shift - qq) <= W for shift in (0, W, 2 * W)]
    return jnp.asarray(np.where(np.stack(masks), 0.0, NEG_INF), F32)


def _swa_kernel(sink_ref, q_ref, k_ref, v_ref, kc_ref, vc_ref, *rest, banded):
    W = SWA_BLOCK
    i = pl.program_id(1)
    if banded:
        bias_ref, o_ref = rest
        start = jnp.clip(i * SWA_TQ - W, 0, k_ref.shape[1] - SWA_BAND)
        start = pl.multiple_of(start, W)
        k_all = jnp.concatenate([k_ref[0, pl.ds(start, SWA_BAND), :], kc_ref[0]], axis=0)
        v_all = jnp.concatenate([v_ref[0, pl.ds(start, SWA_BAND), :], vc_ref[0]], axis=0)
    else:
        (o_ref,) = rest
        k_all, v_all = kc_ref[0], vc_ref[0]
    tq = q_ref.shape[1]
    pairs_per_kv = SWA_Q_HEADS // SWA_KV_HEADS // 2
    upper = lax.broadcasted_iota(jnp.int32, (2 * tq, 1), 0) >= tq
    lo = _lane_iota((tq, LANES)) < 64
    for j in range(SWA_Q_HEADS // 2):
        gs = slice((j // pairs_per_kv) * LANES, (j // pairs_per_kv + 1) * LANES)
        q_pair = q_ref[0, :, j * LANES:(j + 1) * LANES]
        zq = jnp.zeros_like(q_pair)
        q2 = jnp.concatenate([jnp.where(lo, q_pair, zq), jnp.where(lo, zq, q_pair)], axis=0)
        sink = jnp.where(upper, sink_ref[2 * j + 1], sink_ref[2 * j])
        s = _dot_nt(q2, k_all[:, gs])
        if banded:
            s = jnp.concatenate([s[:, :SWA_BAND] + bias_ref[0], s[:, SWA_BAND:]], axis=1)
        m = jnp.maximum(jnp.max(s, axis=-1, keepdims=True), sink)
        o = _dot(jnp.exp2(s - m).astype(BF16), v_all[:, gs])
        o = o * (1.0 / (o[:, SWA_ONE_LANE:SWA_ONE_LANE + 1] + jnp.exp2(sink - m)))
        o_ref[0, :, j * LANES:(j + 1) * LANES] = jnp.where(lo, o[:tq], pltpu.roll(o[tq:], 64, 1)).astype(o_ref.dtype)


def _swa_attention(sink, q, k, v, kc, vc, banded):
    B, L, _ = q.shape
    Lc = kc.shape[1]
    tq = SWA_TQ if banded else L
    nb = L // tq
    Lkv = k.shape[1]
    in_specs = [pl.BlockSpec(memory_space=pltpu.SMEM),
                pl.BlockSpec((1, tq, 512), lambda b, i: (b, i, 0)),
                pl.BlockSpec((1, Lkv, 256), lambda b, i: (b, 0, 0)),
                pl.BlockSpec((1, Lkv, 256), lambda b, i: (b, 0, 0)),
                pl.BlockSpec((1, Lc, 256), lambda b, i: (b, 0, 0)),
                pl.BlockSpec((1, Lc, 256), lambda b, i: (b, 0, 0))]
    args = [sink, q, k, v, kc, vc]
    if banded:
        assert nb >= 2 and L >= SWA_BAND, "band masks assume distinct first and last query tiles"
        bias = _swa_bias()
        in_specs.append(pl.BlockSpec((1,) + bias.shape[1:],
                                     lambda b, i: (jnp.where(i == 0, 0, jnp.where(i == nb - 1, 2, 1)), 0, 0)))
        args.append(bias)
    return pl.pallas_call(
        functools.partial(_swa_kernel, banded=banded),
        out_shape=jax.ShapeDtypeStruct((B, L, 512), BF16),
        grid=(B, nb),
        in_specs=in_specs,
        out_specs=pl.BlockSpec((1, tq, 512), lambda b, i: (b, i, 0)),
        compiler_params=_cparams(("arbitrary", "arbitrary")),
        name="swa_banded" if banded else "swa_context",
    )(*args)


def _ret_tables(decay_f, decay_b):
    C = RET_CHUNK
    lg_f = jnp.log(jax.nn.sigmoid(decay_f.astype(F32)))
    lg_b = jnp.log(jax.nn.sigmoid(decay_b.astype(F32)))
    idx = jnp.arange(C, dtype=F32)
    diff = idx[:, None] - idx[None, :]
    intra = (jnp.where(diff >= 0, jnp.exp(lg_f[:, None, None] * jnp.maximum(diff, 0.0)), 0.0)
             + jnp.where(diff <= 0, jnp.exp(lg_b[:, None, None] * jnp.maximum(-diff, 0.0)), 0.0))
    lanes = lambda t: jnp.repeat(t.T, RET_DK, axis=1)
    qdf = lanes(jnp.exp(lg_f[:, None] * (idx + 1.0)))
    qdb = lanes(jnp.exp(lg_b[:, None] * (C - idx)))
    kdf = lanes(jnp.exp(lg_f[:, None] * (C - 1.0 - idx)))
    kdb = lanes(jnp.exp(lg_b[:, None] * idx))
    cdf = jnp.repeat(jnp.exp(lg_f * C), RET_DV).reshape(1, -1)
    cdb = jnp.repeat(jnp.exp(lg_b * C), RET_DV).reshape(1, -1)
    return intra, qdf, qdb, kdf, kdb, cdf, cdb


def _ret_kernel(q_ref, k_ref, v_ref, g_ref, qx_ref, kx_ref, vx_ref, gx_ref,
                d_ref, qdf_ref, qdb_ref, kdf_ref, kdb_ref, cdf_ref, cdb_ref,
                *rest, ctx_out):
    if ctx_out:
        o_ref, ox_ref, sf, sb, sfx, sbx = rest
    else:
        o_ref, sf, sb, sfx, sbx = rest
        ox_ref = None
    C = RET_CHUNK
    nc = q_ref.shape[1] // C
    ncx = qx_ref.shape[1] // C
    NG = RET_HEADS // 2
    r = lax.broadcasted_iota(jnp.int32, (LANES, LANES), 0)
    cidx = lax.broadcasted_iota(jnp.int32, (LANES, LANES), 1)
    blockdiag = (r // 64) == (cidx // 64)
    lane = _lane_iota((C, LANES))
    lo = lane < 64

    def state_pass(kr, vr, stf, stb, n):
        def sums(c, _):
            c0 = pl.multiple_of(c * C, C)
            for j in range(NG):
                gs = slice(j * LANES, (j + 1) * LANES)
                kf = kr[0, pl.ds(c0, C), gs].astype(F32)
                kd = jnp.concatenate([(kf * kdf_ref[:, gs]).astype(BF16),
                                      (kf * kdb_ref[:, gs]).astype(BF16)], axis=1)
                kv = _dot_tn(kd, vr[0, pl.ds(c0, C), gs])
                stf[c + 1, j] = jnp.where(blockdiag, kv[:LANES], 0.0)
                stb[c, j] = jnp.where(blockdiag, kv[LANES:], 0.0)
            return 0
        lax.fori_loop(0, n, sums, 0, unroll=min(4, n))

        def scan(t, _):
            cf, cb = t, n - 1 - t
            for j in range(NG):
                gs = slice(j * LANES, (j + 1) * LANES)
                stf[cf + 1, j] = stf[cf, j] * cdf_ref[:, gs] + stf[cf + 1, j]
                stb[cb, j] = stb[cb + 1, j] * cdb_ref[:, gs] + stb[cb, j]
            return 0
        lax.fori_loop(0, n, scan, 0, unroll=min(4, n))

    zero = jnp.zeros((NG, LANES, LANES), F32)
    sfx[0] = zero
    sbx[ncx] = zero
    state_pass(kx_ref, vx_ref, sfx, sbx, ncx)
    sf[0] = sfx[ncx]
    sb[nc] = sbx[0]
    state_pass(k_ref, v_ref, sf, sb, nc)

    def out_pass(qr, kr, vr, gr, orf, stf, stb, n):
        def body(c, _):
            c0 = pl.multiple_of(c * C, C)
            for j in range(NG):
                gs = slice(j * LANES, (j + 1) * LANES)
                qg, kg, vg = qr[0, pl.ds(c0, C), gs], kr[0, pl.ds(c0, C), gs], vr[0, pl.ds(c0, C), gs]
                halves = []
                for half in range(2):
                    zq = jnp.zeros_like(qg)
                    qh = jnp.where(lo, qg, zq) if half == 0 else jnp.where(lo, zq, qg)
                    att = _dot_nt(qh, kg) * d_ref[2 * j + half]
                    halves.append(_dot(att.astype(BF16), vg))
                o = jnp.where(lo, halves[0], halves[1])
                qf = qg.astype(F32)
                qd = jnp.concatenate([(qf * qdf_ref[:, gs]).astype(BF16),
                                      (qf * qdb_ref[:, gs]).astype(BF16)], axis=1)
                s_cat = jnp.concatenate([stf[c, j], stb[c + 1, j]], axis=0).astype(BF16)
                o = o + _dot(qd, s_cat)
                o2 = o * o
                ms = jnp.where(lo, jnp.sum(jnp.where(lo, o2, 0.0), axis=-1, keepdims=True),
                               jnp.sum(jnp.where(lo, 0.0, o2), axis=-1, keepdims=True)) * (1.0 / RET_DV)
                gate = gr[0, pl.ds(c0, C), gs].astype(F32)
                y = o * lax.rsqrt(ms + NORM_EPS) * (gate * jax.nn.sigmoid(gate))
                orf[0, pl.ds(c0, C), gs] = y.astype(orf.dtype)
            return 0
        lax.fori_loop(0, n, body, 0, unroll=min(4, n))

    out_pass(q_ref, k_ref, v_ref, g_ref, o_ref, sf, sb, nc)
    if ctx_out:
        out_pass(qx_ref, kx_ref, vx_ref, gx_ref, ox_ref, sfx, sbx, ncx)


def _retention(q, k, v, g, qx, kx, vx, gx, tabs, ctx_out):
    B, L, W = q.shape
    Lc = qx.shape[1]
    nc, ncx = L // RET_CHUNK, Lc // RET_CHUNK
    lat = pl.BlockSpec((1, L, W), lambda b: (b, 0, 0))
    cx = pl.BlockSpec((1, Lc, W), lambda b: (b, 0, 0))
    full = lambda a: pl.BlockSpec(a.shape, lambda b: (0,) * a.ndim)
    out_shape = [jax.ShapeDtypeStruct((B, L, W), BF16)]
    out_specs = [lat]
    if ctx_out:
        out_shape.append(jax.ShapeDtypeStruct((B, Lc, W), BF16))
        out_specs.append(cx)
    res = pl.pallas_call(
        functools.partial(_ret_kernel, ctx_out=ctx_out),
        out_shape=out_shape,
        grid=(B,),
        in_specs=[lat, lat, lat, lat, cx, cx, cx, cx] + [full(t) for t in tabs],
        out_specs=out_specs,
        scratch_shapes=[pltpu.VMEM((n + 1, RET_HEADS // 2, LANES, LANES), F32) for n in (nc, nc, ncx, ncx)],
        compiler_params=_cparams(("arbitrary",)),
        name="retention",
    )(q, k, v, g, qx, kx, vx, gx, *tabs)
    return (res[0], res[1]) if ctx_out else (res[0], None)


def _swiglu_act(g, u):
    h = 0.5 * g
    return ((h + h * jnp.tanh(h)) * u).astype(BF16)


def _swiglu_chunks(x, wg, wu, wd, bounds):
    acc = None
    for c0, c1 in bounds:
        part = _dot(_swiglu_act(_dot(x, wg[:, c0:c1]), _dot(x, wu[:, c0:c1])), wd[c0:c1, :])
        acc = part if acc is None else acc + part
    return acc


def _chunk_bounds(total, size):
    return tuple((c, min(c + size, total)) for c in range(0, total, size))


def _split_hi_lo(a):
    hi = a.astype(BF16)
    return hi, (a - hi.astype(F32)).astype(BF16)


def _mix_out(oa_ref, ob_ref, oc_ref, x_ref, g1_ref, a2_ref, sh2_ref, wo_ref):
    y = (_dot(oa_ref[0], wo_ref[0:256, :]) + _dot(ob_ref[0], wo_ref[256:768, :])
         + _dot(oc_ref[0], wo_ref[768:1024, :]))
    x1 = x_ref[0] + g1_ref[0] * y
    h2 = x1 * lax.rsqrt(jnp.mean(x1 * x1, axis=-1, keepdims=True) + NORM_EPS) * a2_ref[0] + sh2_ref[0]
    return x1, h2


def _out_ffn_kernel(oa_ref, ob_ref, oc_ref, x_ref, g1_ref, a2_ref, sh2_ref, g2_ref, wo_ref,
                    wg_ref, wu_ref, wd_ref, o_ref, *, bounds):
    x1, h2 = _mix_out(oa_ref, ob_ref, oc_ref, x_ref, g1_ref, a2_ref, sh2_ref, wo_ref)
    o_ref[0] = x1 + g2_ref[0] * _swiglu_chunks(h2.astype(BF16), wg_ref, wu_ref, wd_ref, bounds)


def _out_ffn(oa, ob, oc, x, g1, a2, sh2, g2, wo, wg, wu, wd, tm):
    B, L, D = x.shape
    tm = min(tm, L)
    bm = (lambda b: b) if g1.shape[0] == B else (lambda b: 0)
    tok = lambda w: pl.BlockSpec((1, tm, w), lambda b, i: (b, i, 0))
    vec = pl.BlockSpec((1, 1, D), lambda b, i: (bm(b), 0, 0))
    wspec = lambda w: pl.BlockSpec(w.shape, lambda b, i: (0, 0), pipeline_mode=pl.Buffered(1))
    return pl.pallas_call(
        functools.partial(_out_ffn_kernel, bounds=_chunk_bounds(wg.shape[1], TF_SUB)),
        out_shape=jax.ShapeDtypeStruct((B, L, D), F32),
        grid=(B, L // tm),
        in_specs=[tok(256), tok(512), tok(256), tok(D), vec, vec, vec, vec,
                  wspec(wo), wspec(wg), wspec(wu), wspec(wd)],
        out_specs=tok(D),
        compiler_params=_cparams(("arbitrary", "arbitrary")),
        name="out_proj_dense_ffn",
    )(oa, ob, oc, x, g1, a2, sh2, g2, wo, wg, wu, wd)


def _out_proj_kernel(oa_ref, ob_ref, oc_ref, x_ref, g1_ref, a2_ref, sh2_ref, wo_ref, r_ref, tri_ref,
                     x1_ref, h2_ref, rt_ref, rtt_ref, cnt_ref, base_ref):
    @pl.when((pl.program_id(0) == 0) & (pl.program_id(1) == 0))
    def _():
        base_ref[...] = jnp.zeros_like(base_ref)

    x1, h2 = _mix_out(oa_ref, ob_ref, oc_ref, x_ref, g1_ref, a2_ref, sh2_ref, wo_ref)
    x1_ref[0] = x1
    h2_ref[0] = h2
    tm = h2.shape[0]
    h_hi, h_lo = _split_hi_lo(h2)
    prod = _dot(jnp.concatenate([h_hi, h_lo], axis=0), r_ref[...])
    logits = prod[:tm] + pltpu.roll(prod[:tm], LANES - N_EXPERTS, 1) + prod[tm:]
    lane = _lane_iota(logits.shape)
    logits = jnp.where(lane < N_EXPERTS, logits, -jnp.inf)
    v1 = jnp.max(logits, axis=-1, keepdims=True)
    i1 = jnp.min(jnp.where(logits == v1, lane, LANES), axis=-1, keepdims=True)
    rest_l = jnp.where(lane == i1, -jnp.inf, logits)
    v2 = jnp.max(rest_l, axis=-1, keepdims=True)
    i2 = jnp.min(jnp.where(rest_l == v2, lane, LANES), axis=-1, keepdims=True)
    e2 = jnp.exp(v2 - v1)
    w1 = 1.0 / (1.0 + e2)
    w2 = e2 * w1
    oh1 = jnp.where(lane == i1, 1.0, 0.0)
    oh2 = jnp.where(lane == i2, 1.0, 0.0)
    both = oh1 + oh2
    seen = _dot(tri_ref[...], both.astype(BF16)) + base_ref[...]
    rank1 = jnp.sum(oh1 * seen, axis=-1, keepdims=True)
    rank2 = jnp.sum(oh2 * seen, axis=-1, keepdims=True)
    base_ref[...] += jnp.sum(both, axis=0, keepdims=True)
    cnt_ref[...] = base_ref[...]
    vals = (i1.astype(F32), i2.astype(F32), w1, w2, rank1, rank2)
    row = jnp.zeros_like(logits)
    for k, v in enumerate(vals):
        row = jnp.where(lane == k, v, row)
    rt_ref[0] = row
    rtt_ref[0] = row.T[:ROUTE_FIELDS]


ROUTE_FIELDS = 8


def _out_proj_route(oa, ob, oc, x, g1, a2, sh2, wo, router, tm):
    B, L, D = x.shape
    tm = min(tm, L)
    nt = L // tm
    tok = lambda w: pl.BlockSpec((1, tm, w), lambda b, i: (b, i, 0))
    vec = pl.BlockSpec((1, 1, D), lambda b, i: (b, 0, 0))
    const = lambda a: pl.BlockSpec(a.shape, lambda b, i: (0, 0))
    tri = jnp.asarray(np.tril(np.ones((tm, tm), np.float32), -1), BF16)
    return pl.pallas_call(
        _out_proj_kernel,
        out_shape=[jax.ShapeDtypeStruct((B, L, D), F32), jax.ShapeDtypeStruct((B, L, D), F32),
                   jax.ShapeDtypeStruct((B, L, LANES), F32),
                   jax.ShapeDtypeStruct((B * nt, ROUTE_FIELDS, tm), F32), jax.ShapeDtypeStruct((1, LANES), F32)],
        grid=(B, nt),
        in_specs=[tok(256), tok(512), tok(256), tok(D), vec, vec, vec, const(wo), const(router), const(tri)],
        out_specs=[tok(D), tok(D), tok(LANES),
                   pl.BlockSpec((1, ROUTE_FIELDS, tm), lambda b, i: (b * nt + i, 0, 0)),
                   pl.BlockSpec((1, LANES), lambda b, i: (0, 0))],
        scratch_shapes=[pltpu.VMEM((1, LANES), F32)],
        compiler_params=_cparams(("arbitrary", "arbitrary")),
        name="out_proj_route",
    )(oa, ob, oc, x, g1, a2, sh2, wo, router, tri)


ROW_UNROLL = 16
ZERO_ROWS = 256


def _dispatch_kernel(ends_ref, pad_ref, d0_ref, d1_ref, h_ref, wg_ref, wu_ref, wd_ref,
                     o_ref, wgb_ref, wub_ref, wdb_ref, zeros_ref, sem, zsem, *, tm):
    rows = h_ref.shape[0]
    zr = zeros_ref.shape[0]
    for src, dst in ((wg_ref, wgb_ref), (wu_ref, wub_ref), (wd_ref, wdb_ref)):
        dst[...] = src[...].astype(dst.dtype)

    @pl.when(pl.program_id(0) == 0)
    def _():
        zeros_ref[...] = jnp.zeros_like(zeros_ref)

        def clear(row0, part):
            dst = o_ref.at[pl.ds(pl.multiple_of(row0, tm) + part * zr, zr)]
            return pltpu.make_async_copy(zeros_ref, dst, zsem)
        jobs = [(pad_ref[e] > 0, ends_ref[e] - tm) for e in range(N_EXPERTS)]
        total = ends_ref[N_EXPERTS - 1]
        jobs += [(total + t * tm < o_ref.shape[0], total + t * tm) for t in range(N_EXPERTS)]
        for cond, row0 in jobs:
            @pl.when(cond)
            def _():
                for part in range(tm // zr):
                    clear(row0, part).start()
        for cond, row0 in jobs:
            @pl.when(cond)
            def _():
                for part in range(tm // zr):
                    clear(row0, part).wait()

    def start(r, _):
        pltpu.make_async_copy(h_ref.at[pl.ds(r, 1)], o_ref.at[pl.ds(d0_ref[0, 0, r], 1)], sem).start(priority=0)
        pltpu.make_async_copy(h_ref.at[pl.ds(r, 1)], o_ref.at[pl.ds(d1_ref[0, 0, r], 1)], sem).start(priority=1)
        return 0

    lax.fori_loop(0, rows, start, 0, unroll=ROW_UNROLL)
    for _ in range(2):
        pltpu.make_async_copy(h_ref, o_ref.at[pl.ds(0, rows)], sem).wait()


def _dispatch(h, d0, d1, ends, padding, n_slots, tm, td, wg, wu, wd):
    N, D = h.shape
    steps = N // td
    idx_spec = pl.BlockSpec((1, 1, td), lambda i, *_: (i, 0, 0), memory_space=pltpu.SMEM)
    flat = [w.reshape(-1, w.shape[-1]) for w in (wg, wu, wd)]
    w_specs = [pl.BlockSpec((w.shape[0] // steps, w.shape[1]), lambda i, *_: (i, 0)) for w in flat]
    res = pl.pallas_call(
        functools.partial(_dispatch_kernel, tm=tm),
        out_shape=[jax.ShapeDtypeStruct((n_slots, D), h.dtype)]
        + [jax.ShapeDtypeStruct(w.shape, BF16) for w in flat],
        grid_spec=pltpu.PrefetchScalarGridSpec(
            num_scalar_prefetch=2,
            grid=(steps,),
            in_specs=[idx_spec, idx_spec, pl.BlockSpec((td, D), lambda i, *_: (i, 0))] + w_specs,
            out_specs=[pl.BlockSpec(memory_space=pl.ANY)] + w_specs,
            scratch_shapes=[pltpu.VMEM((ZERO_ROWS, D), h.dtype), pltpu.SemaphoreType.DMA,
                            pltpu.SemaphoreType.DMA]),
        compiler_params=_cparams(("arbitrary",)),
        name="moe_dispatch",
    )(ends, padding, d0.reshape(steps, 1, td), d1.reshape(steps, 1, td), h, *flat)
    return res[0], res[1].reshape(wg.shape), res[2].reshape(wu.shape), res[3].reshape(wd.shape)


def _gmm_kernel(te_ref, nu_ref, x_ref, wg_ref, wu_ref, wd_ref, o_ref, *, bounds):
    @pl.when(pl.program_id(0) < nu_ref[0])
    def _():
        o_ref[...] = _swiglu_chunks(x_ref[...].astype(BF16), wg_ref.at[0], wu_ref.at[0], wd_ref.at[0], bounds)

    @pl.when(pl.program_id(0) >= nu_ref[0])
    def _():
        o_ref[...] = jnp.zeros_like(o_ref)


def _grouped_swiglu(xs, tile_expert, n_used, wg, wu, wd, tm):
    P, D = xs.shape
    E, _, F = wg.shape
    resident = lambda shape, bufs=1: pl.BlockSpec((1,) + shape, lambda i, te, nu: (te[i], 0, 0),
                                                  pipeline_mode=pl.Buffered(bufs))
    return pl.pallas_call(
        functools.partial(_gmm_kernel, bounds=_chunk_bounds(F, TF_SUB)),
        out_shape=jax.ShapeDtypeStruct((P, D), F32),
        grid_spec=pltpu.PrefetchScalarGridSpec(
            num_scalar_prefetch=2,
            grid=(P // tm,),
            in_specs=[pl.BlockSpec((tm, D), lambda i, te, nu: (jnp.minimum(i, nu[0] - 1), 0)),
                      resident((D, F)), resident((D, F)), resident((F, D), 2)],
            out_specs=pl.BlockSpec((tm, D), lambda i, te, nu: (i, 0))),
        compiler_params=_cparams(("arbitrary",)),
        name="moe_grouped_swiglu",
    )(tile_expert, n_used, xs, wg, wu, wd)


def _combine_kernel(d0_ref, d1_ref, y_ref, x_ref, g2_ref, rt_ref, fg_ref, o_ref, buf, sem):
    rows = o_ref.shape[1]

    def start(r, _):
        pltpu.make_async_copy(y_ref.at[pl.ds(d0_ref[0, 0, r], 1)], buf.at[0, pl.ds(r, 1)], sem).start(priority=0)
        pltpu.make_async_copy(y_ref.at[pl.ds(d1_ref[0, 0, r], 1)], buf.at[1, pl.ds(r, 1)], sem).start(priority=1)
        return 0

    lax.fori_loop(0, rows, start, 0, unroll=ROW_UNROLL)
    for k in range(2):
        pltpu.make_async_copy(y_ref.at[pl.ds(0, rows)], buf.at[k], sem).wait()
    rt = rt_ref[0]
    moe = rt[:, 2:3] * buf[0] + rt[:, 3:4] * buf[1]
    x2 = x_ref[0] + g2_ref[0] * moe
    o_ref[0] = x2 * lax.rsqrt(jnp.mean(x2 * x2, axis=-1, keepdims=True) + NORM_EPS) * fg_ref[...]


def _combine(y, d0, d1, x1, g2, rt, final_g, tc):
    B, L, D = x1.shape
    tc = min(tc, L)
    nt = L // tc
    idx_spec = pl.BlockSpec((1, 1, tc), lambda b, i: (b * nt + i, 0, 0), memory_space=pltpu.SMEM)
    tok = lambda w: pl.BlockSpec((1, tc, w), lambda b, i: (b, i, 0))
    return pl.pallas_call(
        _combine_kernel,
        out_shape=jax.ShapeDtypeStruct((B, L, D), F32),
        grid=(B, nt),
        in_specs=[idx_spec, idx_spec, pl.BlockSpec(memory_space=pl.ANY), tok(D),
                  pl.BlockSpec((1, 1, D), lambda b, i: (b, 0, 0)), tok(LANES),
                  pl.BlockSpec((1, D), lambda b, i: (0, 0))],
        out_specs=tok(D),
        scratch_shapes=[pltpu.VMEM((2, tc, D), F32), pltpu.SemaphoreType.DMA],
        compiler_params=_cparams(("arbitrary", "arbitrary")),
        name="moe_combine_norm",
    )(d0.reshape(B * nt, 1, tc), d1.reshape(B * nt, 1, tc), y, x1, g2, rt, final_g.reshape(1, D))


def _routing(rtt, counts, tm):
    field = lambda k: rtt[:, k, :].reshape(-1).astype(jnp.int32)
    n = rtt.shape[0] * rtt.shape[2]
    counts = counts[0, :N_EXPERTS].astype(jnp.int32)
    padded = ((counts + tm - 1) // tm) * tm
    ends = jnp.cumsum(padded)
    offs = ends - padded

    def slot(e, rank):
        start = jnp.zeros_like(e)
        for j in range(N_EXPERTS):
            start = jnp.where(e == j, offs[j], start)
        return start + rank
    dest = (slot(field(0), field(4)), slot(field(1), field(5)))
    n_slots = 2 * n + N_EXPERTS * tm
    tile_start = jnp.arange(n_slots // tm, dtype=jnp.int32) * tm
    n_used = (ends[-1] // tm).astype(jnp.int32)
    te = jnp.sum((tile_start[:, None] >= ends[None, :]).astype(jnp.int32), axis=1)
    last = jnp.sum((ends[-1] - tm >= ends).astype(jnp.int32))
    te = jnp.where(tile_start < ends[-1], te, last).astype(jnp.int32)
    return (n_slots, te, n_used.reshape(1), dest[0], dest[1], ends.astype(jnp.int32),
            (padded - counts).astype(jnp.int32))


TM_IN = 512
TQ_MLA = 256
TM_OUT = 512
TM_FFN = 512
TM_MOE = 512
TF_SUB = 512
TD_DISPATCH = 512
TC_COMBINE = 1024


def kernel(x, c, ctx, c_ctx, w_mod, b_mod, norm1_g, norm2_g, w_in, mla_q_norm, mla_w_uq, mla_kv_norm,
           mla_w_ukv, swa_sink, ret_decay_fwd, ret_decay_bwd, w_out, ffn_w_gate, ffn_w_up, ffn_w_down,
           moe_router, moe_w_gate, moe_w_up, moe_w_down, final_norm_g):
    B, L, D = x.shape
    Lc = ctx.shape[1]
    depth = w_mod.shape[0]
    xc = ctx

    cond = jnp.concatenate([c, c_ctx[None], jnp.zeros((16 - B - 1, D), F32)], axis=0)
    mod_all = _modulation(cond, w_mod, b_mod)
    tables = _rope_tables(L)

    for layer in range(depth):
        last = layer == depth - 1
        mod = mod_all[layer].reshape(16, 6, 1, D)
        sh1, sc1, g1, sh2, sc2, g2 = (mod[:B, j] for j in range(6))
        sh1x, sc1x, g1x, sh2x, sc2x, g2x = (mod[B:B + 1, j] for j in range(6))
        n1, n2 = norm1_g[layer], norm2_g[layer]

        wts = _prep_in_weights(w_in[layer], mla_q_norm[layer], mla_w_uq[layer],
                               mla_kv_norm[layer], mla_w_ukv[layer])
        lat = _in_proj(x, n1 * (1.0 + sc1), sh1, wts, tables, TM_IN)
        cx = _in_proj(xc, n1 * (1.0 + sc1x), sh1x, wts, None, TM_IN)
        qm, km, vm, sq, sk, sv, rq, rk, rv, rg = lat
        qmx, kmx, vmx, sqx, skx, svx, rqx, rkx, rvx, rgx = cx

        o_a = _mla_attention(qm, [(km, vm), (kmx, vmx)], TQ_MLA)
        sink = swa_sink[layer].astype(F32) * LOG2E
        o_b = _swa_attention(sink, sq, sk, sv, skx, svx, True)
        rtabs = _ret_tables(ret_decay_fwd[layer], ret_decay_bwd[layer])
        o_c, oc_c = _retention(rq, rk, rv, rg, rqx, rkx, rvx, rgx, rtabs, not last)
        wo = w_out[layer].astype(BF16)

        if layer % 2 == 0:
            i = layer // 2
            wg, wu, wd = (ffn_w_gate[i].astype(BF16), ffn_w_up[i].astype(BF16),
                          ffn_w_down[i].astype(BF16))
            x_next = _out_ffn(o_a, o_b, o_c, x, g1, n2 * (1.0 + sc2), sh2, g2, wo, wg, wu, wd, TM_FFN)
        else:
            i = layer // 2
            r_hi = moe_router[i].astype(BF16)
            r_lo = (moe_router[i] - r_hi.astype(F32)).astype(BF16)
            router = jnp.pad(jnp.concatenate([r_hi, r_lo], axis=1), ((0, 0), (0, LANES - 2 * N_EXPERTS)))
            x1, h2, rt, rtt, counts = _out_proj_route(o_a, o_b, o_c, x, g1, n2 * (1.0 + sc2), sh2, wo, router,
                                                      TM_OUT)
            n_slots, te, n_used, d0, d1, ends, padding = _routing(rtt, counts, TM_MOE)
            xs, ewg, ewu, ewd = _dispatch(h2.reshape(B * L, D), d0, d1, ends, padding, n_slots, TM_MOE,
                                          TD_DISPATCH, moe_w_gate[i], moe_w_up[i], moe_w_down[i])
            y = _grouped_swiglu(xs, te, n_used, ewg, ewu, ewd, TM_MOE)
            if last:
                return _combine(y, d0, d1, x1, g2, rt, final_norm_g, TC_COMBINE)
            raise NotImplementedError("expert layer is only supported as the last layer")

        if not last:
            oc_a = _mla_attention(qmx, [(kmx, vmx)], TQ_MLA)
            oc_b = _swa_attention(sink, sqx, skx, svx, skx, svx, False)
            xc = _out_ffn(oc_a, oc_b, oc_c, xc, g1x, n2 * (1.0 + sc2x), sh2x, g2x, wo, wg, wu, wd, TM_FFN)
        x = x_next
    raise NotImplementedError("trunk must end with the expert layer")
```
